```python
import math
import jax, jax.numpy as jnp
from jax import lax
import numpy as np


D_MODEL = 1024
BATCH = 1
SEQ = 16384
DEPTH = 2

N_MIXERS = 2
N_A_LAYERS = (DEPTH + 1) // 2
N_B_LAYERS = DEPTH // 2
N_SUBLAYERS = 3
RMS_EPS = 1e-6

D_FF = 2816

GM_D = 3 * D_MODEL
GM_GROUPS = 16
GM_GROUP_DIM = GM_D // GM_GROUPS
GM_CHUNK = 128

MB_HEADS = 16
MB_HEAD_DIM = D_MODEL // MB_HEADS
MB_BLOCK = 256
MB_TOPK = 3
MB_QBLOCK = 128
MB_SCALE = MB_HEAD_DIM ** -0.5

T5_NUM_BUCKETS = 32
T5_MAX_EXACT = T5_NUM_BUCKETS // 2
T5_MAX_DISTANCE = 128

NEG_INF = -1e30

kernel_name = "hybrid_gmlp_moba_macaron_adaln"


def rmsnorm(x, g):
    xf = x.astype(jnp.float32)
    y = xf * lax.rsqrt(jnp.mean(xf * xf, axis=-1, keepdims=True) + RMS_EPS)
    return (y * g.astype(jnp.float32)).astype(x.dtype)


def adaln(x, g, shift, scale):
    return rmsnorm(x, g) * (1 + scale[:, None, :]) + shift[:, None, :]


def swiglu(h, w_in, w_out):
    gate, up = jnp.split(h @ w_in, 2, axis=-1)
    return (jax.nn.silu(gate) * up) @ w_out


def t5_bucket(rel):
    n = jnp.maximum(rel, 0)
    is_small = n < T5_MAX_EXACT
    nf = jnp.maximum(n, T5_MAX_EXACT).astype(jnp.float32)
    large = T5_MAX_EXACT + (jnp.log(nf / T5_MAX_EXACT)
                            / math.log(T5_MAX_DISTANCE / T5_MAX_EXACT)
                            * (T5_NUM_BUCKETS - T5_MAX_EXACT)).astype(jnp.int32)
    large = jnp.minimum(large, T5_NUM_BUCKETS - 1)
    return jnp.where(is_small, n, large)


def gmlp_mixer(h, w_in, v_norm, w_s, b_s, w_out):
    B, S, _ = h.shape
    z = jax.nn.gelu(h @ w_in, approximate=False)
    u, v = jnp.split(z, 2, axis=-1)
    v = rmsnorm(v, v_norm).reshape(B, S // GM_CHUNK, GM_CHUNK, GM_GROUPS, GM_GROUP_DIM)
    causal = jnp.tril(jnp.ones((GM_CHUNK, GM_CHUNK), dtype=bool))
    w_sm = jnp.where(causal, w_s, 0)
    sv = jnp.einsum("gts,bcsgd->bctgd", w_sm, v) + b_s.T[:, :, None]
    return (u * sv.reshape(B, S, GM_D)) @ w_out


def moba_mixer(h, w_qkv, w_o, rel_bias):
    B, S, _ = h.shape
    qkv = (h @ w_qkv).reshape(B, S, 3, MB_HEADS, MB_HEAD_DIM)
    q = qkv[:, :, 0].transpose(0, 2, 1, 3)
    k = qkv[:, :, 1].transpose(0, 2, 1, 3)
    v = qkv[:, :, 2].transpose(0, 2, 1, 3)
    nb = -(-S // MB_BLOCK)
    pad = ((0, 0), (0, 0), (0, nb * MB_BLOCK - S), (0, 0))
    k = jnp.pad(k, pad)
    v = jnp.pad(v, pad)
    k_blocks = k.reshape(B, MB_HEADS, nb, MB_BLOCK, MB_HEAD_DIM)
    v_blocks = v.reshape(B, MB_HEADS, nb, MB_BLOCK, MB_HEAD_DIM)
    k_mean = jnp.mean(k_blocks.astype(jnp.float32), axis=3)
    topk = min(MB_TOPK, nb)
    b_ix = jnp.arange(B)[:, None, None, None]
    h_ix = jnp.arange(MB_HEADS)[None, :, None, None]
    blk_ar = jnp.arange(nb)
    slot_ar = jnp.arange(topk)
    key_ar = jnp.arange(MB_BLOCK)
    q_ar = jnp.arange(MB_QBLOCK)

    def one_query_block(qb):
        p0 = qb * MB_QBLOCK
        j = p0 // MB_BLOCK
        q_blk = lax.dynamic_slice_in_dim(q, p0, MB_QBLOCK, axis=2)
        q_pos = p0 + q_ar
        gate = jnp.einsum("bhqd,bhnd->bhqn", q_blk.astype(jnp.float32), k_mean)
        gate = jnp.where(blk_ar < j, gate, NEG_INF)
        _, idx = lax.top_k(gate, topk)
        k_sel = k_blocks[b_ix, h_ix, idx]
        v_sel = v_blocks[b_ix, h_ix, idx]
        logit_sel = jnp.einsum("bhqd,bhqrkd->bhqrk", q_blk, k_sel).astype(jnp.float32) * MB_SCALE
        k_pos_sel = idx[..., None] * MB_BLOCK + key_ar
        bucket_sel = t5_bucket(q_pos[:, None, None] - k_pos_sel)
        logit_sel = logit_sel + rel_bias[bucket_sel, h_ix[..., None]].astype(jnp.float32)
        logit_sel = jnp.where((slot_ar < j)[:, None], logit_sel, NEG_INF)
        k_own = lax.dynamic_slice_in_dim(k, j * MB_BLOCK, MB_BLOCK, axis=2)
        v_own = lax.dynamic_slice_in_dim(v, j * MB_BLOCK, MB_BLOCK, axis=2)
        logit_own = jnp.einsum("bhqd,bhkd->bhqk", q_blk, k_own).astype(jnp.float32) * MB_SCALE
        rel_own = q_pos[:, None] - (j * MB_BLOCK + key_ar)[None, :]
        logit_own = logit_own + rel_bias[t5_bucket(rel_own)].transpose(2, 0, 1).astype(jnp.float32)
        logit_own = jnp.where(rel_own >= 0, logit_own, NEG_INF)
        logits = jnp.concatenate(
            [logit_sel.reshape(B, MB_HEADS, MB_QBLOCK, topk * MB_BLOCK), logit_own], axis=-1)
        p = jax.nn.softmax(logits, axis=-1).astype(v.dtype)
        p_sel = p[..., : topk * MB_BLOCK].reshape(B, MB_HEADS, MB_QBLOCK, topk, MB_BLOCK)
        p_own = p[..., topk * MB_BLOCK:]
        return (jnp.einsum("bhqrk,bhqrkd->bhqd", p_sel, v_sel)
                + jnp.einsum("bhqk,bhkd->bhqd", p_own, v_own))

    o = lax.map(one_query_block, jnp.arange(S // MB_QBLOCK))
    o = o.transpose(1, 0, 3, 2, 4).reshape(B, S, MB_HEADS * MB_HEAD_DIM)
    return o @ w_o


def setup_inputs(seed: int = 0) -> dict:
    key = jax.random.key(seed)
    ks = jax.random.split(key, 16)
    f32 = jnp.float32

    def nrm(k, shape, scale):
        return jax.random.normal(k, shape, f32) * scale

    return {
        "x": nrm(ks[0], (BATCH, SEQ, D_MODEL), 1.0),
        "c": nrm(ks[1], (BATCH, D_MODEL), 1.0),
        "rel_bias": nrm(ks[2], (T5_NUM_BUCKETS, MB_HEADS), 0.5),
        "mod_w": nrm(ks[3], (DEPTH, D_MODEL, N_SUBLAYERS * 3 * D_MODEL), D_MODEL ** -0.5),
        "mod_b": nrm(ks[4], (DEPTH, N_SUBLAYERS * 3 * D_MODEL), 0.02),
        "norm_g": 1.0 + nrm(ks[5], (DEPTH, N_SUBLAYERS, D_MODEL), 0.02),
        "ffn_w_in": nrm(ks[6], (DEPTH, 2, D_MODEL, 2 * D_FF), D_MODEL ** -0.5),
        "ffn_w_out": nrm(ks[7], (DEPTH, 2, D_FF, D_MODEL), D_FF ** -0.5),
        "gmlp_w_in": nrm(ks[8], (N_A_LAYERS, D_MODEL, 2 * GM_D), D_MODEL ** -0.5),
        "gmlp_v_norm": 1.0 + nrm(ks[9], (N_A_LAYERS, GM_D), 0.02),
        "gmlp_w_s": nrm(ks[10], (N_A_LAYERS, GM_GROUPS, GM_CHUNK, GM_CHUNK), GM_CHUNK ** -0.5),
        "gmlp_b_s": 1.0 + nrm(ks[11], (N_A_LAYERS, GM_GROUPS, GM_CHUNK), 0.02),
        "gmlp_w_out": nrm(ks[12], (N_A_LAYERS, GM_D, D_MODEL), GM_D ** -0.5),
        "moba_w_qkv": nrm(ks[13], (N_B_LAYERS, D_MODEL, 3 * D_MODEL), D_MODEL ** -0.5),
        "moba_w_o": nrm(ks[14], (N_B_LAYERS, D_MODEL, D_MODEL), D_MODEL ** -0.5),
        "final_norm": 1.0 + nrm(ks[15], (D_MODEL,), 0.02),
    }


def reference(x, c, rel_bias, mod_w, mod_b, norm_g, ffn_w_in, ffn_w_out,
              gmlp_w_in, gmlp_v_norm, gmlp_w_s, gmlp_b_s, gmlp_w_out,
              moba_w_qkv, moba_w_o, final_norm):
    B = x.shape[0]
    c_act = jax.nn.silu(c)
    for i in range(DEPTH):
        mod = (c_act @ mod_w[i] + mod_b[i]).reshape(B, N_SUBLAYERS, 3, D_MODEL)
        h = adaln(x, norm_g[i, 0], mod[:, 0, 0], mod[:, 0, 1])
        x = x + 0.5 * mod[:, 0, 2][:, None, :] * swiglu(h, ffn_w_in[i, 0], ffn_w_out[i, 0])
        h = adaln(x, norm_g[i, 1], mod[:, 1, 0], mod[:, 1, 1])
        li = i // N_MIXERS
        if i % N_MIXERS == 0:
            y = gmlp_mixer(h, gmlp_w_in[li], gmlp_v_norm[li], gmlp_w_s[li], gmlp_b_s[li], gmlp_w_out[li])
        else:
            y = moba_mixer(h, moba_w_qkv[li], moba_w_o[li], rel_bias)
        x = x + mod[:, 1, 2][:, None, :] * y
        h = adaln(x, norm_g[i, 2], mod[:, 2, 0], mod[:, 2, 1])
        x = x + 0.5 * mod[:, 2, 2][:, None, :] * swiglu(h, ffn_w_in[i, 1], ffn_w_out[i, 1])
    return rmsnorm(x, final_norm)
```

```python
import functools
import math

import numpy as np
import jax
import jax.numpy as jnp
from jax import lax
from jax.experimental import pallas as pl
from jax.experimental.pallas import tpu as pltpu

F32 = jnp.float32
BF16 = jnp.bfloat16

D_MODEL = 1024
SEQ = 16384
DEPTH = 2
N_SUBLAYERS = 3
RMS_EPS = 1e-6
D_FF = 2816

GM_D = 3 * D_MODEL
GM_GROUPS = 16
GM_GROUP_DIM = GM_D // GM_GROUPS
GM_CHUNK = 128
GM_PAIR = 2 * GM_GROUP_DIM

MB_HEADS = 16
MB_HEAD_DIM = 64
MB_BLOCK = 256
MB_TOPK = 3
MB_SCALE = MB_HEAD_DIM ** -0.5
N_KBLOCKS = SEQ // MB_BLOCK

T5_NUM_BUCKETS = 32
T5_MAX_EXACT = 16
T5_MAX_DISTANCE = 128
NEG_INF = -1e30
BELOW_NEG_INF = -3e38

LANES = 128
VMEM_LIMIT = 56 * 1024 * 1024

MOD_TN = 1536
FFN_TM = 512
FFN_CHUNK = 1408
GM_TM = 256
GM_VCHUNK = 768
QKV_TM = 512
ATT_TQ = MB_BLOCK
PROJ_TM = 512


def _resident(shape):
    nd = len(shape)
    return pl.BlockSpec(shape, lambda *_: (0,) * nd, pipeline_mode=pl.Buffered(1))


def _params(*sem):
    return pltpu.CompilerParams(dimension_semantics=sem, vmem_limit_bytes=VMEM_LIMIT)


def _rms(x):
    return x * lax.rsqrt(jnp.mean(x * x, axis=-1, keepdims=True) + RMS_EPS)


def _adaln(x, vec_ref):
    y = _rms(x) * vec_ref[0:1, :]
    return y * (1.0 + vec_ref[2:3, :]) + vec_ref[1:2, :]


def _gelu(x):
    return 0.5 * x * (1.0 + lax.erf(x * np.float32(math.sqrt(0.5))))


def _mod_kernel(c_ref, w_ref, b_ref, o_ref):
    c = c_ref[...]
    c_act = c * jax.nn.sigmoid(c)
    o_ref[...] = jnp.sum(c_act * w_ref[...], axis=0, keepdims=True) + b_ref[...]


def _modulation(c_col, mod_w, mod_b):
    n = mod_w.shape[-1]
    return pl.pallas_call(
        _mod_kernel,
        grid=(DEPTH, n // MOD_TN),
        in_specs=[
            pl.BlockSpec((D_MODEL, 1), lambda i, j: (0, 0)),
            pl.BlockSpec((None, D_MODEL, MOD_TN), lambda i, j: (i, 0, j)),
            pl.BlockSpec((None, 1, MOD_TN), lambda i, j: (i, 0, j)),
        ],
        out_specs=pl.BlockSpec((None, 1, MOD_TN), lambda i, j: (i, 0, j)),
        out_shape=jax.ShapeDtypeStruct((DEPTH, 1, n), F32),
        compiler_params=_params("arbitrary", "arbitrary"),
        name="modulation",
    )(c_col, mod_w, mod_b.reshape(DEPTH, 1, n))


def _ffn_kernel(x_ref, vec_ref, win_ref, wout_ref, fin_ref, o_ref, act_ref, *, final):
    x = x_ref[...]
    h = _adaln(x, vec_ref).astype(BF16)
    for c in range(D_FF // FFN_CHUNK):
        lo = c * FFN_CHUNK
        g = jnp.dot(h, win_ref[:, lo:lo + FFN_CHUNK], preferred_element_type=F32)
        u = jnp.dot(h, win_ref[:, D_FF + lo:D_FF + lo + FFN_CHUNK], preferred_element_type=F32)
        act_ref[:, lo:lo + FFN_CHUNK] = (g * jax.nn.sigmoid(g) * u).astype(BF16)
    y = jnp.dot(act_ref[...], wout_ref[...], preferred_element_type=F32)
    out = x + (0.5 * vec_ref[3:4, :]) * y
    if final:
        out = _rms(out) * fin_ref[...]
    o_ref[...] = out


def _ffn(x, vec, w_in, w_out, fin, final):
    return pl.pallas_call(
        functools.partial(_ffn_kernel, final=final),
        grid=(SEQ // FFN_TM,),
        in_specs=[
            pl.BlockSpec((FFN_TM, D_MODEL), lambda i: (i, 0)),
            _resident((8, D_MODEL)),
            _resident((D_MODEL, 2 * D_FF)),
            _resident((D_FF, D_MODEL)),
            _resident((1, D_MODEL)),
        ],
        out_specs=pl.BlockSpec((FFN_TM, D_MODEL), lambda i: (i, 0)),
        out_shape=jax.ShapeDtypeStruct((SEQ, D_MODEL), F32),
        scratch_shapes=[pltpu.VMEM((FFN_TM, D_FF), BF16)],
        compiler_params=_params("parallel"),
        name="ffn_final" if final else "ffn",
    )(x, vec, w_in, w_out, fin)


def _gmlp_kernel(x_ref, vec_ref, win_ref, vnorm_ref, ws_ref, bias_ref, wout_ref,
                 o_ref, v_ref, g_ref):
    x = x_ref[...]
    h = _adaln(x, vec_ref).astype(BF16)

    ssq = jnp.zeros((GM_TM, 1), F32)
    for c in range(GM_D // GM_VCHUNK):
        lo = c * GM_VCHUNK
        v = _gelu(jnp.dot(h, win_ref[:, GM_D + lo:GM_D + lo + GM_VCHUNK],
                          preferred_element_type=F32))
        v_ref[:, lo:lo + GM_VCHUNK] = v
        ssq = ssq + jnp.sum(v * v, axis=-1, keepdims=True)
    r = lax.rsqrt(ssq * (1.0 / GM_D) + RMS_EPS)

    row = lax.broadcasted_iota(jnp.int32, (GM_CHUNK, GM_CHUNK), 0)
    col = lax.broadcasted_iota(jnp.int32, (GM_CHUNK, GM_CHUNK), 1)
    causal = row >= col
    low_half = lax.broadcasted_iota(jnp.int32, (GM_CHUNK, LANES), 1) < (GM_GROUP_DIM - LANES)

    for p in range(GM_GROUPS // 2):
        lo = p * GM_PAIR
        vn = (v_ref[:, lo:lo + GM_PAIR] * r * vnorm_ref[:, lo:lo + GM_PAIR]).astype(BF16)
        u = _gelu(jnp.dot(h, win_ref[:, lo:lo + GM_PAIR], preferred_element_type=F32))
        w0 = jnp.where(causal, ws_ref[2 * p], jnp.zeros((), BF16))
        w1 = jnp.where(causal, ws_ref[2 * p + 1], jnp.zeros((), BF16))
        for c in range(GM_TM // GM_CHUNK):
            rows = slice(c * GM_CHUNK, (c + 1) * GM_CHUNK)
            vc = vn[rows]
            a = jnp.dot(w0, vc[:, :2 * LANES], preferred_element_type=F32)
            b = jnp.dot(w1, vc[:, LANES:], preferred_element_type=F32)
            mid = jnp.where(low_half, a[:, LANES:], b[:, :LANES])
            sv = jnp.concatenate([a[:, :LANES], mid, b[:, LANES:]], axis=1)
            sv = sv + bias_ref[:, lo:lo + GM_PAIR]
            g_ref[rows, lo:lo + GM_PAIR] = (u[rows] * sv).astype(BF16)

    y = jnp.dot(g_ref[...], wout_ref[...], preferred_element_type=F32)
    o_ref[...] = x + vec_ref[3:4, :] * y


def _gmlp(x, vec, w_in, v_norm, w_s, bias, w_out):
    return pl.pallas_call(
        _gmlp_kernel,
        grid=(SEQ // GM_TM,),
        in_specs=[
            pl.BlockSpec((GM_TM, D_MODEL), lambda i: (i, 0)),
            _resident((8, D_MODEL)),
            _resident((D_MODEL, 2 * GM_D)),
            _resident((1, GM_D)),
            _resident((GM_GROUPS, GM_CHUNK, GM_CHUNK)),
            _resident((GM_CHUNK, GM_D)),
            _resident((GM_D, D_MODEL)),
        ],
        out_specs=pl.BlockSpec((GM_TM, D_MODEL), lambda i: (i, 0)),
        out_shape=jax.ShapeDtypeStruct((SEQ, D_MODEL), F32),
        scratch_shapes=[pltpu.VMEM((GM_TM, GM_D), F32), pltpu.VMEM((GM_TM, GM_D), BF16)],
        compiler_params=_params("parallel"),
        name="gmlp",
    )(x, vec, w_in, v_norm, w_s, bias, w_out)


def _qkv_kernel(x_ref, vec_ref, w_ref, q_ref, k_ref, v_ref, kmean_ref):
    h = _adaln(x_ref[...], vec_ref).astype(BF16)
    q = jnp.dot(h, w_ref[:, :D_MODEL], preferred_element_type=F32)
    q_ref[...] = (q * MB_SCALE).astype(BF16)
    k = jnp.dot(h, w_ref[:, D_MODEL:2 * D_MODEL], preferred_element_type=F32)
    k_ref[...] = k.astype(BF16)
    for b in range(QKV_TM // MB_BLOCK):
        kmean_ref[b] = jnp.mean(k[b * MB_BLOCK:(b + 1) * MB_BLOCK], axis=0, keepdims=True)
    v = jnp.dot(h, w_ref[:, 2 * D_MODEL:], preferred_element_type=F32)
    v_ref[...] = v.astype(BF16)


def _qkv(x, vec, w_qkv):
    row = pl.BlockSpec((QKV_TM, D_MODEL), lambda i: (i, 0))
    nb = QKV_TM // MB_BLOCK
    return pl.pallas_call(
        _qkv_kernel,
        grid=(SEQ // QKV_TM,),
        in_specs=[row, _resident((8, D_MODEL)), _resident((D_MODEL, 3 * D_MODEL))],
        out_specs=[row, row, row, pl.BlockSpec((nb, 1, D_MODEL), lambda i: (i, 0, 0))],
        out_shape=[jax.ShapeDtypeStruct((SEQ, D_MODEL), BF16)] * 3
        + [jax.ShapeDtypeStruct((N_KBLOCKS, 1, D_MODEL), F32)],
        compiler_params=_params("parallel"),
        name="moba_qkv",
    )(x, vec, w_qkv)


def _attn_kernel(c31_ref, q_ref, k_ref, v_ref, kmean_ref, bown_ref, bprev_ref, o_ref):
    pair = pl.program_id(0)
    j = pl.program_id(1)
    q = q_ref[...]
    lane = lax.broadcasted_iota(jnp.int32, (1, LANES), 1)
    col = lax.broadcasted_iota(jnp.int32, (ATT_TQ, N_KBLOCKS), 1)
    contract_last = (((1,), (1,)), ((), ()))
    own = pl.multiple_of(j * MB_BLOCK, MB_BLOCK)
    prev = pl.multiple_of(jnp.maximum(j - 1, 0) * MB_BLOCK, MB_BLOCK)

    outs = []
    for hh in range(2):
        in_head = (lane < MB_HEAD_DIM) if hh == 0 else (lane >= MB_HEAD_DIM)
        qh = jnp.where(in_head, q, jnp.zeros((), BF16))

        gate = lax.dot_general(qh.astype(F32), kmean_ref[...], contract_last,
                               preferred_element_type=F32)
        g = jnp.where(col < j, gate, NEG_INF)
        sel = jnp.zeros((ATT_TQ, N_KBLOCKS), F32)
        for r in range(MB_TOPK):
            m = jnp.max(g, axis=1, keepdims=True)
            first = jnp.min(jnp.where(g == m, col, N_KBLOCKS), axis=1, keepdims=True)
            hit = col == first
            sel = jnp.where(jnp.logical_and(hit, r < j), 1.0, sel)
            g = jnp.where(hit, BELOW_NEG_INF, g)

        def block_probs(s, m_old):
            m_new = jnp.maximum(m_old, jnp.max(s, axis=1, keepdims=True))
            return m_new, jnp.exp(s - m_new)

        s = lax.dot_general(qh, k_ref[pl.ds(own, MB_BLOCK), :], contract_last,
                            preferred_element_type=F32) + bown_ref[hh]
        m_run = jnp.max(s, axis=1, keepdims=True)
        p = jnp.exp(s - m_run)
        l_run = jnp.sum(p, axis=1, keepdims=True)
        acc = jnp.dot(p.astype(BF16), v_ref[pl.ds(own, MB_BLOCK), :], preferred_element_type=F32)

        def past_block(start, bias, n, carry):
            m_old, l_old, acc_old = carry
            picked = jnp.sum(jnp.where(col == n, sel, 0.0), axis=1, keepdims=True) > 0.0
            s = lax.dot_general(qh, k_ref[pl.ds(start, MB_BLOCK), :], contract_last,
                                preferred_element_type=F32)
            s = jnp.where(picked, s + bias, NEG_INF)
            m_new, p = block_probs(s, m_old)
            alpha = jnp.exp(m_old - m_new)
            l_new = alpha * l_old + jnp.sum(p, axis=1, keepdims=True)
            acc_new = alpha * acc_old + jnp.dot(p.astype(BF16), v_ref[pl.ds(start, MB_BLOCK), :],
                                                preferred_element_type=F32)
            return m_new, l_new, acc_new

        far_bias = c31_ref[2 * pair + hh]

        def far_body(n, carry):
            return past_block(pl.multiple_of(n * MB_BLOCK, MB_BLOCK), far_bias, n, carry)

        carry = lax.fori_loop(0, jnp.maximum(j - 1, 0), far_body, (m_run, l_run, acc))
        m_run, l_run, acc = past_block(prev, bprev_ref[hh], j - 1, carry)
        outs.append(acc / l_run)

    o_ref[...] = jnp.where(lane < MB_HEAD_DIM, outs[0], outs[1]).astype(BF16)


def _attention(c31, q, k, v, kmean, bias_own, bias_prev):
    tile = pl.BlockSpec((ATT_TQ, LANES), lambda p, i, c: (i, p))
    whole = pl.BlockSpec((SEQ, LANES), lambda p, i, c: (0, p))
    bias = pl.BlockSpec((2, MB_BLOCK, MB_BLOCK), lambda p, i, c: (p, 0, 0))
    return pl.pallas_call(
        _attn_kernel,
        grid_spec=pltpu.PrefetchScalarGridSpec(
            num_scalar_prefetch=1,
            grid=(MB_HEADS // 2, SEQ // ATT_TQ),
            in_specs=[tile, whole, whole,
                      pl.BlockSpec((N_KBLOCKS, LANES), lambda p, i, c: (0, p)),
                      bias, bias],
            out_specs=tile,
        ),
        out_shape=jax.ShapeDtypeStruct((SEQ, D_MODEL), BF16),
        compiler_params=_params("parallel", "parallel"),
        name="moba_attention",
    )(c31, q, k, v, kmean, bias_own, bias_prev)


def _proj_kernel(x_ref, a_ref, vec_ref, w_ref, o_ref):
    y = jnp.dot(a_ref[...], w_ref[...], preferred_element_type=F32)
    o_ref[...] = x_ref[...] + vec_ref[3:4, :] * y


def _out_proj(x, a, vec, w_o):
    row = pl.BlockSpec((PROJ_TM, D_MODEL), lambda i: (i, 0))
    return pl.pallas_call(
        _proj_kernel,
        grid=(SEQ // PROJ_TM,),
        in_specs=[row, row, _resident((8, D_MODEL)), _resident((D_MODEL, D_MODEL))],
        out_specs=row,
        out_shape=jax.ShapeDtypeStruct((SEQ, D_MODEL), F32),
        compiler_params=_params("parallel"),
        name="moba_out_proj",
    )(x, a, vec, w_o)


def _t5_bucket_np(rel):
    n = np.maximum(rel, 0)
    nf = np.maximum(n, T5_MAX_EXACT).astype(np.float32)
    large = T5_MAX_EXACT + (np.log(nf / np.float32(T5_MAX_EXACT))
                            / np.float32(math.log(T5_MAX_DISTANCE / T5_MAX_EXACT))
                            * np.float32(T5_NUM_BUCKETS - T5_MAX_EXACT)).astype(np.int32)
    large = np.minimum(large, T5_NUM_BUCKETS - 1)
    return np.where(n < T5_MAX_EXACT, n, large)


def _moba_bias_tables(rel_bias):
    a = np.arange(MB_BLOCK)
    rel_own = a[:, None] - a[None, :]
    rel_prev = rel_own + MB_BLOCK
    table = rel_bias.astype(F32).T
    b_own = jnp.where(jnp.asarray(rel_own >= 0)[None], table[:, _t5_bucket_np(rel_own)], NEG_INF)
    b_prev = table[:, _t5_bucket_np(rel_prev)]
    return b_own, b_prev, table[:, T5_NUM_BUCKETS - 1]


def _vec(norm_g, mod3):
    return jnp.concatenate([norm_g[None, :], mod3, jnp.zeros((4, D_MODEL), F32)], axis=0)


def kernel(x, c, rel_bias, mod_w, mod_b, norm_g, ffn_w_in, ffn_w_out, gmlp_w_in, gmlp_v_norm,
           gmlp_w_s, gmlp_b_s, gmlp_w_out, moba_w_qkv, moba_w_o, final_norm):
    assert x.shape == (1, SEQ, D_MODEL)
    xs = x.reshape(SEQ, D_MODEL)
    mod = _modulation(c.reshape(D_MODEL, 1), mod_w, mod_b).reshape(DEPTH, N_SUBLAYERS, 3, D_MODEL)
    fin = final_norm.reshape(1, D_MODEL)

    for i in range(DEPTH):
        li = i // 2
        xs = _ffn(xs, _vec(norm_g[i, 0], mod[i, 0]), ffn_w_in[i, 0].astype(BF16),
                  ffn_w_out[i, 0].astype(BF16), fin, False)
        vec = _vec(norm_g[i, 1], mod[i, 1])
        if i % 2 == 0:
            bias = jnp.repeat(gmlp_b_s[li].T, GM_GROUP_DIM, axis=1)
            xs = _gmlp(xs, vec, gmlp_w_in[li].astype(BF16), gmlp_v_norm[li].reshape(1, GM_D),
                       gmlp_w_s[li].astype(BF16), bias, gmlp_w_out[li].astype(BF16))
        else:
            q, k, v, kmean = _qkv(xs, vec, moba_w_qkv[li].astype(BF16))
            b_own, b_prev, c31 = _moba_bias_tables(rel_bias)
            a = _attention(c31, q, k, v, kmean.reshape(N_KBLOCKS, D_MODEL), b_own, b_prev)
            xs = _out_proj(xs, a, vec, moba_w_o[li].astype(BF16))
        xs = _ffn(xs, _vec(norm_g[i, 2], mod[i, 2]), ffn_w_in[i, 1].astype(BF16),
                  ffn_w_out[i, 1].astype(BF16), fin, i == DEPTH - 1)
    return xs.reshape(1, SEQ, D_MODEL)
```

```python
import functools
import math

import numpy as np
import jax
import jax.numpy as jnp
from jax import lax
from jax.experimental import pallas as pl
from jax.experimental.pallas import tpu as pltpu

F32 = jnp.float32
BF16 = jnp.bfloat16

D_MODEL = 1024
SEQ = 16384
DEPTH = 2
N_SUBLAYERS = 3
RMS_EPS = 1e-6
D_FF = 2816

GM_D = 3 * D_MODEL
GM_GROUPS = 16
GM_GROUP_DIM = GM_D // GM_GROUPS
GM_CHUNK = 128
GM_PAIR = 2 * GM_GROUP_DIM

MB_HEADS = 16
MB_HEAD_DIM = 64
MB_BLOCK = 256
MB_TOPK = 3
MB_SCALE = MB_HEAD_DIM ** -0.5
N_KBLOCKS = SEQ // MB_BLOCK

T5_NUM_BUCKETS = 32
T5_MAX_EXACT = 16
T5_MAX_DISTANCE = 128
NEG_INF = -1e30
BELOW_NEG_INF = -3e38

LANES = 128
VMEM_LIMIT = 56 * 1024 * 1024

MOD_TN = 1536
FFN_TM = 512
FFN_CHUNK = 1408
GM_TM = 256
GM_VCHUNK = 768
QKV_TM = 512
PROJ_TM = 512

ROUTE_T = 512
ROUTE_ALIGN = 8
STEP_ROWS = 1024
NSTEPS = SEQ // STEP_ROWS
BUF_DATA_ROWS = MB_TOPK * SEQ + N_KBLOCKS * ROUTE_ALIGN
NULL_ROW = BUF_DATA_ROWS + MB_BLOCK
DUMP_ROW = NULL_ROW + ROUTE_ALIGN
BUF_ROWS = NULL_ROW + MB_BLOCK
ROUTE_NONE = 1e9


def _resident(shape):
    nd = len(shape)
    return pl.BlockSpec(shape, lambda *_: (0,) * nd, pipeline_mode=pl.Buffered(1))


def _params(*sem):
    return pltpu.CompilerParams(dimension_semantics=sem, vmem_limit_bytes=VMEM_LIMIT)


def _rms(x):
    return x * lax.rsqrt(jnp.mean(x * x, axis=-1, keepdims=True) + RMS_EPS)


def _adaln(x, vec_ref):
    y = _rms(x) * vec_ref[0:1, :]
    return y * (1.0 + vec_ref[2:3, :]) + vec_ref[1:2, :]


def _gelu(x):
    return 0.5 * x * (1.0 + lax.erf(x * np.float32(math.sqrt(0.5))))


def _mod_kernel(c_ref, w_ref, b_ref, o_ref):
    c = c_ref[...]
    c_act = c * jax.nn.sigmoid(c)
    o_ref[...] = jnp.sum(c_act * w_ref[...], axis=0, keepdims=True) + b_ref[...]


def _modulation(c_col, mod_w, mod_b):
    n = mod_w.shape[-1]
    return pl.pallas_call(
        _mod_kernel,
        grid=(DEPTH, n // MOD_TN),
        in_specs=[
            pl.BlockSpec((D_MODEL, 1), lambda i, j: (0, 0)),
            pl.BlockSpec((None, D_MODEL, MOD_TN), lambda i, j: (i, 0, j)),
            pl.BlockSpec((None, 1, MOD_TN), lambda i, j: (i, 0, j)),
        ],
        out_specs=pl.BlockSpec((None, 1, MOD_TN), lambda i, j: (i, 0, j)),
        out_shape=jax.ShapeDtypeStruct((DEPTH, 1, n), F32),
        compiler_params=_params("arbitrary", "arbitrary"),
        name="modulation",
    )(c_col, mod_w, mod_b.reshape(DEPTH, 1, n))


def _ffn_kernel(x_ref, vec_ref, win_ref, wout_ref, fin_ref, o_ref, act_ref, *, final):
    x = x_ref[...]
    h = _adaln(x, vec_ref).astype(BF16)
    for c in range(D_FF // FFN_CHUNK):
        lo = c * FFN_CHUNK
        g = jnp.dot(h, win_ref[:, lo:lo + FFN_CHUNK], preferred_element_type=F32)
        u = jnp.dot(h, win_ref[:, D_FF + lo:D_FF + lo + FFN_CHUNK], preferred_element_type=F32)
        act_ref[:, lo:lo + FFN_CHUNK] = (g * jax.nn.sigmoid(g) * u).astype(BF16)
    y = jnp.dot(act_ref[...], wout_ref[...], preferred_element_type=F32)
    out = x + (0.5 * vec_ref[3:4, :]) * y
    if final:
        out = _rms(out) * fin_ref[...]
    o_ref[...] = out


def _ffn(x, vec, w_in, w_out, fin, final):
    return pl.pallas_call(
        functools.partial(_ffn_kernel, final=final),
        grid=(SEQ // FFN_TM,),
        in_specs=[
            pl.BlockSpec((FFN_TM, D_MODEL), lambda i: (i, 0)),
            _resident((8, D_MODEL)),
            _resident((D_MODEL, 2 * D_FF)),
            _resident((D_FF, D_MODEL)),
            _resident((1, D_MODEL)),
        ],
        out_specs=pl.BlockSpec((FFN_TM, D_MODEL), lambda i: (i, 0)),
        out_shape=jax.ShapeDtypeStruct((SEQ, D_MODEL), F32),
        scratch_shapes=[pltpu.VMEM((FFN_TM, D_FF), BF16)],
        compiler_params=_params("parallel"),
        name="ffn_final" if final else "ffn",
    )(x, vec, w_in, w_out, fin)


def _gmlp_kernel(x_ref, vec_ref, win_ref, vnorm_ref, ws_ref, bias_ref, wout_ref,
                 o_ref, v_ref, g_ref):
    x = x_ref[...]
    h = _adaln(x, vec_ref).astype(BF16)

    ssq = jnp.zeros((GM_TM, 1), F32)
    for c in range(GM_D // GM_VCHUNK):
        lo = c * GM_VCHUNK
        v = _gelu(jnp.dot(h, win_ref[:, GM_D + lo:GM_D + lo + GM_VCHUNK],
                          preferred_element_type=F32))
        v_ref[:, lo:lo + GM_VCHUNK] = v
        ssq = ssq + jnp.sum(v * v, axis=-1, keepdims=True)
    r = lax.rsqrt(ssq * (1.0 / GM_D) + RMS_EPS)

    row = lax.broadcasted_iota(jnp.int32, (GM_CHUNK, GM_CHUNK), 0)
    col = lax.broadcasted_iota(jnp.int32, (GM_CHUNK, GM_CHUNK), 1)
    causal = row >= col
    low_half = lax.broadcasted_iota(jnp.int32, (GM_CHUNK, LANES), 1) < (GM_GROUP_DIM - LANES)

    for p in range(GM_GROUPS // 2):
        lo = p * GM_PAIR
        vn = (v_ref[:, lo:lo + GM_PAIR] * r * vnorm_ref[:, lo:lo + GM_PAIR]).astype(BF16)
        u = _gelu(jnp.dot(h, win_ref[:, lo:lo + GM_PAIR], preferred_element_type=F32))
        w0 = jnp.where(causal, ws_ref[2 * p], jnp.zeros((), BF16))
        w1 = jnp.where(causal, ws_ref[2 * p + 1], jnp.zeros((), BF16))
        for c in range(GM_TM // GM_CHUNK):
            rows = slice(c * GM_CHUNK, (c + 1) * GM_CHUNK)
            vc = vn[rows]
            a = jnp.dot(w0, vc[:, :2 * LANES], preferred_element_type=F32)
            b = jnp.dot(w1, vc[:, LANES:], preferred_element_type=F32)
            mid = jnp.where(low_half, a[:, LANES:], b[:, :LANES])
            sv = jnp.concatenate([a[:, :LANES], mid, b[:, LANES:]], axis=1)
            sv = sv + bias_ref[:, lo:lo + GM_PAIR]
            g_ref[rows, lo:lo + GM_PAIR] = (u[rows] * sv).astype(BF16)

    y = jnp.dot(g_ref[...], wout_ref[...], preferred_element_type=F32)
    o_ref[...] = x + vec_ref[3:4, :] * y


def _gmlp(x, vec, w_in, v_norm, w_s, bias, w_out):
    return pl.pallas_call(
        _gmlp_kernel,
        grid=(SEQ // GM_TM,),
        in_specs=[
            pl.BlockSpec((GM_TM, D_MODEL), lambda i: (i, 0)),
            _resident((8, D_MODEL)),
            _resident((D_MODEL, 2 * GM_D)),
            _resident((1, GM_D)),
            _resident((GM_GROUPS, GM_CHUNK, GM_CHUNK)),
            _resident((GM_CHUNK, GM_D)),
            _resident((GM_D, D_MODEL)),
        ],
        out_specs=pl.BlockSpec((GM_TM, D_MODEL), lambda i: (i, 0)),
        out_shape=jax.ShapeDtypeStruct((SEQ, D_MODEL), F32),
        scratch_shapes=[pltpu.VMEM((GM_TM, GM_D), F32), pltpu.VMEM((GM_TM, GM_D), BF16)],
        compiler_params=_params("parallel"),
        name="gmlp",
    )(x, vec, w_in, v_norm, w_s, bias, w_out)


def _qkv_kernel(x_ref, vec_ref, w_ref, q_ref, k_ref, v_ref, selt_ref, prevt_ref, cnt_ref,
                kmean_ref):
    i = pl.program_id(0)

    @pl.when(i == 0)
    def _init():
        kmean_ref[...] = jnp.zeros_like(kmean_ref)
        cnt_ref[...] = jnp.zeros_like(cnt_ref)

    h = _adaln(x_ref[...], vec_ref).astype(BF16)
    q = jnp.dot(h, w_ref[:, :D_MODEL], preferred_element_type=F32) * MB_SCALE
    q_ref[...] = q
    k = jnp.dot(h, w_ref[:, D_MODEL:2 * D_MODEL], preferred_element_type=F32)
    k_ref[...] = k.astype(BF16)
    nb = QKV_TM // MB_BLOCK
    for b in range(nb):
        kmean_ref[pl.ds(nb * i + b, 1), :] = jnp.mean(
            k[b * MB_BLOCK:(b + 1) * MB_BLOCK], axis=0, keepdims=True)
    v = jnp.dot(h, w_ref[:, 2 * D_MODEL:], preferred_element_type=F32)
    v_ref[...] = v.astype(BF16)

    lane = lax.broadcasted_iota(jnp.int32, (1, LANES), 1)
    blk = lax.broadcasted_iota(jnp.int32, (N_KBLOCKS, MB_BLOCK), 0)
    contract_last = (((1,), (1,)), ((), ()))
    for b in range(nb):
        j = nb * i + b
        cols = slice(b * MB_BLOCK, (b + 1) * MB_BLOCK)
        for pair in range(MB_HEADS // 2):
            km = kmean_ref[:, pair * LANES:(pair + 1) * LANES]
            qp = q[cols, pair * LANES:(pair + 1) * LANES]
            for hh in range(2):
                head = 2 * pair + hh
                km_h = jnp.where((lane // MB_HEAD_DIM) == hh, km, 0.0)
                g = lax.dot_general(km_h, qp, contract_last, preferred_element_type=F32)
                g = jnp.where(blk < j, g, NEG_INF)
                sel = jnp.zeros((N_KBLOCKS, MB_BLOCK), F32)
                for r in range(MB_TOPK):
                    m = jnp.max(g, axis=0, keepdims=True)
                    first = jnp.min(jnp.where(g == m, blk, N_KBLOCKS), axis=0, keepdims=True)
                    hit = blk == first
                    sel = jnp.where(jnp.logical_and(hit, r < j), 1.0, sel)
                    g = jnp.where(hit, BELOW_NEG_INF, g)
                prevt_ref[head:head + 1, cols] = jnp.max(
                    jnp.where(blk == j - 1, sel, 0.0), axis=0, keepdims=True)
                routed = jnp.where(blk < j - 1, sel, 0.0)
                selt_ref[head * N_KBLOCKS:(head + 1) * N_KBLOCKS, cols] = routed.astype(BF16)
                rows = slice(head * N_KBLOCKS, (head + 1) * N_KBLOCKS)
                cnt_ref[rows, :] = cnt_ref[rows, :] + jnp.sum(routed, axis=1, keepdims=True)


def _qkv(x, vec, w_qkv):
    row = pl.BlockSpec((QKV_TM, D_MODEL), lambda i: (i, 0))
    n_items = MB_HEADS * N_KBLOCKS
    return pl.pallas_call(
        _qkv_kernel,
        grid=(SEQ // QKV_TM,),
        in_specs=[row, _resident((8, D_MODEL)), _resident((D_MODEL, 3 * D_MODEL))],
        out_specs=[row, row, row,
                   pl.BlockSpec((n_items, QKV_TM), lambda i: (0, i)),
                   pl.BlockSpec((MB_HEADS, QKV_TM), lambda i: (0, i)),
                   pl.BlockSpec((n_items, LANES), lambda i: (0, 0))],
        out_shape=[jax.ShapeDtypeStruct((SEQ, D_MODEL), F32),
                   jax.ShapeDtypeStruct((SEQ, D_MODEL), BF16),
                   jax.ShapeDtypeStruct((SEQ, D_MODEL), BF16),
                   jax.ShapeDtypeStruct((n_items, SEQ), BF16),
                   jax.ShapeDtypeStruct((MB_HEADS, SEQ), F32),
                   jax.ShapeDtypeStruct((n_items, LANES), F32)],
        scratch_shapes=[pltpu.VMEM((N_KBLOCKS, D_MODEL), F32)],
        compiler_params=_params("arbitrary"),
        name="moba_qkv_gate",
    )(x, vec, w_qkv)


def _route_kernel(selt_ref, base_ref, pos_ref, carry_ref):
    i = pl.program_id(0)

    @pl.when(i == 0)
    def _init():
        carry_ref[...] = jnp.zeros_like(carry_ref)

    sel = selt_ref[...]
    before = (lax.broadcasted_iota(jnp.int32, (ROUTE_T, ROUTE_T), 0)
              < lax.broadcasted_iota(jnp.int32, (ROUTE_T, ROUTE_T), 1))
    rank = jnp.dot(sel, jnp.where(before, 1.0, 0.0).astype(BF16), preferred_element_type=F32)
    offset = base_ref[...] + carry_ref[...]
    rank = rank + jnp.concatenate([offset] * (ROUTE_T // LANES), axis=1)
    carry_ref[...] = carry_ref[...] + jnp.sum(sel.astype(F32), axis=1, keepdims=True)
    val = jnp.where(sel > 0, rank, ROUTE_NONE)
    for head in range(MB_HEADS):
        vh = val[head * N_KBLOCKS:(head + 1) * N_KBLOCKS]
        for r in range(MB_TOPK):
            m = jnp.min(vh, axis=0, keepdims=True)
            slot = head * MB_TOPK + r
            pos_ref[slot:slot + 1, :] = jnp.where(m < ROUTE_NONE, m, float(NULL_ROW)).astype(jnp.int32)
            vh = jnp.where(vh == m, ROUTE_NONE, vh)


def _route(selt, base):
    n_items = MB_HEADS * N_KBLOCKS
    return pl.pallas_call(
        _route_kernel,
        grid=(SEQ // ROUTE_T,),
        in_specs=[pl.BlockSpec((n_items, ROUTE_T), lambda i: (0, i)),
                  pl.BlockSpec((n_items, LANES), lambda i: (0, 0))],
        out_specs=pl.BlockSpec((MB_HEADS * MB_TOPK, ROUTE_T), lambda i: (0, i)),
        out_shape=jax.ShapeDtypeStruct((MB_HEADS * MB_TOPK, SEQ), jnp.int32),
        scratch_shapes=[pltpu.VMEM((n_items, LANES), F32)],
        compiler_params=_params("arbitrary"),
        name="moba_route",
    )(selt, base)


def _moba_kernel(gstart_ref, gcnt_ref, c31_ref,
                 qf_ref, pos_ref, k_ref, v_ref, kb_ref, vb_ref, prev_ref, bown_ref, bprev_ref,
                 o_ref, buf_ref, stage_ref, hold_ref):
    head = pl.program_id(0)
    phase = pl.program_id(1)
    step = pl.program_id(2)
    hh = head % 2
    lane = lax.broadcasted_iota(jnp.int32, (1, LANES), 1)
    in_head = (lane // MB_HEAD_DIM) == hh
    stat0 = (1 - hh) * MB_HEAD_DIM
    is_m_lane = lane == stat0
    is_l_lane = lane == stat0 + 1
    contract_last = (((1,), (1,)), ((), ()))
    ones_bf = jnp.ones((), BF16)

    def attend(qh, keys, vals, bias, m_old=None, acc_old=None, row_mask=None):
        s = lax.dot_general(qh, keys, contract_last, preferred_element_type=F32)
        if bias is not None:
            s = s + bias
        if row_mask is not None:
            s = jnp.where(row_mask, s, NEG_INF)
        m = jnp.max(s, axis=1, keepdims=True)
        if m_old is not None:
            m = jnp.maximum(m, m_old)
        p = jnp.exp(s - m).astype(BF16)
        acc = jnp.dot(p, jnp.where(in_head, vals, ones_bf), preferred_element_type=F32)
        if acc_old is not None:
            acc = acc + jnp.exp(m_old - m) * acc_old
        return m, acc

    @pl.when(phase == 0)
    def _dispatch():
        @pl.when(step == 0)
        def _init():
            def zero(c, _):
                buf_ref[pl.ds(pl.multiple_of(c * MB_BLOCK, MB_BLOCK), MB_BLOCK), :] = (
                    jnp.zeros((MB_BLOCK, LANES), F32))
                return 0
            lax.fori_loop(0, BUF_ROWS // MB_BLOCK, zero, 0)
            buf_ref[NULL_ROW:NULL_ROW + 8, :] = jnp.where(
                is_m_lane, NEG_INF, jnp.zeros((8, LANES), F32))

        def body(a, _):
            row = qf_ref[pl.ds(a, 1), :]
            for r in range(MB_TOPK):
                p = pos_ref[r, a]
                p = jnp.where(p == NULL_ROW, DUMP_ROW, p)
                buf_ref[pl.ds(p, 1), :] = row
            return 0
        lax.fori_loop(0, STEP_ROWS, body, 0, unroll=8)

    @pl.when(phase == 1)
    def _routed():
        far_bias = c31_ref[head]
        row_id = lax.broadcasted_iota(jnp.int32, (MB_BLOCK, 1), 0)
        for g in range(STEP_ROWS // MB_BLOCK):
            grp = head * N_KBLOCKS + step * (STEP_ROWS // MB_BLOCK) + g
            keys = k_ref[g * MB_BLOCK:(g + 1) * MB_BLOCK, :]
            vals = v_ref[g * MB_BLOCK:(g + 1) * MB_BLOCK, :]
            start = gstart_ref[grp]
            cnt = gcnt_ref[grp]

            def chunk(c, _):
                off = pl.multiple_of(start + c * MB_BLOCK, ROUTE_ALIGN)
                rows = buf_ref[pl.ds(off, MB_BLOCK), :]
                qh = jnp.where(in_head, rows, 0.0).astype(BF16)
                m, acc = attend(qh, keys, vals, None)
                new = jnp.where(is_m_lane, m + far_bias, acc)
                buf_ref[pl.ds(off, MB_BLOCK), :] = jnp.where(row_id < cnt - c * MB_BLOCK, new, rows)
                return 0
            lax.fori_loop(0, (cnt + MB_BLOCK - 1) // MB_BLOCK, chunk, 0)

    @pl.when(phase == 2)
    def _combine():
        lane16 = lax.broadcasted_iota(jnp.int32, (1, MB_HEADS), 1)
        for jj in range(STEP_ROWS // MB_BLOCK):
            j = step * (STEP_ROWS // MB_BLOCK) + jj
            rs = slice(jj * MB_BLOCK, (jj + 1) * MB_BLOCK)
            qh = jnp.where(in_head, qf_ref[rs, :], 0.0).astype(BF16)
            m_d, acc = attend(qh, k_ref[rs, :], v_ref[rs, :], bown_ref[...])
            if jj == 0:
                k_prev, v_prev = kb_ref[...], vb_ref[...]
            else:
                ps = slice((jj - 1) * MB_BLOCK, jj * MB_BLOCK)
                k_prev, v_prev = k_ref[ps, :], v_ref[ps, :]
            picked = jnp.sum(jnp.where(lane16 == head, prev_ref[rs, :], 0.0),
                             axis=1, keepdims=True) > 0.0
            m_d, acc = attend(qh, k_prev, v_prev, bprev_ref[...], m_d, acc, picked)

            def gather(a, _):
                for r in range(MB_TOPK):
                    stage_ref[r, pl.ds(a, 1), :] = buf_ref[pl.ds(pos_ref[r, jj * MB_BLOCK + a], 1), :]
                return 0
            lax.fori_loop(0, MB_BLOCK, gather, 0, unroll=8)

            parts = [stage_ref[r] for r in range(MB_TOPK)]
            ms = [jnp.sum(jnp.where(is_m_lane, part, 0.0), axis=1, keepdims=True) for part in parts]
            m_all = m_d
            for m_r in ms:
                m_all = jnp.maximum(m_all, m_r)
            total = jnp.exp(m_d - m_all) * acc
            for m_r, part in zip(ms, parts):
                total = total + jnp.exp(m_r - m_all) * part
            den = jnp.sum(jnp.where(is_l_lane, total, 0.0), axis=1, keepdims=True)
            out = (total / den).astype(BF16)
            hold_rows = pl.ds(pl.multiple_of(j * MB_BLOCK, MB_BLOCK), MB_BLOCK)

            @pl.when(hh == 0)
            def _keep():
                hold_ref[hold_rows, :] = out

            @pl.when(hh == 1)
            def _emit():
                o_ref[rs, :] = jnp.where(lane < MB_HEAD_DIM, hold_ref[hold_rows, :], out)


def _moba_attention(gstart, gcnt, c31, qf, pos, k, v, prevc, bias_own, bias_prev):
    last = NSTEPS - 1

    def stay(ph, st):
        return jnp.where(ph == 1, last, st)

    kv_big = pl.BlockSpec((STEP_ROWS, LANES), lambda h, ph, st, *_: (jnp.where(ph == 0, 0, st), h // 2))
    kv_prev = pl.BlockSpec(
        (MB_BLOCK, LANES),
        lambda h, ph, st, *_: (jnp.where(ph == 2, jnp.maximum(st * (STEP_ROWS // MB_BLOCK) - 1, 0), 0), h // 2))
    bias = pl.BlockSpec((None, MB_BLOCK, MB_BLOCK), lambda h, ph, st, *_: (h, 0, 0))
    return pl.pallas_call(
        _moba_kernel,
        grid_spec=pltpu.PrefetchScalarGridSpec(
            num_scalar_prefetch=3,
            grid=(MB_HEADS, 3, NSTEPS),
            in_specs=[
                pl.BlockSpec((STEP_ROWS, LANES), lambda h, ph, st, *_: (stay(ph, st), h // 2)),
                pl.BlockSpec((None, MB_TOPK, STEP_ROWS), lambda h, ph, st, *_: (h, 0, stay(ph, st)),
                             memory_space=pltpu.SMEM),
                kv_big, kv_big, kv_prev, kv_prev,
                pl.BlockSpec((STEP_ROWS, MB_HEADS), lambda h, ph, st, *_: (jnp.where(ph == 2, st, 0), 0)),
                bias, bias,
            ],
            out_specs=pl.BlockSpec(
                (STEP_ROWS, LANES),
                lambda h, ph, st, *_: (jnp.where(jnp.logical_and(ph == 2, h % 2 == 1), st, 0), h // 2)),
            scratch_shapes=[pltpu.VMEM((BUF_ROWS, LANES), F32),
                            pltpu.VMEM((MB_TOPK, MB_BLOCK, LANES), F32),
                            pltpu.VMEM((SEQ, LANES), BF16)],
        ),
        out_shape=jax.ShapeDtypeStruct((SEQ, D_MODEL), BF16),
        compiler_params=_params("arbitrary", "arbitrary", "arbitrary"),
        name="moba_attention",
    )(gstart, gcnt, c31, qf, pos, k, v, k, v, prevc, bias_own, bias_prev)


def _proj_kernel(x_ref, a_ref, vec_ref, w_ref, o_ref):
    y = jnp.dot(a_ref[...], w_ref[...], preferred_element_type=F32)
    o_ref[...] = x_ref[...] + vec_ref[3:4, :] * y


def _out_proj(x, a, vec, w_o):
    row = pl.BlockSpec((PROJ_TM, D_MODEL), lambda i: (i, 0))
    return pl.pallas_call(
        _proj_kernel,
        grid=(SEQ // PROJ_TM,),
        in_specs=[row, row, _resident((8, D_MODEL)), _resident((D_MODEL, D_MODEL))],
        out_specs=row,
        out_shape=jax.ShapeDtypeStruct((SEQ, D_MODEL), F32),
        compiler_params=_params("parallel"),
        name="moba_out_proj",
    )(x, a, vec, w_o)


def _t5_bucket_np(rel):
    n = np.maximum(rel, 0)
    nf = np.maximum(n, T5_MAX_EXACT).astype(np.float32)
    large = T5_MAX_EXACT + (np.log(nf / np.float32(T5_MAX_EXACT))
                            / np.float32(math.log(T5_MAX_DISTANCE / T5_MAX_EXACT))
                            * np.float32(T5_NUM_BUCKETS - T5_MAX_EXACT)).astype(np.int32)
    large = np.minimum(large, T5_NUM_BUCKETS - 1)
    return np.where(n < T5_MAX_EXACT, n, large)


def _moba_bias_tables(rel_bias):
    a = np.arange(MB_BLOCK)
    rel_own = a[:, None] - a[None, :]
    table = rel_bias.astype(F32).T

    def expand(rel):
        bucket = jnp.asarray(_t5_bucket_np(rel).reshape(-1).astype(np.int32))
        onehot = (bucket[None, :] == jnp.arange(T5_NUM_BUCKETS)[:, None]).astype(F32)
        return jnp.dot(table, onehot, precision=lax.Precision.HIGHEST).reshape(
            MB_HEADS, MB_BLOCK, MB_BLOCK)

    b_own = jnp.where(jnp.asarray(rel_own >= 0)[None], expand(rel_own), NEG_INF)
    return b_own, expand(rel_own + MB_BLOCK), table[:, T5_NUM_BUCKETS - 1]


def _moba(xs, vec, w_qkv, w_o, rel_bias):
    qf, k, v, selt, prevt, cnt = _qkv(xs, vec, w_qkv)
    cnt = cnt[:, 0].astype(jnp.int32).reshape(MB_HEADS, N_KBLOCKS)
    padded = (cnt + ROUTE_ALIGN - 1) // ROUTE_ALIGN * ROUTE_ALIGN
    start = jnp.cumsum(padded, axis=1) - padded
    base = jnp.broadcast_to(start.reshape(-1, 1).astype(F32), (MB_HEADS * N_KBLOCKS, LANES))
    pos = _route(selt, base).reshape(MB_HEADS, MB_TOPK, SEQ)
    b_own, b_prev, c31 = _moba_bias_tables(rel_bias)
    a = _moba_attention(start.reshape(-1), cnt.reshape(-1), c31, qf, pos, k, v, prevt.T, b_own, b_prev)
    return _out_proj(xs, a, vec, w_o)


def _vec(norm_g, mod3):
    return jnp.concatenate([norm_g[None, :], mod3, jnp.zeros((4, D_MODEL), F32)], axis=0)


def kernel(x, c, rel_bias, mod_w, mod_b, norm_g, ffn_w_in, ffn_w_out, gmlp_w_in, gmlp_v_norm,
           gmlp_w_s, gmlp_b_s, gmlp_w_out, moba_w_qkv, moba_w_o, final_norm):
    assert x.shape == (1, SEQ, D_MODEL)
    xs = x.reshape(SEQ, D_MODEL)
    mod = _modulation(c.reshape(D_MODEL, 1), mod_w, mod_b).reshape(DEPTH, N_SUBLAYERS, 3, D_MODEL)
    fin = final_norm.reshape(1, D_MODEL)

    for i in range(DEPTH):
        li = i // 2
        xs = _ffn(xs, _vec(norm_g[i, 0], mod[i, 0]), ffn_w_in[i, 0].astype(BF16),
                  ffn_w_out[i, 0].astype(BF16), fin, False)
        vec = _vec(norm_g[i, 1], mod[i, 1])
        if i % 2 == 0:
            bias = jnp.repeat(gmlp_b_s[li].T, GM_GROUP_DIM, axis=1)
            xs = _gmlp(xs, vec, gmlp_w_in[li].astype(BF16), gmlp_v_norm[li].reshape(1, GM_D),
                       gmlp_w_s[li].astype(BF16), bias, gmlp_w_out[li].astype(BF16))
        else:
            xs = _moba(xs, vec, moba_w_qkv[li].astype(BF16), moba_w_o[li].astype(BF16), rel_bias)
        xs = _ffn(xs, _vec(norm_g[i, 2], mod[i, 2]), ffn_w_in[i, 1].astype(BF16),
                  ffn_w_out[i, 1].astype(BF16), fin, i == DEPTH - 1)
    return xs.reshape(1, SEQ, D_MODEL)
```

```python
import functools
import math

import numpy as np
import jax
import jax.numpy as jnp
from jax import lax
from jax.experimental import pallas as pl
from jax.experimental.pallas import tpu as pltpu

F32 = jnp.float32
BF16 = jnp.bfloat16

D_MODEL = 1024
SEQ = 16384
DEPTH = 2
N_SUBLAYERS = 3
RMS_EPS = 1e-6
D_FF = 2816

GM_D = 3 * D_MODEL
GM_GROUPS = 16
GM_GROUP_DIM = GM_D // GM_GROUPS
GM_CHUNK = 128
GM_PAIR = 2 * GM_GROUP_DIM

MB_HEADS = 16
MB_HEAD_DIM = 64
MB_BLOCK = 256
MB_TOPK = 3
MB_SCALE = MB_HEAD_DIM ** -0.5
N_KBLOCKS = SEQ // MB_BLOCK

T5_NUM_BUCKETS = 32
T5_MAX_EXACT = 16
T5_MAX_DISTANCE = 128
NEG_INF = -1e30
BELOW_NEG_INF = -3e38

LANES = 128
VMEM_LIMIT = 56 * 1024 * 1024

MOD_TN = 1536
FFN_TM = 512
FFN_CHUNK = 1408
GM_TM = 256
GM_VCHUNK = 768
QKV_TM = 512
PROJ_TM = 512

ROUTE_T = 512
ROUTE_ALIGN = 8
STEP_ROWS = 1024
NSTEPS = SEQ // STEP_ROWS
BUF_DATA_ROWS = MB_TOPK * SEQ + N_KBLOCKS * ROUTE_ALIGN
BUF_ROWS = BUF_DATA_ROWS + 3 * MB_BLOCK
NULL_ROW = BUF_ROWS - ROUTE_ALIGN
ROUTE_NONE = 1e9


def _resident(shape):
    nd = len(shape)
    return pl.BlockSpec(shape, lambda *_: (0,) * nd, pipeline_mode=pl.Buffered(1))


def _params(*sem):
    return pltpu.CompilerParams(dimension_semantics=sem, vmem_limit_bytes=VMEM_LIMIT)


def _rms(x):
    return x * lax.rsqrt(jnp.mean(x * x, axis=-1, keepdims=True) + RMS_EPS)


def _adaln(x, vec_ref):
    y = _rms(x) * vec_ref[0:1, :]
    return y * (1.0 + vec_ref[2:3, :]) + vec_ref[1:2, :]


def _gelu(x):
    return 0.5 * x * (1.0 + lax.erf(x * np.float32(math.sqrt(0.5))))


def _mod_kernel(c_ref, w_ref, b_ref, o_ref):
    c = c_ref[...]
    c_act = c * jax.nn.sigmoid(c)
    o_ref[...] = jnp.sum(c_act * w_ref[...], axis=0, keepdims=True) + b_ref[...]


def _modulation(c_col, mod_w, mod_b):
    n = mod_w.shape[-1]
    return pl.pallas_call(
        _mod_kernel,
        grid=(DEPTH, n // MOD_TN),
        in_specs=[
            pl.BlockSpec((D_MODEL, 1), lambda i, j: (0, 0)),
            pl.BlockSpec((None, D_MODEL, MOD_TN), lambda i, j: (i, 0, j)),
            pl.BlockSpec((None, 1, MOD_TN), lambda i, j: (i, 0, j)),
        ],
        out_specs=pl.BlockSpec((None, 1, MOD_TN), lambda i, j: (i, 0, j)),
        out_shape=jax.ShapeDtypeStruct((DEPTH, 1, n), F32),
        compiler_params=_params("arbitrary", "arbitrary"),
        name="modulation",
    )(c_col, mod_w, mod_b.reshape(DEPTH, 1, n))


def _ffn_kernel(x_ref, vec_ref, win_ref, wout_ref, fin_ref, o_ref, act_ref, *, final):
    x = x_ref[...]
    h = _adaln(x, vec_ref).astype(BF16)
    for c in range(D_FF // FFN_CHUNK):
        lo = c * FFN_CHUNK
        g = jnp.dot(h, win_ref[:, lo:lo + FFN_CHUNK], preferred_element_type=F32)
        u = jnp.dot(h, win_ref[:, D_FF + lo:D_FF + lo + FFN_CHUNK], preferred_element_type=F32)
        act_ref[:, lo:lo + FFN_CHUNK] = (g * jax.nn.sigmoid(g) * u).astype(BF16)
    y = jnp.dot(act_ref[...], wout_ref[...], preferred_element_type=F32)
    out = x + (0.5 * vec_ref[3:4, :]) * y
    if final:
        out = _rms(out) * fin_ref[...]
    o_ref[...] = out


def _ffn(x, vec, w_in, w_out, fin, final):
    return pl.pallas_call(
        functools.partial(_ffn_kernel, final=final),
        grid=(SEQ // FFN_TM,),
        in_specs=[
            pl.BlockSpec((FFN_TM, D_MODEL), lambda i: (i, 0)),
            _resident((8, D_MODEL)),
            _resident((D_MODEL, 2 * D_FF)),
            _resident((D_FF, D_MODEL)),
            _resident((1, D_MODEL)),
        ],
        out_specs=pl.BlockSpec((FFN_TM, D_MODEL), lambda i: (i, 0)),
        out_shape=jax.ShapeDtypeStruct((SEQ, D_MODEL), F32),
        scratch_shapes=[pltpu.VMEM((FFN_TM, D_FF), BF16)],
        compiler_params=_params("parallel"),
        name="ffn_final" if final else "ffn",
    )(x, vec, w_in, w_out, fin)


def _gmlp_kernel(x_ref, vec_ref, win_ref, vnorm_ref, ws_ref, bias_ref, wout_ref,
                 o_ref, v_ref, g_ref):
    x = x_ref[...]
    h = _adaln(x, vec_ref).astype(BF16)

    ssq = jnp.zeros((GM_TM, 1), F32)
    for c in range(GM_D // GM_VCHUNK):
        lo = c * GM_VCHUNK
        v = _gelu(jnp.dot(h, win_ref[:, GM_D + lo:GM_D + lo + GM_VCHUNK],
                          preferred_element_type=F32))
        v_ref[:, lo:lo + GM_VCHUNK] = v
        ssq = ssq + jnp.sum(v * v, axis=-1, keepdims=True)
    r = lax.rsqrt(ssq * (1.0 / GM_D) + RMS_EPS)

    row = lax.broadcasted_iota(jnp.int32, (GM_CHUNK, GM_CHUNK), 0)
    col = lax.broadcasted_iota(jnp.int32, (GM_CHUNK, GM_CHUNK), 1)
    causal = row >= col
    low_half = lax.broadcasted_iota(jnp.int32, (GM_CHUNK, LANES), 1) < (GM_GROUP_DIM - LANES)

    for p in range(GM_GROUPS // 2):
        lo = p * GM_PAIR
        vn = (v_ref[:, lo:lo + GM_PAIR] * r * vnorm_ref[:, lo:lo + GM_PAIR]).astype(BF16)
        u = _gelu(jnp.dot(h, win_ref[:, lo:lo + GM_PAIR], preferred_element_type=F32))
        w0 = jnp.where(causal, ws_ref[2 * p], jnp.zeros((), BF16))
        w1 = jnp.where(causal, ws_ref[2 * p + 1], jnp.zeros((), BF16))
        for c in range(GM_TM // GM_CHUNK):
            rows = slice(c * GM_CHUNK, (c + 1) * GM_CHUNK)
            vc = vn[rows]
            a = jnp.dot(w0, vc[:, :2 * LANES], preferred_element_type=F32)
            b = jnp.dot(w1, vc[:, LANES:], preferred_element_type=F32)
            mid = jnp.where(low_half, a[:, LANES:], b[:, :LANES])
            sv = jnp.concatenate([a[:, :LANES], mid, b[:, LANES:]], axis=1)
            sv = sv + bias_ref[:, lo:lo + GM_PAIR]
            g_ref[rows, lo:lo + GM_PAIR] = (u[rows] * sv).astype(BF16)

    y = jnp.dot(g_ref[...], wout_ref[...], preferred_element_type=F32)
    o_ref[...] = x + vec_ref[3:4, :] * y


def _gmlp(x, vec, w_in, v_norm, w_s, bias, w_out):
    return pl.pallas_call(
        _gmlp_kernel,
        grid=(SEQ // GM_TM,),
        in_specs=[
            pl.BlockSpec((GM_TM, D_MODEL), lambda i: (i, 0)),
            _resident((8, D_MODEL)),
            _resident((D_MODEL, 2 * GM_D)),
            _resident((1, GM_D)),
            _resident((GM_GROUPS, GM_CHUNK, GM_CHUNK)),
            _resident((GM_CHUNK, GM_D)),
            _resident((GM_D, D_MODEL)),
        ],
        out_specs=pl.BlockSpec((GM_TM, D_MODEL), lambda i: (i, 0)),
        out_shape=jax.ShapeDtypeStruct((SEQ, D_MODEL), F32),
        scratch_shapes=[pltpu.VMEM((GM_TM, GM_D), F32), pltpu.VMEM((GM_TM, GM_D), BF16)],
        compiler_params=_params("parallel"),
        name="gmlp",
    )(x, vec, w_in, v_norm, w_s, bias, w_out)


def _qkv_kernel(x_ref, vec_ref, w_ref, q_ref, k_ref, v_ref, selt_ref, prevt_ref, cnt_ref,
                kmean_ref):
    i = pl.program_id(0)

    @pl.when(i == 0)
    def _init():
        kmean_ref[...] = jnp.zeros_like(kmean_ref)
        cnt_ref[...] = jnp.zeros_like(cnt_ref)

    h = _adaln(x_ref[...], vec_ref).astype(BF16)
    q = jnp.dot(h, w_ref[:, :D_MODEL], preferred_element_type=F32) * MB_SCALE
    q_ref[...] = q
    k = jnp.dot(h, w_ref[:, D_MODEL:2 * D_MODEL], preferred_element_type=F32)
    k_ref[...] = k.astype(BF16)
    nb = QKV_TM // MB_BLOCK
    for b in range(nb):
        kmean_ref[pl.ds(nb * i + b, 1), :] = jnp.mean(
            k[b * MB_BLOCK:(b + 1) * MB_BLOCK], axis=0, keepdims=True)
    v = jnp.dot(h, w_ref[:, 2 * D_MODEL:], preferred_element_type=F32)
    v_ref[...] = v.astype(BF16)

    lane = lax.broadcasted_iota(jnp.int32, (1, LANES), 1)
    blk = lax.broadcasted_iota(jnp.int32, (N_KBLOCKS, MB_BLOCK), 0)
    contract_last = (((1,), (1,)), ((), ()))
    for b in range(nb):
        j = nb * i + b
        cols = slice(b * MB_BLOCK, (b + 1) * MB_BLOCK)
        for pair in range(MB_HEADS // 2):
            km = kmean_ref[:, pair * LANES:(pair + 1) * LANES]
            qp = q[cols, pair * LANES:(pair + 1) * LANES]
            for hh in range(2):
                head = 2 * pair + hh
                km_h = jnp.where((lane // MB_HEAD_DIM) == hh, km, 0.0)
                g = lax.dot_general(km_h, qp, contract_last, preferred_element_type=F32)
                g = jnp.where(blk < j, g, NEG_INF)
                sel = jnp.zeros((N_KBLOCKS, MB_BLOCK), F32)
                for r in range(MB_TOPK):
                    m = jnp.max(g, axis=0, keepdims=True)
                    first = jnp.min(jnp.where(g == m, blk, N_KBLOCKS), axis=0, keepdims=True)
                    hit = blk == first
                    sel = jnp.where(jnp.logical_and(hit, r < j), 1.0, sel)
                    g = jnp.where(hit, BELOW_NEG_INF, g)
                prevt_ref[head:head + 1, cols] = jnp.max(
                    jnp.where(blk == j - 1, sel, 0.0), axis=0, keepdims=True)
                routed = jnp.where(blk < j - 1, sel, 0.0)
                selt_ref[head * N_KBLOCKS:(head + 1) * N_KBLOCKS, cols] = routed.astype(BF16)
                rows = slice(head * N_KBLOCKS, (head + 1) * N_KBLOCKS)
                cnt_ref[rows, :] = cnt_ref[rows, :] + jnp.sum(routed, axis=1, keepdims=True)


def _qkv(x, vec, w_qkv):
    row = pl.BlockSpec((QKV_TM, D_MODEL), lambda i: (i, 0))
    n_items = MB_HEADS * N_KBLOCKS
    return pl.pallas_call(
        _qkv_kernel,
        grid=(SEQ // QKV_TM,),
        in_specs=[row, _resident((8, D_MODEL)), _resident((D_MODEL, 3 * D_MODEL))],
        out_specs=[row, row, row,
                   pl.BlockSpec((n_items, QKV_TM), lambda i: (0, i)),
                   pl.BlockSpec((MB_HEADS, QKV_TM), lambda i: (0, i)),
                   pl.BlockSpec((n_items, LANES), lambda i: (0, 0))],
        out_shape=[jax.ShapeDtypeStruct((SEQ, D_MODEL), F32),
                   jax.ShapeDtypeStruct((SEQ, D_MODEL), BF16),
                   jax.ShapeDtypeStruct((SEQ, D_MODEL), BF16),
                   jax.ShapeDtypeStruct((n_items, SEQ), BF16),
                   jax.ShapeDtypeStruct((MB_HEADS, SEQ), F32),
                   jax.ShapeDtypeStruct((n_items, LANES), F32)],
        scratch_shapes=[pltpu.VMEM((N_KBLOCKS, D_MODEL), F32)],
        compiler_params=_params("arbitrary"),
        name="moba_qkv_gate",
    )(x, vec, w_qkv)


def _route_kernel(selt_ref, base_ref, pos_ref, carry_ref):
    i = pl.program_id(0)

    @pl.when(i == 0)
    def _init():
        carry_ref[...] = jnp.zeros_like(carry_ref)

    sel = selt_ref[...]
    before = (lax.broadcasted_iota(jnp.int32, (ROUTE_T, ROUTE_T), 0)
              < lax.broadcasted_iota(jnp.int32, (ROUTE_T, ROUTE_T), 1))
    rank = jnp.dot(sel, jnp.where(before, 1.0, 0.0).astype(BF16), preferred_element_type=F32)
    offset = base_ref[...] + carry_ref[...]
    rank = rank + jnp.concatenate([offset] * (ROUTE_T // LANES), axis=1)
    carry_ref[...] = carry_ref[...] + jnp.sum(sel.astype(F32), axis=1, keepdims=True)
    val = jnp.where(sel > 0, rank, ROUTE_NONE)
    for head in range(MB_HEADS):
        vh = val[head * N_KBLOCKS:(head + 1) * N_KBLOCKS]
        for r in range(MB_TOPK):
            m = jnp.min(vh, axis=0, keepdims=True)
            slot = head * MB_TOPK + r
            pos_ref[slot:slot + 1, :] = jnp.where(m < ROUTE_NONE, m, float(NULL_ROW)).astype(jnp.int32)
            vh = jnp.where(vh == m, ROUTE_NONE, vh)


def _route(selt, base):
    n_items = MB_HEADS * N_KBLOCKS
    return pl.pallas_call(
        _route_kernel,
        grid=(SEQ // ROUTE_T,),
        in_specs=[pl.BlockSpec((n_items, ROUTE_T), lambda i: (0, i)),
                  pl.BlockSpec((n_items, LANES), lambda i: (0, 0))],
        out_specs=pl.BlockSpec((MB_HEADS * MB_TOPK, ROUTE_T), lambda i: (0, i)),
        out_shape=jax.ShapeDtypeStruct((MB_HEADS * MB_TOPK, SEQ), jnp.int32),
        scratch_shapes=[pltpu.VMEM((n_items, LANES), F32)],
        compiler_params=_params("arbitrary"),
        name="moba_route",
    )(selt, base)


def _moba_kernel(gstart_ref, gcnt_ref, c31_ref,
                 qf_ref, pos_ref, k_ref, v_ref, kb_ref, vb_ref, prev_ref, bown_ref, bprev_ref,
                 o_ref, buf_ref, stage_ref, hold_ref):
    head = pl.program_id(0)
    phase = pl.program_id(1)
    step = pl.program_id(2)
    hh = head % 2
    lane = lax.broadcasted_iota(jnp.int32, (1, LANES), 1)
    in_head = (lane // MB_HEAD_DIM) == hh
    stat0 = (1 - hh) * MB_HEAD_DIM
    is_m_lane = lane == stat0
    is_l_lane = lane == stat0 + 1
    contract_last = (((1,), (1,)), ((), ()))
    ones_bf = jnp.ones((), BF16)

    def attend(qh, keys, vals, bias, m_old=None, acc_old=None, row_mask=None):
        s = lax.dot_general(qh, keys, contract_last, preferred_element_type=F32)
        if bias is not None:
            s = s + bias
        if row_mask is not None:
            s = jnp.where(row_mask, s, NEG_INF)
        m = jnp.max(s, axis=1, keepdims=True)
        if m_old is not None:
            m = jnp.maximum(m, m_old)
        p = jnp.exp(s - m).astype(BF16)
        acc = jnp.dot(p, jnp.where(in_head, vals, ones_bf), preferred_element_type=F32)
        if acc_old is not None:
            acc = acc + jnp.exp(m_old - m) * acc_old
        return m, acc

    @pl.when(phase == 0)
    def _dispatch():
        @pl.when(step == 0)
        def _init():
            def zero(c, _):
                buf_ref[pl.ds(pl.multiple_of(c * MB_BLOCK, MB_BLOCK), MB_BLOCK), :] = (
                    jnp.zeros((MB_BLOCK, LANES), F32))
                return 0
            lax.fori_loop(0, BUF_ROWS // MB_BLOCK, zero, 0)

        def body(a, _):
            row = qf_ref[pl.ds(a, 1), :]
            for r in range(MB_TOPK):
                buf_ref[pl.ds(pos_ref[r * STEP_ROWS + a], 1), :] = row
            return 0
        lax.fori_loop(0, STEP_ROWS, body, 0, unroll=8)

    @pl.when(phase == 1)
    def _routed():
        @pl.when(step == 0)
        def _null_row():
            buf_ref[NULL_ROW:NULL_ROW + ROUTE_ALIGN, :] = jnp.where(
                is_m_lane, NEG_INF, jnp.zeros((ROUTE_ALIGN, LANES), F32))

        far_bias = c31_ref[head]
        row_id = lax.broadcasted_iota(jnp.int32, (MB_BLOCK, 1), 0)
        for g in range(STEP_ROWS // MB_BLOCK):
            grp = head * N_KBLOCKS + step * (STEP_ROWS // MB_BLOCK) + g
            keys = k_ref[g * MB_BLOCK:(g + 1) * MB_BLOCK, :]
            vals = v_ref[g * MB_BLOCK:(g + 1) * MB_BLOCK, :]
            start = gstart_ref[grp]
            cnt = gcnt_ref[grp]

            def chunks(base, n_chunks, valid_rows):
                offs = [pl.multiple_of(base + u * MB_BLOCK, ROUTE_ALIGN) for u in range(n_chunks)]
                rows = [buf_ref[pl.ds(off, MB_BLOCK), :] for off in offs]
                news = []
                for x in rows:
                    m, acc = attend(jnp.where(in_head, x, 0.0).astype(BF16), keys, vals, None)
                    news.append(jnp.where(is_m_lane, m + far_bias, acc))
                for u in range(n_chunks):
                    new = news[u]
                    if valid_rows is not None:
                        new = jnp.where(row_id + u * MB_BLOCK < valid_rows, new, rows[u])
                    buf_ref[pl.ds(offs[u], MB_BLOCK), :] = new

            n_full = cnt // (2 * MB_BLOCK)

            def full_pair(c, _):
                chunks(start + c * (2 * MB_BLOCK), 2, None)
                return 0
            lax.fori_loop(0, n_full, full_pair, 0)
            tail = cnt - n_full * (2 * MB_BLOCK)
            tail_base = start + n_full * (2 * MB_BLOCK)

            @pl.when(tail > MB_BLOCK)
            def _tail_pair():
                chunks(tail_base, 2, tail)

            @pl.when(jnp.logical_and(tail > 0, tail <= MB_BLOCK))
            def _tail_single():
                chunks(tail_base, 1, tail)

    @pl.when(phase == 2)
    def _combine():
        lane16 = lax.broadcasted_iota(jnp.int32, (1, MB_HEADS), 1)

        def gather(a, _):
            for r in range(MB_TOPK):
                stage_ref[r, pl.ds(a, 1), :] = buf_ref[pl.ds(pos_ref[r * STEP_ROWS + a], 1), :]
            return 0
        lax.fori_loop(0, STEP_ROWS, gather, 0, unroll=8)

        for jj in range(STEP_ROWS // MB_BLOCK):
            j = step * (STEP_ROWS // MB_BLOCK) + jj
            rs = slice(jj * MB_BLOCK, (jj + 1) * MB_BLOCK)
            qh = jnp.where(in_head, qf_ref[rs, :], 0.0).astype(BF16)
            m_d, acc = attend(qh, k_ref[rs, :], v_ref[rs, :], bown_ref[...])
            if jj == 0:
                k_prev, v_prev = kb_ref[...], vb_ref[...]
            else:
                ps = slice((jj - 1) * MB_BLOCK, jj * MB_BLOCK)
                k_prev, v_prev = k_ref[ps, :], v_ref[ps, :]
            picked = jnp.sum(jnp.where(lane16 == head, prev_ref[rs, :], 0.0),
                             axis=1, keepdims=True) > 0.0
            m_d, acc = attend(qh, k_prev, v_prev, bprev_ref[...], m_d, acc, picked)

            parts = [stage_ref[r, rs, :] for r in range(MB_TOPK)]
            ms = [jnp.sum(jnp.where(is_m_lane, part, 0.0), axis=1, keepdims=True) for part in parts]
            m_all = m_d
            for m_r in ms:
                m_all = jnp.maximum(m_all, m_r)
            total = jnp.exp(m_d - m_all) * acc
            for m_r, part in zip(ms, parts):
                total = total + jnp.exp(m_r - m_all) * part
            den = jnp.sum(jnp.where(is_l_lane, total, 0.0), axis=1, keepdims=True)
            out = (total / den).astype(BF16)
            hold_rows = pl.ds(pl.multiple_of(j * MB_BLOCK, MB_BLOCK), MB_BLOCK)

            @pl.when(hh == 0)
            def _keep():
                hold_ref[hold_rows, :] = out

            @pl.when(hh == 1)
            def _emit():
                o_ref[rs, :] = jnp.where(lane < MB_HEAD_DIM, hold_ref[hold_rows, :], out)


def _moba_attention(gstart, gcnt, c31, qf, pos, k, v, prevc, bias_own, bias_prev):
    last = NSTEPS - 1

    def stay(ph, st):
        return jnp.where(ph == 1, last, st)

    kv_big = pl.BlockSpec((STEP_ROWS, LANES), lambda h, ph, st, *_: (jnp.where(ph == 0, 0, st), h // 2))
    kv_prev = pl.BlockSpec(
        (MB_BLOCK, LANES),
        lambda h, ph, st, *_: (jnp.where(ph == 2, jnp.maximum(st * (STEP_ROWS // MB_BLOCK) - 1, 0), 0), h // 2))
    bias = pl.BlockSpec((None, MB_BLOCK, MB_BLOCK), lambda h, ph, st, *_: (h, 0, 0))
    return pl.pallas_call(
        _moba_kernel,
        grid_spec=pltpu.PrefetchScalarGridSpec(
            num_scalar_prefetch=3,
            grid=(MB_HEADS, 3, NSTEPS),
            in_specs=[
                pl.BlockSpec((STEP_ROWS, LANES), lambda h, ph, st, *_: (stay(ph, st), h // 2)),
                pl.BlockSpec((MB_TOPK * STEP_ROWS,), lambda h, ph, st, *_: (h * NSTEPS + stay(ph, st),),
                             memory_space=pltpu.SMEM),
                kv_big, kv_big, kv_prev, kv_prev,
                pl.BlockSpec((STEP_ROWS, MB_HEADS), lambda h, ph, st, *_: (jnp.where(ph == 2, st, 0), 0)),
                bias, bias,
            ],
            out_specs=pl.BlockSpec(
                (STEP_ROWS, LANES),
                lambda h, ph, st, *_: (jnp.where(jnp.logical_and(ph == 2, h % 2 == 1), st, 0), h // 2)),
            scratch_shapes=[pltpu.VMEM((BUF_ROWS, LANES), F32),
                            pltpu.VMEM((MB_TOPK, STEP_ROWS, LANES), F32),
                            pltpu.VMEM((SEQ, LANES), BF16)],
        ),
        out_shape=jax.ShapeDtypeStruct((SEQ, D_MODEL), BF16),
        compiler_params=_params("arbitrary", "arbitrary", "arbitrary"),
        name="moba_attention",
    )(gstart, gcnt, c31, qf, pos, k, v, k, v, prevc, bias_own, bias_prev)


def _proj_kernel(x_ref, a_ref, vec_ref, w_ref, o_ref):
    y = jnp.dot(a_ref[...], w_ref[...], preferred_element_type=F32)
    o_ref[...] = x_ref[...] + vec_ref[3:4, :] * y


def _out_proj(x, a, vec, w_o):
    row = pl.BlockSpec((PROJ_TM, D_MODEL), lambda i: (i, 0))
    return pl.pallas_call(
        _proj_kernel,
        grid=(SEQ // PROJ_TM,),
        in_specs=[row, row, _resident((8, D_MODEL)), _resident((D_MODEL, D_MODEL))],
        out_specs=row,
        out_shape=jax.ShapeDtypeStruct((SEQ, D_MODEL), F32),
        compiler_params=_params("parallel"),
        name="moba_out_proj",
    )(x, a, vec, w_o)


def _t5_bucket_np(rel):
    n = np.maximum(rel, 0)
    nf = np.maximum(n, T5_MAX_EXACT).astype(np.float32)
    large = T5_MAX_EXACT + (np.log(nf / np.float32(T5_MAX_EXACT))
                            / np.float32(math.log(T5_MAX_DISTANCE / T5_MAX_EXACT))
                            * np.float32(T5_NUM_BUCKETS - T5_MAX_EXACT)).astype(np.int32)
    large = np.minimum(large, T5_NUM_BUCKETS - 1)
    return np.where(n < T5_MAX_EXACT, n, large)


def _moba_bias_tables(rel_bias):
    a = np.arange(MB_BLOCK)
    rel_own = a[:, None] - a[None, :]
    table = rel_bias.astype(F32).T

    def expand(rel):
        bucket = jnp.asarray(_t5_bucket_np(rel).reshape(-1).astype(np.int32))
        onehot = (bucket[None, :] == jnp.arange(T5_NUM_BUCKETS)[:, None]).astype(F32)
        return jnp.dot(table, onehot, precision=lax.Precision.HIGHEST).reshape(
            MB_HEADS, MB_BLOCK, MB_BLOCK)

    b_own = jnp.where(jnp.asarray(rel_own >= 0)[None], expand(rel_own), NEG_INF)
    return b_own, expand(rel_own + MB_BLOCK), table[:, T5_NUM_BUCKETS - 1]


def _moba(xs, vec, w_qkv, w_o, rel_bias):
    qf, k, v, selt, prevt, cnt = _qkv(xs, vec, w_qkv)
    cnt = cnt[:, 0].astype(jnp.int32).reshape(MB_HEADS, N_KBLOCKS)
    padded = (cnt + ROUTE_ALIGN - 1) // ROUTE_ALIGN * ROUTE_ALIGN
    start = jnp.cumsum(padded, axis=1) - padded
    base = jnp.broadcast_to(start.reshape(-1, 1).astype(F32), (MB_HEADS * N_KBLOCKS, LANES))
    pos = _route(selt, base).reshape(MB_HEADS, MB_TOPK, NSTEPS, STEP_ROWS)
    pos = pos.transpose(0, 2, 1, 3).reshape(-1)
    b_own, b_prev, c31 = _moba_bias_tables(rel_bias)
    a = _moba_attention(start.reshape(-1), cnt.reshape(-1), c31, qf, pos, k, v, prevt.T, b_own, b_prev)
    return _out_proj(xs, a, vec, w_o)


def _vec(norm_g, mod3):
    return jnp.concatenate([norm_g[None, :], mod3, jnp.zeros((4, D_MODEL), F32)], axis=0)


def kernel(x, c, rel_bias, mod_w, mod_b, norm_g, ffn_w_in, ffn_w_out, gmlp_w_in, gmlp_v_norm,
           gmlp_w_s, gmlp_b_s, gmlp_w_out, moba_w_qkv, moba_w_o, final_norm):
    assert x.shape == (1, SEQ, D_MODEL)
    xs = x.reshape(SEQ, D_MODEL)
    mod = _modulation(c.reshape(D_MODEL, 1), mod_w, mod_b).reshape(DEPTH, N_SUBLAYERS, 3, D_MODEL)
    fin = final_norm.reshape(1, D_MODEL)

    for i in range(DEPTH):
        li = i // 2
        xs = _ffn(xs, _vec(norm_g[i, 0], mod[i, 0]), ffn_w_in[i, 0].astype(BF16),
                  ffn_w_out[i, 0].astype(BF16), fin, False)
        vec = _vec(norm_g[i, 1], mod[i, 1])
        if i % 2 == 0:
            bias = jnp.repeat(gmlp_b_s[li].T, GM_GROUP_DIM, axis=1)
            xs = _gmlp(xs, vec, gmlp_w_in[li].astype(BF16), gmlp_v_norm[li].reshape(1, GM_D),
                       gmlp_w_s[li].astype(BF16), bias, gmlp_w_out[li].astype(BF16))
        else:
            xs = _moba(xs, vec, moba_w_qkv[li].astype(BF16), moba_w_o[li].astype(BF16), rel_bias)
        xs = _ffn(xs, _vec(norm_g[i, 2], mod[i, 2]), ffn_w_in[i, 1].astype(BF16),
                  ffn_w_out[i, 1].astype(BF16), fin, i == DEPTH - 1)
    return xs.reshape(1, SEQ, D_MODEL)
```

```python
import functools
import math

import numpy as np
import jax
import jax.numpy as jnp
from jax import lax
from jax.experimental import pallas as pl
from jax.experimental.pallas import tpu as pltpu

F32 = jnp.float32
BF16 = jnp.bfloat16

D_MODEL = 1024
SEQ = 16384
DEPTH = 2
N_SUBLAYERS = 3
RMS_EPS = 1e-6
D_FF = 2816

GM_D = 3 * D_MODEL
GM_GROUPS = 16
GM_GROUP_DIM = GM_D // GM_GROUPS
GM_CHUNK = 128
GM_PAIR = 2 * GM_GROUP_DIM

MB_HEADS = 16
MB_HEAD_DIM = 64
MB_BLOCK = 256
MB_TOPK = 3
MB_SCALE = MB_HEAD_DIM ** -0.5
N_KBLOCKS = SEQ // MB_BLOCK

T5_NUM_BUCKETS = 32
T5_MAX_EXACT = 16
T5_MAX_DISTANCE = 128
NEG_INF = -1e30
BELOW_NEG_INF = -3e38

LANES = 128
VMEM_LIMIT = 56 * 1024 * 1024

MOD_TN = 1536
FFN_TM = 512
FFN_CHUNK = 1408
GM_TM = 256
GM_VCHUNK = 768
QKV_TM = 512
PROJ_TM = 512

ROUTE_T = 512
ROUTE_ALIGN = 8
STEP_ROWS = 1024
NSTEPS = SEQ // STEP_ROWS
BUF_DATA_ROWS = MB_TOPK * SEQ + N_KBLOCKS * ROUTE_ALIGN
BUF_ROWS = BUF_DATA_ROWS + 3 * MB_BLOCK
NULL_ROW = BUF_ROWS - ROUTE_ALIGN
ROUTE_NONE = 1e9
MAX_CHUNKS = MB_TOPK * SEQ // MB_BLOCK + N_KBLOCKS


def _resident(shape):
    nd = len(shape)
    return pl.BlockSpec(shape, lambda *_: (0,) * nd, pipeline_mode=pl.Buffered(1))


def _params(*sem):
    return pltpu.CompilerParams(dimension_semantics=sem, vmem_limit_bytes=VMEM_LIMIT)


def _rms(x):
    return x * lax.rsqrt(jnp.mean(x * x, axis=-1, keepdims=True) + RMS_EPS)


def _adaln(x, vec_ref):
    y = _rms(x) * vec_ref[0:1, :]
    return y * (1.0 + vec_ref[2:3, :]) + vec_ref[1:2, :]


def _gelu(x):
    return 0.5 * x * (1.0 + lax.erf(x * np.float32(math.sqrt(0.5))))


def _mod_kernel(c_ref, w_ref, b_ref, o_ref):
    c = c_ref[...]
    c_act = c * jax.nn.sigmoid(c)
    o_ref[...] = jnp.sum(c_act * w_ref[...], axis=0, keepdims=True) + b_ref[...]


def _modulation(c_col, mod_w, mod_b):
    n = mod_w.shape[-1]
    return pl.pallas_call(
        _mod_kernel,
        grid=(DEPTH, n // MOD_TN),
        in_specs=[
            pl.BlockSpec((D_MODEL, 1), lambda i, j: (0, 0)),
            pl.BlockSpec((None, D_MODEL, MOD_TN), lambda i, j: (i, 0, j)),
            pl.BlockSpec((None, 1, MOD_TN), lambda i, j: (i, 0, j)),
        ],
        out_specs=pl.BlockSpec((None, 1, MOD_TN), lambda i, j: (i, 0, j)),
        out_shape=jax.ShapeDtypeStruct((DEPTH, 1, n), F32),
        compiler_params=_params("arbitrary", "arbitrary"),
        name="modulation",
    )(c_col, mod_w, mod_b.reshape(DEPTH, 1, n))


def _ffn_kernel(x_ref, vec_ref, win_ref, wout_ref, fin_ref, o_ref, act_ref, *, final):
    x = x_ref[...]
    h = _adaln(x, vec_ref).astype(BF16)
    for c in range(D_FF // FFN_CHUNK):
        lo = c * FFN_CHUNK
        g = jnp.dot(h, win_ref[:, lo:lo + FFN_CHUNK], preferred_element_type=F32)
        u = jnp.dot(h, win_ref[:, D_FF + lo:D_FF + lo + FFN_CHUNK], preferred_element_type=F32)
        act_ref[:, lo:lo + FFN_CHUNK] = (g * jax.nn.sigmoid(g) * u).astype(BF16)
    y = jnp.dot(act_ref[...], wout_ref[...], preferred_element_type=F32)
    out = x + (0.5 * vec_ref[3:4, :]) * y
    if final:
        out = _rms(out) * fin_ref[...]
    o_ref[...] = out


def _ffn(x, vec, w_in, w_out, fin, final):
    return pl.pallas_call(
        functools.partial(_ffn_kernel, final=final),
        grid=(SEQ // FFN_TM,),
        in_specs=[
            pl.BlockSpec((FFN_TM, D_MODEL), lambda i: (i, 0)),
            _resident((8, D_MODEL)),
            _resident((D_MODEL, 2 * D_FF)),
            _resident((D_FF, D_MODEL)),
            _resident((1, D_MODEL)),
        ],
        out_specs=pl.BlockSpec((FFN_TM, D_MODEL), lambda i: (i, 0)),
        out_shape=jax.ShapeDtypeStruct((SEQ, D_MODEL), F32),
        scratch_shapes=[pltpu.VMEM((FFN_TM, D_FF), BF16)],
        compiler_params=_params("parallel"),
        name="ffn_final" if final else "ffn",
    )(x, vec, w_in, w_out, fin)


def _gmlp_kernel(x_ref, vec_ref, win_ref, vnorm_ref, ws_ref, bias_ref, wout_ref,
                 o_ref, v_ref, g_ref):
    x = x_ref[...]
    h = _adaln(x, vec_ref).astype(BF16)

    ssq = jnp.zeros((GM_TM, 1), F32)
    for c in range(GM_D // GM_VCHUNK):
        lo = c * GM_VCHUNK
        v = _gelu(jnp.dot(h, win_ref[:, GM_D + lo:GM_D + lo + GM_VCHUNK],
                          preferred_element_type=F32))
        v_ref[:, lo:lo + GM_VCHUNK] = v
        ssq = ssq + jnp.sum(v * v, axis=-1, keepdims=True)
    r = lax.rsqrt(ssq * (1.0 / GM_D) + RMS_EPS)

    row = lax.broadcasted_iota(jnp.int32, (GM_CHUNK, GM_CHUNK), 0)
    col = lax.broadcasted_iota(jnp.int32, (GM_CHUNK, GM_CHUNK), 1)
    causal = row >= col
    low_half = lax.broadcasted_iota(jnp.int32, (GM_CHUNK, LANES), 1) < (GM_GROUP_DIM - LANES)

    for p in range(GM_GROUPS // 2):
        lo = p * GM_PAIR
        vn = (v_ref[:, lo:lo + GM_PAIR] * r * vnorm_ref[:, lo:lo + GM_PAIR]).astype(BF16)
        u = _gelu(jnp.dot(h, win_ref[:, lo:lo + GM_PAIR], preferred_element_type=F32))
        w0 = jnp.where(causal, ws_ref[2 * p], jnp.zeros((), BF16))
        w1 = jnp.where(causal, ws_ref[2 * p + 1], jnp.zeros((), BF16))
        for c in range(GM_TM // GM_CHUNK):
            rows = slice(c * GM_CHUNK, (c + 1) * GM_CHUNK)
            vc = vn[rows]
            a = jnp.dot(w0, vc[:, :2 * LANES], preferred_element_type=F32)
            b = jnp.dot(w1, vc[:, LANES:], preferred_element_type=F32)
            mid = jnp.where(low_half, a[:, LANES:], b[:, :LANES])
            sv = jnp.concatenate([a[:, :LANES], mid, b[:, LANES:]], axis=1)
            sv = sv + bias_ref[:, lo:lo + GM_PAIR]
            g_ref[rows, lo:lo + GM_PAIR] = (u[rows] * sv).astype(BF16)

    y = jnp.dot(g_ref[...], wout_ref[...], preferred_element_type=F32)
    o_ref[...] = x + vec_ref[3:4, :] * y


def _gmlp(x, vec, w_in, v_norm, w_s, bias, w_out):
    return pl.pallas_call(
        _gmlp_kernel,
        grid=(SEQ // GM_TM,),
        in_specs=[
            pl.BlockSpec((GM_TM, D_MODEL), lambda i: (i, 0)),
            _resident((8, D_MODEL)),
            _resident((D_MODEL, 2 * GM_D)),
            _resident((1, GM_D)),
            _resident((GM_GROUPS, GM_CHUNK, GM_CHUNK)),
            _resident((GM_CHUNK, GM_D)),
            _resident((GM_D, D_MODEL)),
        ],
        out_specs=pl.BlockSpec((GM_TM, D_MODEL), lambda i: (i, 0)),
        out_shape=jax.ShapeDtypeStruct((SEQ, D_MODEL), F32),
        scratch_shapes=[pltpu.VMEM((GM_TM, GM_D), F32), pltpu.VMEM((GM_TM, GM_D), BF16)],
        compiler_params=_params("parallel"),
        name="gmlp",
    )(x, vec, w_in, v_norm, w_s, bias, w_out)


def _qkv_kernel(x_ref, vec_ref, w_ref, q_ref, k_ref, v_ref, selt_ref, prevt_ref, cnt_ref,
                kmean_ref):
    i = pl.program_id(0)

    @pl.when(i == 0)
    def _init():
        kmean_ref[...] = jnp.zeros_like(kmean_ref)
        cnt_ref[...] = jnp.zeros_like(cnt_ref)

    h = _adaln(x_ref[...], vec_ref).astype(BF16)
    q = jnp.dot(h, w_ref[:, :D_MODEL], preferred_element_type=F32) * MB_SCALE
    q_ref[...] = q
    k = jnp.dot(h, w_ref[:, D_MODEL:2 * D_MODEL], preferred_element_type=F32)
    k_ref[...] = k.astype(BF16)
    nb = QKV_TM // MB_BLOCK
    for b in range(nb):
        kmean_ref[pl.ds(nb * i + b, 1), :] = jnp.mean(
            k[b * MB_BLOCK:(b + 1) * MB_BLOCK], axis=0, keepdims=True)
    v = jnp.dot(h, w_ref[:, 2 * D_MODEL:], preferred_element_type=F32)
    v_ref[...] = v.astype(BF16)

    lane = lax.broadcasted_iota(jnp.int32, (1, LANES), 1)
    blk = lax.broadcasted_iota(jnp.int32, (N_KBLOCKS, MB_BLOCK), 0)
    contract_last = (((1,), (1,)), ((), ()))
    for b in range(nb):
        j = nb * i + b
        cols = slice(b * MB_BLOCK, (b + 1) * MB_BLOCK)
        for pair in range(MB_HEADS // 2):
            km = kmean_ref[:, pair * LANES:(pair + 1) * LANES]
            qp = q[cols, pair * LANES:(pair + 1) * LANES]
            for hh in range(2):
                head = 2 * pair + hh
                km_h = jnp.where((lane // MB_HEAD_DIM) == hh, km, 0.0)
                g = lax.dot_general(km_h, qp, contract_last, preferred_element_type=F32)
                g = jnp.where(blk < j, g, NEG_INF)
                sel = jnp.zeros((N_KBLOCKS, MB_BLOCK), F32)
                for r in range(MB_TOPK):
                    m = jnp.max(g, axis=0, keepdims=True)
                    first = jnp.min(jnp.where(g == m, blk, N_KBLOCKS), axis=0, keepdims=True)
                    hit = blk == first
                    sel = jnp.where(jnp.logical_and(hit, r < j), 1.0, sel)
                    g = jnp.where(hit, BELOW_NEG_INF, g)
                prevt_ref[head:head + 1, cols] = jnp.max(
                    jnp.where(blk == j - 1, sel, 0.0), axis=0, keepdims=True)
                routed = jnp.where(blk < j - 1, sel, 0.0)
                selt_ref[head * N_KBLOCKS:(head + 1) * N_KBLOCKS, cols] = routed.astype(BF16)
                rows = slice(head * N_KBLOCKS, (head + 1) * N_KBLOCKS)
                cnt_ref[rows, :] = cnt_ref[rows, :] + jnp.sum(routed, axis=1, keepdims=True)


def _qkv(x, vec, w_qkv):
    row = pl.BlockSpec((QKV_TM, D_MODEL), lambda i: (i, 0))
    n_items = MB_HEADS * N_KBLOCKS
    return pl.pallas_call(
        _qkv_kernel,
        grid=(SEQ // QKV_TM,),
        in_specs=[row, _resident((8, D_MODEL)), _resident((D_MODEL, 3 * D_MODEL))],
        out_specs=[row, row, row,
                   pl.BlockSpec((n_items, QKV_TM), lambda i: (0, i)),
                   pl.BlockSpec((MB_HEADS, QKV_TM), lambda i: (0, i)),
                   pl.BlockSpec((n_items, LANES), lambda i: (0, 0))],
        out_shape=[jax.ShapeDtypeStruct((SEQ, D_MODEL), F32),
                   jax.ShapeDtypeStruct((SEQ, D_MODEL), BF16),
                   jax.ShapeDtypeStruct((SEQ, D_MODEL), BF16),
                   jax.ShapeDtypeStruct((n_items, SEQ), BF16),
                   jax.ShapeDtypeStruct((MB_HEADS, SEQ), F32),
                   jax.ShapeDtypeStruct((n_items, LANES), F32)],
        scratch_shapes=[pltpu.VMEM((N_KBLOCKS, D_MODEL), F32)],
        compiler_params=_params("arbitrary"),
        name="moba_qkv_gate",
    )(x, vec, w_qkv)


def _route_kernel(selt_ref, base_ref, pos_ref, carry_ref):
    i = pl.program_id(0)

    @pl.when(i == 0)
    def _init():
        carry_ref[...] = jnp.zeros_like(carry_ref)

    sel = selt_ref[...]
    before = (lax.broadcasted_iota(jnp.int32, (ROUTE_T, ROUTE_T), 0)
              < lax.broadcasted_iota(jnp.int32, (ROUTE_T, ROUTE_T), 1))
    rank = jnp.dot(sel, jnp.where(before, 1.0, 0.0).astype(BF16), preferred_element_type=F32)
    offset = base_ref[...] + carry_ref[...]
    rank = rank + jnp.concatenate([offset] * (ROUTE_T // LANES), axis=1)
    carry_ref[...] = carry_ref[...] + jnp.sum(sel.astype(F32), axis=1, keepdims=True)
    val = jnp.where(sel > 0, rank, ROUTE_NONE)
    for head in range(MB_HEADS):
        vh = val[head * N_KBLOCKS:(head + 1) * N_KBLOCKS]
        for r in range(MB_TOPK):
            m = jnp.min(vh, axis=0, keepdims=True)
            slot = head * MB_TOPK + r
            pos_ref[slot:slot + 1, :] = jnp.where(m < ROUTE_NONE, m, float(NULL_ROW)).astype(jnp.int32)
            vh = jnp.where(vh == m, ROUTE_NONE, vh)


def _route(selt, base):
    n_items = MB_HEADS * N_KBLOCKS
    return pl.pallas_call(
        _route_kernel,
        grid=(SEQ // ROUTE_T,),
        in_specs=[pl.BlockSpec((n_items, ROUTE_T), lambda i: (0, i)),
                  pl.BlockSpec((n_items, LANES), lambda i: (0, 0))],
        out_specs=pl.BlockSpec((MB_HEADS * MB_TOPK, ROUTE_T), lambda i: (0, i)),
        out_shape=jax.ShapeDtypeStruct((MB_HEADS * MB_TOPK, SEQ), jnp.int32),
        scratch_shapes=[pltpu.VMEM((n_items, LANES), F32)],
        compiler_params=_params("arbitrary"),
        name="moba_route",
    )(selt, base)


def _moba_kernel(choff_ref, chblk_ref, chvalid_ref, nch_ref, c31_ref,
                 qf_ref, pos_ref, k_ref, v_ref, prev_ref, bown_ref, bprev_ref,
                 o_ref, buf_ref, stage_ref, hold_ref, s0_ref, s1_ref, p0_ref, p1_ref, m0_ref, m1_ref):
    head = pl.program_id(0)
    t = pl.program_id(1)
    hh = head % 2
    lane = lax.broadcasted_iota(jnp.int32, (1, LANES), 1)
    in_head = (lane // MB_HEAD_DIM) == hh
    stat0 = (1 - hh) * MB_HEAD_DIM
    is_m_lane = lane == stat0
    is_l_lane = lane == stat0 + 1
    contract_last = (((1,), (1,)), ((), ()))
    ones_bf = jnp.ones((), BF16)

    def queries(rows):
        return jnp.where(in_head, rows, 0.0).astype(BF16)

    def values(rows):
        return jnp.where(in_head, rows, ones_bf)

    @pl.when(t < NSTEPS)
    def _dispatch():
        @pl.when(t == 0)
        def _init():
            def zero(c, _):
                buf_ref[pl.ds(pl.multiple_of(c * MB_BLOCK, MB_BLOCK), MB_BLOCK), :] = (
                    jnp.zeros((MB_BLOCK, LANES), F32))
                return 0
            lax.fori_loop(0, BUF_ROWS // MB_BLOCK, zero, 0)

        def body(a, _):
            row = qf_ref[pl.ds(a, 1), :]
            for r in range(MB_TOPK):
                buf_ref[pl.ds(pos_ref[r * STEP_ROWS + a], 1), :] = row
            return 0
        lax.fori_loop(0, STEP_ROWS, body, 0, unroll=8)

    @pl.when(t == NSTEPS)
    def _routed():
        buf_ref[NULL_ROW:NULL_ROW + ROUTE_ALIGN, :] = jnp.where(
            is_m_lane, NEG_INF, jnp.zeros((ROUTE_ALIGN, LANES), F32))
        n = nch_ref[head]
        far_bias = c31_ref[head]
        row_id = lax.broadcasted_iota(jnp.int32, (MB_BLOCK, 1), 0)

        def meta(i):
            idx = head * MAX_CHUNKS + jnp.minimum(i, n - 1)
            off = pl.multiple_of(choff_ref[idx], ROUTE_ALIGN)
            blk = pl.multiple_of(chblk_ref[idx] * MB_BLOCK, MB_BLOCK)
            return off, blk, jnp.where(i < n, chvalid_ref[idx], 0)

        def scores(rows, blk):
            return lax.dot_general(queries(rows), k_ref[pl.ds(blk, MB_BLOCK), :], contract_last,
                                   preferred_element_type=F32)

        def probs(s_ref, p_ref, m_ref):
            s = s_ref[...]
            m = jnp.max(s, axis=1, keepdims=True)
            p_ref[...] = jnp.exp(s - m).astype(BF16)
            m_ref[...] = jnp.broadcast_to(m, (MB_BLOCK, LANES))

        def finish(off, blk, valid, old, p_ref, m_ref):
            acc = jnp.dot(p_ref[...], values(v_ref[pl.ds(blk, MB_BLOCK), :]), preferred_element_type=F32)
            new = jnp.where(is_m_lane, m_ref[...] + far_bias, acc)
            buf_ref[pl.ds(off, MB_BLOCK), :] = jnp.where(row_id < valid, new, old)

        @pl.when(n > 0)
        def _chunks():
            off, blk, _ = meta(0)
            s0_ref[...] = scores(buf_ref[pl.ds(off, MB_BLOCK), :], blk)
            off, blk, _ = meta(1)
            s1_ref[...] = scores(buf_ref[pl.ds(off, MB_BLOCK), :], blk)
            probs(s0_ref, p0_ref, m0_ref)

            def two_chunks(c, _):
                i = 2 * c
                off_a, blk_a, _ = meta(i + 2)
                off_b, blk_b, _ = meta(i + 3)
                off_c, blk_c, valid_c = meta(i)
                off_d, blk_d, valid_d = meta(i + 1)
                rows_a = buf_ref[pl.ds(off_a, MB_BLOCK), :]
                rows_b = buf_ref[pl.ds(off_b, MB_BLOCK), :]
                old_c = buf_ref[pl.ds(off_c, MB_BLOCK), :]
                old_d = buf_ref[pl.ds(off_d, MB_BLOCK), :]

                s_a = scores(rows_a, blk_a)
                probs(s1_ref, p1_ref, m1_ref)
                finish(off_c, blk_c, valid_c, old_c, p0_ref, m0_ref)
                s0_ref[...] = s_a

                s_b = scores(rows_b, blk_b)
                probs(s0_ref, p0_ref, m0_ref)
                finish(off_d, blk_d, valid_d, old_d, p1_ref, m1_ref)
                s1_ref[...] = s_b
                return 0
            lax.fori_loop(0, (n + 1) // 2, two_chunks, 0)

    @pl.when(t > NSTEPS)
    def _combine():
        step = t - (NSTEPS + 1)
        lane16 = lax.broadcasted_iota(jnp.int32, (1, MB_HEADS), 1)

        def gather(a, _):
            for r in range(MB_TOPK):
                stage_ref[r, pl.ds(a, 1), :] = buf_ref[pl.ds(pos_ref[r * STEP_ROWS + a], 1), :]
            return 0
        lax.fori_loop(0, STEP_ROWS, gather, 0, unroll=8)

        n_sub = STEP_ROWS // MB_BLOCK
        windows, all_scores = [], []
        for jj in range(n_sub):
            j = step * n_sub + jj
            rs = slice(jj * MB_BLOCK, (jj + 1) * MB_BLOCK)
            win = pl.ds(pl.multiple_of(jnp.maximum(j - 1, 0) * MB_BLOCK, MB_BLOCK), 2 * MB_BLOCK)
            windows.append(win)
            all_scores.append(lax.dot_general(queries(qf_ref[rs, :]), k_ref[win, :], contract_last,
                                              preferred_element_type=F32))

        for jj in range(n_sub):
            j = step * n_sub + jj
            rs = slice(jj * MB_BLOCK, (jj + 1) * MB_BLOCK)
            s = all_scores[jj]
            picked = jnp.sum(jnp.where(lane16 == head, prev_ref[rs, :], 0.0),
                             axis=1, keepdims=True) > 0.0
            bias_lo, bias_hi = bprev_ref[...], bown_ref[...]
            if jj == 0:
                first = step == 0
                bias_lo = jnp.where(first, bown_ref[...], bias_lo)
                bias_hi = jnp.where(first, NEG_INF, bias_hi)
                picked = jnp.logical_or(picked, first)
            s_lo = jnp.where(picked, s[:, :MB_BLOCK] + bias_lo, NEG_INF)
            s_hi = s[:, MB_BLOCK:] + bias_hi
            m_d = jnp.maximum(jnp.max(s_lo, axis=1, keepdims=True), jnp.max(s_hi, axis=1, keepdims=True))
            p = jnp.concatenate([jnp.exp(s_lo - m_d), jnp.exp(s_hi - m_d)], axis=1).astype(BF16)
            acc = jnp.dot(p, values(v_ref[windows[jj], :]), preferred_element_type=F32)

            parts = [stage_ref[r, rs, :] for r in range(MB_TOPK)]
            ms = [jnp.sum(jnp.where(is_m_lane, part, 0.0), axis=1, keepdims=True) for part in parts]
            m_all = m_d
            for m_r in ms:
                m_all = jnp.maximum(m_all, m_r)
            total = jnp.exp(m_d - m_all) * acc
            for m_r, part in zip(ms, parts):
                total = total + jnp.exp(m_r - m_all) * part
            den = jnp.sum(jnp.where(is_l_lane, total, 0.0), axis=1, keepdims=True)
            out = (total / den).astype(BF16)
            hold_rows = pl.ds(pl.multiple_of(j * MB_BLOCK, MB_BLOCK), MB_BLOCK)

            @pl.when(hh == 0)
            def _keep():
                hold_ref[hold_rows, :] = out

            @pl.when(hh == 1)
            def _emit():
                o_ref[rs, :] = jnp.where(lane < MB_HEAD_DIM, hold_ref[hold_rows, :], out)


def _moba_attention(chunk_meta, c31, qf, pos, k, v, prevc, bias_own, bias_prev):
    def seq_step(t):
        return jnp.where(t < NSTEPS, t, jnp.maximum(t - (NSTEPS + 1), 0) + jnp.where(t == NSTEPS, NSTEPS - 1, 0))

    def combine_step(t):
        return jnp.maximum(t - (NSTEPS + 1), 0)

    pair_resident = pl.BlockSpec((SEQ, LANES), lambda h, t, *_: (0, h // 2), pipeline_mode=pl.Buffered(1))
    bias = pl.BlockSpec((None, MB_BLOCK, MB_BLOCK), lambda h, t, *_: (h, 0, 0))
    chunk_f32 = pltpu.VMEM((MB_BLOCK, MB_BLOCK), F32)
    chunk_bf16 = pltpu.VMEM((MB_BLOCK, MB_BLOCK), BF16)
    chunk_max = pltpu.VMEM((MB_BLOCK, LANES), F32)
    return pl.pallas_call(
        _moba_kernel,
        grid_spec=pltpu.PrefetchScalarGridSpec(
            num_scalar_prefetch=5,
            grid=(MB_HEADS, 2 * NSTEPS + 1),
            in_specs=[
                pl.BlockSpec((STEP_ROWS, LANES), lambda h, t, *_: (seq_step(t), h // 2)),
                pl.BlockSpec((MB_TOPK * STEP_ROWS,), lambda h, t, *_: (h * NSTEPS + seq_step(t),),
                             memory_space=pltpu.SMEM),
                pair_resident, pair_resident,
                pl.BlockSpec((STEP_ROWS, MB_HEADS), lambda h, t, *_: (combine_step(t), 0)),
                bias, bias,
            ],
            out_specs=pl.BlockSpec(
                (STEP_ROWS, LANES),
                lambda h, t, *_: (jnp.where(h % 2 == 1, combine_step(t), 0), h // 2)),
            scratch_shapes=[pltpu.VMEM((BUF_ROWS, LANES), F32),
                            pltpu.VMEM((MB_TOPK, STEP_ROWS, LANES), F32),
                            pltpu.VMEM((SEQ, LANES), BF16),
                            chunk_f32, chunk_f32, chunk_bf16, chunk_bf16, chunk_max, chunk_max],
        ),
        out_shape=jax.ShapeDtypeStruct((SEQ, D_MODEL), BF16),
        compiler_params=_params("arbitrary", "arbitrary"),
        name="moba_attention",
    )(*chunk_meta, c31, qf, pos, k, v, prevc, bias_own, bias_prev)


def _proj_kernel(x_ref, a_ref, vec_ref, w_ref, o_ref):
    y = jnp.dot(a_ref[...], w_ref[...], preferred_element_type=F32)
    o_ref[...] = x_ref[...] + vec_ref[3:4, :] * y


def _out_proj(x, a, vec, w_o):
    row = pl.BlockSpec((PROJ_TM, D_MODEL), lambda i: (i, 0))
    return pl.pallas_call(
        _proj_kernel,
        grid=(SEQ // PROJ_TM,),
        in_specs=[row, row, _resident((8, D_MODEL)), _resident((D_MODEL, D_MODEL))],
        out_specs=row,
        out_shape=jax.ShapeDtypeStruct((SEQ, D_MODEL), F32),
        compiler_params=_params("parallel"),
        name="moba_out_proj",
    )(x, a, vec, w_o)


def _t5_bucket_np(rel):
    n = np.maximum(rel, 0)
    nf = np.maximum(n, T5_MAX_EXACT).astype(np.float32)
    large = T5_MAX_EXACT + (np.log(nf / np.float32(T5_MAX_EXACT))
                            / np.float32(math.log(T5_MAX_DISTANCE / T5_MAX_EXACT))
                            * np.float32(T5_NUM_BUCKETS - T5_MAX_EXACT)).astype(np.int32)
    large = np.minimum(large, T5_NUM_BUCKETS - 1)
    return np.where(n < T5_MAX_EXACT, n, large)


def _moba_bias_tables(rel_bias):
    a = np.arange(MB_BLOCK)
    rel_own = a[:, None] - a[None, :]
    table = rel_bias.astype(F32).T

    def expand(rel):
        bucket = jnp.asarray(_t5_bucket_np(rel).reshape(-1).astype(np.int32))
        onehot = (bucket[None, :] == jnp.arange(T5_NUM_BUCKETS)[:, None]).astype(F32)
        return jnp.dot(table, onehot, precision=lax.Precision.HIGHEST).reshape(
            MB_HEADS, MB_BLOCK, MB_BLOCK)

    b_own = jnp.where(jnp.asarray(rel_own >= 0)[None], expand(rel_own), NEG_INF)
    return b_own, expand(rel_own + MB_BLOCK), table[:, T5_NUM_BUCKETS - 1]


def _chunk_tables(cnt, start):
    nch = (cnt + MB_BLOCK - 1) // MB_BLOCK
    cum = jnp.cumsum(nch, axis=1)
    i = jnp.arange(MAX_CHUNKS)
    blk = jnp.sum(cum[:, None, :] <= i[None, :, None], axis=2)
    blk = jnp.minimum(blk, N_KBLOCKS - 1)
    onehot = blk[:, :, None] == jnp.arange(N_KBLOCKS)[None, None, :]

    def pick(x):
        return jnp.sum(jnp.where(onehot, x[:, None, :], 0), axis=2)

    def flat(x):
        return x.reshape(-1).astype(jnp.int32)
    within = i[None, :] - pick(cum - nch)
    off = pick(start) + within * MB_BLOCK
    valid = jnp.clip(pick(cnt) - within * MB_BLOCK, 0, MB_BLOCK)
    return flat(off), flat(blk), flat(valid), flat(cum[:, -1])


def _moba(xs, vec, w_qkv, w_o, rel_bias):
    qf, k, v, selt, prevt, cnt = _qkv(xs, vec, w_qkv)
    cnt = cnt[:, 0].astype(jnp.int32).reshape(MB_HEADS, N_KBLOCKS)
    padded = (cnt + ROUTE_ALIGN - 1) // ROUTE_ALIGN * ROUTE_ALIGN
    start = jnp.cumsum(padded, axis=1) - padded
    base = jnp.broadcast_to(start.reshape(-1, 1).astype(F32), (MB_HEADS * N_KBLOCKS, LANES))
    pos = _route(selt, base).reshape(MB_HEADS, MB_TOPK, NSTEPS, STEP_ROWS)
    pos = pos.transpose(0, 2, 1, 3).reshape(-1)
    b_own, b_prev, c31 = _moba_bias_tables(rel_bias)
    a = _moba_attention(_chunk_tables(cnt, start), c31, qf, pos, k, v, prevt.T, b_own, b_prev)
    return _out_proj(xs, a, vec, w_o)


def _vec(norm_g, mod3):
    return jnp.concatenate([norm_g[None, :], mod3, jnp.zeros((4, D_MODEL), F32)], axis=0)


def kernel(x, c, rel_bias, mod_w, mod_b, norm_g, ffn_w_in, ffn_w_out, gmlp_w_in, gmlp_v_norm,
           gmlp_w_s, gmlp_b_s, gmlp_w_out, moba_w_qkv, moba_w_o, final_norm):
    assert x.shape == (1, SEQ, D_MODEL)
    xs = x.reshape(SEQ, D_MODEL)
    mod = _modulation(c.reshape(D_MODEL, 1), mod_w, mod_b).reshape(DEPTH, N_SUBLAYERS, 3, D_MODEL)
    fin = final_norm.reshape(1, D_MODEL)

    for i in range(DEPTH):
        li = i // 2
        xs = _ffn(xs, _vec(norm_g[i, 0], mod[i, 0]), ffn_w_in[i, 0].astype(BF16),
                  ffn_w_out[i, 0].astype(BF16), fin, False)
        vec = _vec(norm_g[i, 1], mod[i, 1])
        if i % 2 == 0:
            bias = jnp.repeat(gmlp_b_s[li].T, GM_GROUP_DIM, axis=1)
            xs = _gmlp(xs, vec, gmlp_w_in[li].astype(BF16), gmlp_v_norm[li].reshape(1, GM_D),
                       gmlp_w_s[li].astype(BF16), bias, gmlp_w_out[li].astype(BF16))
        else:
            xs = _moba(xs, vec, moba_w_qkv[li].astype(BF16), moba_w_o[li].astype(BF16), rel_bias)
        xs = _ffn(xs, _vec(norm_g[i, 2], mod[i, 2]), ffn_w_in[i, 1].astype(BF16),
                  ffn_w_out[i, 1].astype(BF16), fin, i == DEPTH - 1)
    return xs.reshape(1, SEQ, D_MODEL)
```

```python
import functools
import math

import numpy as np
import jax
import jax.numpy as jnp
from jax import lax
from jax.experimental import pallas as pl
from jax.experimental.pallas import tpu as pltpu

F32 = jnp.float32
BF16 = jnp.bfloat16

D_MODEL = 1024
SEQ = 16384
DEPTH = 2
N_SUBLAYERS = 3
RMS_EPS = 1e-6
D_FF = 2816

GM_D = 3 * D_MODEL
GM_GROUPS = 16
GM_GROUP_DIM = GM_D // GM_GROUPS
GM_CHUNK = 128
GM_PAIR = 2 * GM_GROUP_DIM

MB_HEADS = 16
MB_HEAD_DIM = 64
MB_BLOCK = 256
MB_TOPK = 3
MB_SCALE = MB_HEAD_DIM ** -0.5
N_KBLOCKS = SEQ // MB_BLOCK

T5_NUM_BUCKETS = 32
T5_MAX_EXACT = 16
T5_MAX_DISTANCE = 128
NEG_INF = -1e30
BELOW_NEG_INF = -3e38

LANES = 128
VMEM_LIMIT = 56 * 1024 * 1024

MOD_TN = 1536
FFN_TM = 512
FFN_CHUNK = 1408
GM_TM = 256
GM_VCHUNK = 768
QKV_TM = 512
PROJ_TM = 512

ROUTE_T = 512
ROUTE_ALIGN = 8
STEP_ROWS = 1024
NSTEPS = SEQ // STEP_ROWS
BUF_DATA_ROWS = MB_TOPK * SEQ + N_KBLOCKS * ROUTE_ALIGN
BUF_ROWS = BUF_DATA_ROWS + 3 * MB_BLOCK
NULL_ROW = BUF_ROWS - ROUTE_ALIGN
ROUTE_NONE = 1e9
MAX_CHUNKS = MB_TOPK * SEQ // MB_BLOCK + N_KBLOCKS
POS_BLOCK = MB_TOPK * STEP_ROWS


def _resident(shape):
    nd = len(shape)
    return pl.BlockSpec(shape, lambda *_: (0,) * nd, pipeline_mode=pl.Buffered(1))


def _params(*sem):
    return pltpu.CompilerParams(dimension_semantics=sem, vmem_limit_bytes=VMEM_LIMIT)


def _rms(x):
    return x * lax.rsqrt(jnp.mean(x * x, axis=-1, keepdims=True) + RMS_EPS)


def _adaln(x, vec_ref):
    y = _rms(x) * vec_ref[0:1, :]
    return y * (1.0 + vec_ref[2:3, :]) + vec_ref[1:2, :]


def _gelu(x):
    return 0.5 * x * (1.0 + lax.erf(x * np.float32(math.sqrt(0.5))))


def _mod_kernel(c_ref, w_ref, b_ref, o_ref):
    c = c_ref[...]
    c_act = c * jax.nn.sigmoid(c)
    o_ref[...] = jnp.sum(c_act * w_ref[...], axis=0, keepdims=True) + b_ref[...]


def _modulation(c_col, mod_w, mod_b):
    n = mod_w.shape[-1]
    return pl.pallas_call(
        _mod_kernel,
        grid=(DEPTH, n // MOD_TN),
        in_specs=[
            pl.BlockSpec((D_MODEL, 1), lambda i, j: (0, 0)),
            pl.BlockSpec((None, D_MODEL, MOD_TN), lambda i, j: (i, 0, j)),
            pl.BlockSpec((None, 1, MOD_TN), lambda i, j: (i, 0, j)),
        ],
        out_specs=pl.BlockSpec((None, 1, MOD_TN), lambda i, j: (i, 0, j)),
        out_shape=jax.ShapeDtypeStruct((DEPTH, 1, n), F32),
        compiler_params=_params("arbitrary", "arbitrary"),
        name="modulation",
    )(c_col, mod_w, mod_b.reshape(DEPTH, 1, n))


def _ffn_kernel(x_ref, vec_ref, win_ref, wout_ref, fin_ref, o_ref, act_ref, *, final):
    x = x_ref[...]
    h = _adaln(x, vec_ref).astype(BF16)
    for c in range(D_FF // FFN_CHUNK):
        lo = c * FFN_CHUNK
        g = jnp.dot(h, win_ref[:, lo:lo + FFN_CHUNK], preferred_element_type=F32)
        u = jnp.dot(h, win_ref[:, D_FF + lo:D_FF + lo + FFN_CHUNK], preferred_element_type=F32)
        act_ref[:, lo:lo + FFN_CHUNK] = (g * jax.nn.sigmoid(g) * u).astype(BF16)
    y = jnp.dot(act_ref[...], wout_ref[...], preferred_element_type=F32)
    out = x + (0.5 * vec_ref[3:4, :]) * y
    if final:
        out = _rms(out) * fin_ref[...]
    o_ref[...] = out


def _ffn(x, vec, w_in, w_out, fin, final):
    return pl.pallas_call(
        functools.partial(_ffn_kernel, final=final),
        grid=(SEQ // FFN_TM,),
        in_specs=[
            pl.BlockSpec((FFN_TM, D_MODEL), lambda i: (i, 0)),
            _resident((8, D_MODEL)),
            _resident((D_MODEL, 2 * D_FF)),
            _resident((D_FF, D_MODEL)),
            _resident((1, D_MODEL)),
        ],
        out_specs=pl.BlockSpec((FFN_TM, D_MODEL), lambda i: (i, 0)),
        out_shape=jax.ShapeDtypeStruct((SEQ, D_MODEL), F32),
        scratch_shapes=[pltpu.VMEM((FFN_TM, D_FF), BF16)],
        compiler_params=_params("parallel"),
        name="ffn_final" if final else "ffn",
    )(x, vec, w_in, w_out, fin)


def _gmlp_kernel(x_ref, vec_ref, win_ref, vnorm_ref, ws_ref, bias_ref, wout_ref,
                 o_ref, v_ref, g_ref):
    x = x_ref[...]
    h = _adaln(x, vec_ref).astype(BF16)

    ssq = jnp.zeros((GM_TM, 1), F32)
    for c in range(GM_D // GM_VCHUNK):
        lo = c * GM_VCHUNK
        v = _gelu(jnp.dot(h, win_ref[:, GM_D + lo:GM_D + lo + GM_VCHUNK],
                          preferred_element_type=F32))
        v_ref[:, lo:lo + GM_VCHUNK] = v
        ssq = ssq + jnp.sum(v * v, axis=-1, keepdims=True)
    r = lax.rsqrt(ssq * (1.0 / GM_D) + RMS_EPS)

    row = lax.broadcasted_iota(jnp.int32, (GM_CHUNK, GM_CHUNK), 0)
    col = lax.broadcasted_iota(jnp.int32, (GM_CHUNK, GM_CHUNK), 1)
    causal = row >= col
    low_half = lax.broadcasted_iota(jnp.int32, (GM_CHUNK, LANES), 1) < (GM_GROUP_DIM - LANES)

    for p in range(GM_GROUPS // 2):
        lo = p * GM_PAIR
        vn = (v_ref[:, lo:lo + GM_PAIR] * r * vnorm_ref[:, lo:lo + GM_PAIR]).astype(BF16)
        u = _gelu(jnp.dot(h, win_ref[:, lo:lo + GM_PAIR], preferred_element_type=F32))
        w0 = jnp.where(causal, ws_ref[2 * p], jnp.zeros((), BF16))
        w1 = jnp.where(causal, ws_ref[2 * p + 1], jnp.zeros((), BF16))
        for c in range(GM_TM // GM_CHUNK):
            rows = slice(c * GM_CHUNK, (c + 1) * GM_CHUNK)
            vc = vn[rows]
            a = jnp.dot(w0, vc[:, :2 * LANES], preferred_element_type=F32)
            b = jnp.dot(w1, vc[:, LANES:], preferred_element_type=F32)
            mid = jnp.where(low_half, a[:, LANES:], b[:, :LANES])
            sv = jnp.concatenate([a[:, :LANES], mid, b[:, LANES:]], axis=1)
            sv = sv + bias_ref[:, lo:lo + GM_PAIR]
            g_ref[rows, lo:lo + GM_PAIR] = (u[rows] * sv).astype(BF16)

    y = jnp.dot(g_ref[...], wout_ref[...], preferred_element_type=F32)
    o_ref[...] = x + vec_ref[3:4, :] * y


def _gmlp(x, vec, w_in, v_norm, w_s, bias, w_out):
    return pl.pallas_call(
        _gmlp_kernel,
        grid=(SEQ // GM_TM,),
        in_specs=[
            pl.BlockSpec((GM_TM, D_MODEL), lambda i: (i, 0)),
            _resident((8, D_MODEL)),
            _resident((D_MODEL, 2 * GM_D)),
            _resident((1, GM_D)),
            _resident((GM_GROUPS, GM_CHUNK, GM_CHUNK)),
            _resident((GM_CHUNK, GM_D)),
            _resident((GM_D, D_MODEL)),
        ],
        out_specs=pl.BlockSpec((GM_TM, D_MODEL), lambda i: (i, 0)),
        out_shape=jax.ShapeDtypeStruct((SEQ, D_MODEL), F32),
        scratch_shapes=[pltpu.VMEM((GM_TM, GM_D), F32), pltpu.VMEM((GM_TM, GM_D), BF16)],
        compiler_params=_params("parallel"),
        name="gmlp",
    )(x, vec, w_in, v_norm, w_s, bias, w_out)


def _qkv_kernel(x_ref, vec_ref, w_ref, q_ref, k_ref, v_ref, selt_ref, prevt_ref, cnt_ref,
                kmean_ref):
    i = pl.program_id(0)

    @pl.when(i == 0)
    def _init():
        kmean_ref[...] = jnp.zeros_like(kmean_ref)
        cnt_ref[...] = jnp.zeros_like(cnt_ref)

    h = _adaln(x_ref[...], vec_ref).astype(BF16)
    q = jnp.dot(h, w_ref[:, :D_MODEL], preferred_element_type=F32) * MB_SCALE
    q_ref[...] = q
    k = jnp.dot(h, w_ref[:, D_MODEL:2 * D_MODEL], preferred_element_type=F32)
    k_ref[...] = k.astype(BF16)
    nb = QKV_TM // MB_BLOCK
    for b in range(nb):
        kmean_ref[pl.ds(nb * i + b, 1), :] = jnp.mean(
            k[b * MB_BLOCK:(b + 1) * MB_BLOCK], axis=0, keepdims=True)
    v = jnp.dot(h, w_ref[:, 2 * D_MODEL:], preferred_element_type=F32)
    v_ref[...] = v.astype(BF16)

    lane = lax.broadcasted_iota(jnp.int32, (1, LANES), 1)
    blk = lax.broadcasted_iota(jnp.int32, (N_KBLOCKS, MB_BLOCK), 0)
    contract_last = (((1,), (1,)), ((), ()))
    for b in range(nb):
        j = nb * i + b
        cols = slice(b * MB_BLOCK, (b + 1) * MB_BLOCK)
        for pair in range(MB_HEADS // 2):
            km = kmean_ref[:, pair * LANES:(pair + 1) * LANES]
            qp = q[cols, pair * LANES:(pair + 1) * LANES]
            for hh in range(2):
                head = 2 * pair + hh
                km_h = jnp.where((lane // MB_HEAD_DIM) == hh, km, 0.0)
                g = lax.dot_general(km_h, qp, contract_last, preferred_element_type=F32)
                g = jnp.where(blk < j, g, NEG_INF)
                sel = jnp.zeros((N_KBLOCKS, MB_BLOCK), F32)
                for r in range(MB_TOPK):
                    m = jnp.max(g, axis=0, keepdims=True)
                    first = jnp.min(jnp.where(g == m, blk, N_KBLOCKS), axis=0, keepdims=True)
                    hit = blk == first
                    sel = jnp.where(jnp.logical_and(hit, r < j), 1.0, sel)
                    g = jnp.where(hit, BELOW_NEG_INF, g)
                prevt_ref[head:head + 1, cols] = jnp.max(
                    jnp.where(blk == j - 1, sel, 0.0), axis=0, keepdims=True)
                routed = jnp.where(blk < j - 1, sel, 0.0)
                selt_ref[head * N_KBLOCKS:(head + 1) * N_KBLOCKS, cols] = routed.astype(BF16)
                rows = slice(head * N_KBLOCKS, (head + 1) * N_KBLOCKS)
                cnt_ref[rows, :] = cnt_ref[rows, :] + jnp.sum(routed, axis=1, keepdims=True)


def _qkv(x, vec, w_qkv):
    row = pl.BlockSpec((QKV_TM, D_MODEL), lambda i: (i, 0))
    n_items = MB_HEADS * N_KBLOCKS
    return pl.pallas_call(
        _qkv_kernel,
        grid=(SEQ // QKV_TM,),
        in_specs=[row, _resident((8, D_MODEL)), _resident((D_MODEL, 3 * D_MODEL))],
        out_specs=[row, row, row,
                   pl.BlockSpec((n_items, QKV_TM), lambda i: (0, i)),
                   pl.BlockSpec((MB_HEADS, QKV_TM), lambda i: (0, i)),
                   pl.BlockSpec((n_items, LANES), lambda i: (0, 0))],
        out_shape=[jax.ShapeDtypeStruct((SEQ, D_MODEL), F32),
                   jax.ShapeDtypeStruct((SEQ, D_MODEL), BF16),
                   jax.ShapeDtypeStruct((SEQ, D_MODEL), BF16),
                   jax.ShapeDtypeStruct((n_items, SEQ), BF16),
                   jax.ShapeDtypeStruct((MB_HEADS, SEQ), F32),
                   jax.ShapeDtypeStruct((n_items, LANES), F32)],
        scratch_shapes=[pltpu.VMEM((N_KBLOCKS, D_MODEL), F32)],
        compiler_params=_params("arbitrary"),
        name="moba_qkv_gate",
    )(x, vec, w_qkv)


def _route_kernel(selt_ref, base_ref, pos_ref, carry_ref):
    i = pl.program_id(0)

    @pl.when(i == 0)
    def _init():
        carry_ref[...] = jnp.zeros_like(carry_ref)

    sel = selt_ref[...]
    before = (lax.broadcasted_iota(jnp.int32, (ROUTE_T, ROUTE_T), 0)
              < lax.broadcasted_iota(jnp.int32, (ROUTE_T, ROUTE_T), 1))
    rank = jnp.dot(sel, jnp.where(before, 1.0, 0.0).astype(BF16), preferred_element_type=F32)
    offset = base_ref[...] + carry_ref[...]
    rank = rank + jnp.concatenate([offset] * (ROUTE_T // LANES), axis=1)
    carry_ref[...] = carry_ref[...] + jnp.sum(sel.astype(F32), axis=1, keepdims=True)
    val = jnp.where(sel > 0, rank, ROUTE_NONE)
    for head in range(MB_HEADS):
        vh = val[head * N_KBLOCKS:(head + 1) * N_KBLOCKS]
        for r in range(MB_TOPK):
            m = jnp.min(vh, axis=0, keepdims=True)
            slot = head * MB_TOPK + r
            pos_ref[slot:slot + 1, :] = jnp.where(m < ROUTE_NONE, m, float(NULL_ROW)).astype(jnp.int32)
            vh = jnp.where(vh == m, ROUTE_NONE, vh)


def _route(selt, base):
    n_items = MB_HEADS * N_KBLOCKS
    return pl.pallas_call(
        _route_kernel,
        grid=(SEQ // ROUTE_T,),
        in_specs=[pl.BlockSpec((n_items, ROUTE_T), lambda i: (0, i)),
                  pl.BlockSpec((n_items, LANES), lambda i: (0, 0))],
        out_specs=pl.BlockSpec((MB_HEADS * MB_TOPK, ROUTE_T), lambda i: (0, i)),
        out_shape=jax.ShapeDtypeStruct((MB_HEADS * MB_TOPK, SEQ), jnp.int32),
        scratch_shapes=[pltpu.VMEM((n_items, LANES), F32)],
        compiler_params=_params("arbitrary"),
        name="moba_route",
    )(selt, base)


def _moba_kernel(choff_ref, chblk_ref, chvalid_ref, nch_ref, c31_ref,
                 qf_ref, pos_hbm, k_ref, v_ref, prev_ref, bown_ref, bprev_ref,
                 o_ref, buf_ref, stage_ref, hold_ref, s0_ref, s1_ref, p0_ref, p1_ref, m0_ref, m1_ref,
                 pos_even_ref, pos_odd_ref, pos_sem):
    head = pl.program_id(0)
    t = pl.program_id(1)
    hh = head % 2
    lane = lax.broadcasted_iota(jnp.int32, (1, LANES), 1)
    in_head = (lane // MB_HEAD_DIM) == hh
    stat0 = (1 - hh) * MB_HEAD_DIM
    is_m_lane = lane == stat0
    is_l_lane = lane == stat0 + 1
    contract_last = (((1,), (1,)), ((), ()))
    ones_bf = jnp.ones((), BF16)

    def queries(rows):
        return jnp.where(in_head, rows, 0.0).astype(BF16)

    def values(rows):
        return jnp.where(in_head, rows, ones_bf)

    @pl.when(t == 0)
    def _init():
        def zero(c, _):
            buf_ref[pl.ds(pl.multiple_of(c * MB_BLOCK, MB_BLOCK), MB_BLOCK), :] = (
                jnp.zeros((MB_BLOCK, LANES), F32))
            return 0
        lax.fori_loop(0, BUF_ROWS // MB_BLOCK, zero, 0)

    def dispatch(pos_ref):
        for a in range(STEP_ROWS):
            row = qf_ref[a:a + 1, :]
            for r in range(MB_TOPK):
                buf_ref[pl.ds(pos_ref[r * STEP_ROWS + a], 1), :] = row

    @pl.when(t == NSTEPS)
    def _routed():
        buf_ref[NULL_ROW:NULL_ROW + ROUTE_ALIGN, :] = jnp.where(
            is_m_lane, NEG_INF, jnp.zeros((ROUTE_ALIGN, LANES), F32))
        n = nch_ref[head]
        far_bias = c31_ref[head]
        row_id = lax.broadcasted_iota(jnp.int32, (MB_BLOCK, 1), 0)

        def meta(i):
            idx = head * MAX_CHUNKS + jnp.minimum(i, n - 1)
            off = pl.multiple_of(choff_ref[idx], ROUTE_ALIGN)
            blk = pl.multiple_of(chblk_ref[idx] * MB_BLOCK, MB_BLOCK)
            return off, blk, jnp.where(i < n, chvalid_ref[idx], 0)

        def scores(rows, blk):
            return lax.dot_general(queries(rows), k_ref[pl.ds(blk, MB_BLOCK), :], contract_last,
                                   preferred_element_type=F32)

        def probs(s_ref, p_ref, m_ref):
            s = s_ref[...]
            m = jnp.max(s, axis=1, keepdims=True)
            p_ref[...] = jnp.exp(s - m).astype(BF16)
            m_ref[...] = jnp.broadcast_to(m, (MB_BLOCK, LANES))

        def finish(off, blk, valid, old, p_ref, m_ref):
            acc = jnp.dot(p_ref[...], values(v_ref[pl.ds(blk, MB_BLOCK), :]), preferred_element_type=F32)
            new = jnp.where(is_m_lane, m_ref[...] + far_bias, acc)
            buf_ref[pl.ds(off, MB_BLOCK), :] = jnp.where(row_id < valid, new, old)

        @pl.when(n > 0)
        def _chunks():
            off, blk, _ = meta(0)
            s0_ref[...] = scores(buf_ref[pl.ds(off, MB_BLOCK), :], blk)
            off, blk, _ = meta(1)
            s1_ref[...] = scores(buf_ref[pl.ds(off, MB_BLOCK), :], blk)
            probs(s0_ref, p0_ref, m0_ref)

            def two_chunks(c, _):
                i = 2 * c
                off_a, blk_a, _ = meta(i + 2)
                off_b, blk_b, _ = meta(i + 3)
                off_c, blk_c, valid_c = meta(i)
                off_d, blk_d, valid_d = meta(i + 1)
                rows_a = buf_ref[pl.ds(off_a, MB_BLOCK), :]
                rows_b = buf_ref[pl.ds(off_b, MB_BLOCK), :]
                old_c = buf_ref[pl.ds(off_c, MB_BLOCK), :]
                old_d = buf_ref[pl.ds(off_d, MB_BLOCK), :]

                s_a = scores(rows_a, blk_a)
                probs(s1_ref, p1_ref, m1_ref)
                finish(off_c, blk_c, valid_c, old_c, p0_ref, m0_ref)
                s0_ref[...] = s_a

                s_b = scores(rows_b, blk_b)
                probs(s0_ref, p0_ref, m0_ref)
                finish(off_d, blk_d, valid_d, old_d, p1_ref, m1_ref)
                s1_ref[...] = s_b
                return 0
            lax.fori_loop(0, (n + 1) // 2, two_chunks, 0)

    def combine(pos_ref):
        step = t - (NSTEPS + 1)
        lane16 = lax.broadcasted_iota(jnp.int32, (1, MB_HEADS), 1)

        n_sub = STEP_ROWS // MB_BLOCK
        windows, all_scores = [], []
        for jj in range(n_sub):
            j = step * n_sub + jj
            rs = slice(jj * MB_BLOCK, (jj + 1) * MB_BLOCK)
            win = pl.ds(pl.multiple_of(jnp.maximum(j - 1, 0) * MB_BLOCK, MB_BLOCK), 2 * MB_BLOCK)
            windows.append(win)
            all_scores.append(lax.dot_general(queries(qf_ref[rs, :]), k_ref[win, :], contract_last,
                                              preferred_element_type=F32))

        for jj in range(n_sub):
            j = step * n_sub + jj
            rs = slice(jj * MB_BLOCK, (jj + 1) * MB_BLOCK)
            s = all_scores[jj]
            picked = jnp.sum(jnp.where(lane16 == head, prev_ref[rs, :], 0.0),
                             axis=1, keepdims=True) > 0.0
            bias_lo, bias_hi = bprev_ref[...], bown_ref[...]
            if jj == 0:
                first = step == 0
                bias_lo = jnp.where(first, bown_ref[...], bias_lo)
                bias_hi = jnp.where(first, NEG_INF, bias_hi)
                picked = jnp.logical_or(picked, first)
            s_lo = jnp.where(picked, s[:, :MB_BLOCK] + bias_lo, NEG_INF)
            s_hi = s[:, MB_BLOCK:] + bias_hi
            m_d = jnp.maximum(jnp.max(s_lo, axis=1, keepdims=True), jnp.max(s_hi, axis=1, keepdims=True))
            p = jnp.concatenate([jnp.exp(s_lo - m_d), jnp.exp(s_hi - m_d)], axis=1).astype(BF16)
            acc = jnp.dot(p, values(v_ref[windows[jj], :]), preferred_element_type=F32)

            for a in range(jj * MB_BLOCK, (jj + 1) * MB_BLOCK):
                for r in range(MB_TOPK):
                    stage_ref[r, a:a + 1, :] = buf_ref[pl.ds(pos_ref[r * STEP_ROWS + a], 1), :]

            parts = [stage_ref[r, rs, :] for r in range(MB_TOPK)]
            ms = [jnp.sum(jnp.where(is_m_lane, part, 0.0), axis=1, keepdims=True) for part in parts]
            m_all = m_d
            for m_r in ms:
                m_all = jnp.maximum(m_all, m_r)
            total = jnp.exp(m_d - m_all) * acc
            for m_r, part in zip(ms, parts):
                total = total + jnp.exp(m_r - m_all) * part
            den = jnp.sum(jnp.where(is_l_lane, total, 0.0), axis=1, keepdims=True)
            out = (total / den).astype(BF16)
            hold_rows = pl.ds(pl.multiple_of(j * MB_BLOCK, MB_BLOCK), MB_BLOCK)

            @pl.when(hh == 0)
            def _keep():
                hold_ref[hold_rows, :] = out

            @pl.when(hh == 1)
            def _emit():
                o_ref[rs, :] = jnp.where(lane < MB_HEAD_DIM, hold_ref[hold_rows, :], out)

    steps_per_head = 2 * NSTEPS
    use = head * steps_per_head + jnp.where(t < NSTEPS, t, t - 1)
    pos_bufs = (pos_even_ref, pos_odd_ref)

    def pos_copy(u, parity):
        block = (u // steps_per_head) * NSTEPS + u % NSTEPS
        src = pos_hbm.at[pl.ds(pl.multiple_of(block * POS_BLOCK, POS_BLOCK), POS_BLOCK)]
        return pltpu.make_async_copy(src, pos_bufs[parity], pos_sem.at[parity])

    for parity in range(2):
        @pl.when(jnp.logical_and(t != NSTEPS, use % 2 == parity))
        def _positions_step(parity=parity):
            @pl.when(use == 0)
            def _first():
                pos_copy(use, parity).start()

            @pl.when(use + 1 < MB_HEADS * steps_per_head)
            def _prefetch():
                pos_copy(use + 1, 1 - parity).start()

            pos_copy(use, parity).wait()

            @pl.when(t < NSTEPS)
            def _dispatch():
                dispatch(pos_bufs[parity])

            @pl.when(t > NSTEPS)
            def _combine():
                combine(pos_bufs[parity])


def _moba_attention(chunk_meta, c31, qf, pos, k, v, prevc, bias_own, bias_prev):
    def seq_step(t):
        return jnp.where(t < NSTEPS, t, jnp.maximum(t - (NSTEPS + 1), 0) + jnp.where(t == NSTEPS, NSTEPS - 1, 0))

    def combine_step(t):
        return jnp.maximum(t - (NSTEPS + 1), 0)

    pair_resident = pl.BlockSpec((SEQ, LANES), lambda h, t, *_: (0, h // 2), pipeline_mode=pl.Buffered(1))
    bias = pl.BlockSpec((None, MB_BLOCK, MB_BLOCK), lambda h, t, *_: (h, 0, 0))
    chunk_f32 = pltpu.VMEM((MB_BLOCK, MB_BLOCK), F32)
    chunk_bf16 = pltpu.VMEM((MB_BLOCK, MB_BLOCK), BF16)
    chunk_max = pltpu.VMEM((MB_BLOCK, LANES), F32)
    return pl.pallas_call(
        _moba_kernel,
        grid_spec=pltpu.PrefetchScalarGridSpec(
            num_scalar_prefetch=5,
            grid=(MB_HEADS, 2 * NSTEPS + 1),
            in_specs=[
                pl.BlockSpec((STEP_ROWS, LANES), lambda h, t, *_: (seq_step(t), h // 2)),
                pl.BlockSpec(memory_space=pl.ANY),
                pair_resident, pair_resident,
                pl.BlockSpec((STEP_ROWS, MB_HEADS), lambda h, t, *_: (combine_step(t), 0)),
                bias, bias,
            ],
            out_specs=pl.BlockSpec(
                (STEP_ROWS, LANES),
                lambda h, t, *_: (jnp.where(h % 2 == 1, combine_step(t), 0), h // 2)),
            scratch_shapes=[pltpu.VMEM((BUF_ROWS, LANES), F32),
                            pltpu.VMEM((MB_TOPK, STEP_ROWS, LANES), F32),
                            pltpu.VMEM((SEQ, LANES), BF16),
                            chunk_f32, chunk_f32, chunk_bf16, chunk_bf16, chunk_max, chunk_max,
                            pltpu.SMEM((POS_BLOCK,), jnp.int32), pltpu.SMEM((POS_BLOCK,), jnp.int32),
                            pltpu.SemaphoreType.DMA((2,))],
        ),
        out_shape=jax.ShapeDtypeStruct((SEQ, D_MODEL), BF16),
        compiler_params=_params("arbitrary", "arbitrary"),
        name="moba_attention",
    )(*chunk_meta, c31, qf, pos, k, v, prevc, bias_own, bias_prev)


def _proj_kernel(x_ref, a_ref, vec_ref, w_ref, o_ref):
    y = jnp.dot(a_ref[...], w_ref[...], preferred_element_type=F32)
    o_ref[...] = x_ref[...] + vec_ref[3:4, :] * y


def _out_proj(x, a, vec, w_o):
    row = pl.BlockSpec((PROJ_TM, D_MODEL), lambda i: (i, 0))
    return pl.pallas_call(
        _proj_kernel,
        grid=(SEQ // PROJ_TM,),
        in_specs=[row, row, _resident((8, D_MODEL)), _resident((D_MODEL, D_MODEL))],
        out_specs=row,
        out_shape=jax.ShapeDtypeStruct((SEQ, D_MODEL), F32),
        compiler_params=_params("parallel"),
        name="moba_out_proj",
    )(x, a, vec, w_o)


def _t5_bucket_np(rel):
    n = np.maximum(rel, 0)
    nf = np.maximum(n, T5_MAX_EXACT).astype(np.float32)
    large = T5_MAX_EXACT + (np.log(nf / np.float32(T5_MAX_EXACT))
                            / np.float32(math.log(T5_MAX_DISTANCE / T5_MAX_EXACT))
                            * np.float32(T5_NUM_BUCKETS - T5_MAX_EXACT)).astype(np.int32)
    large = np.minimum(large, T5_NUM_BUCKETS - 1)
    return np.where(n < T5_MAX_EXACT, n, large)


def _moba_bias_tables(rel_bias):
    a = np.arange(MB_BLOCK)
    rel_own = a[:, None] - a[None, :]
    table = rel_bias.astype(F32).T

    def expand(rel):
        bucket = jnp.asarray(_t5_bucket_np(rel).reshape(-1).astype(np.int32))
        onehot = (bucket[None, :] == jnp.arange(T5_NUM_BUCKETS)[:, None]).astype(F32)
        return jnp.dot(table, onehot, precision=lax.Precision.HIGHEST).reshape(
            MB_HEADS, MB_BLOCK, MB_BLOCK)

    b_own = jnp.where(jnp.asarray(rel_own >= 0)[None], expand(rel_own), NEG_INF)
    return b_own, expand(rel_own + MB_BLOCK), table[:, T5_NUM_BUCKETS - 1]


def _chunk_tables(cnt, start):
    nch = (cnt + MB_BLOCK - 1) // MB_BLOCK
    cum = jnp.cumsum(nch, axis=1)
    i = jnp.arange(MAX_CHUNKS)
    blk = jnp.sum(cum[:, None, :] <= i[None, :, None], axis=2)
    blk = jnp.minimum(blk, N_KBLOCKS - 1)
    onehot = blk[:, :, None] == jnp.arange(N_KBLOCKS)[None, None, :]

    def pick(x):
        return jnp.sum(jnp.where(onehot, x[:, None, :], 0), axis=2)

    def flat(x):
        return x.reshape(-1).astype(jnp.int32)
    within = i[None, :] - pick(cum - nch)
    off = pick(start) + within * MB_BLOCK
    valid = jnp.clip(pick(cnt) - within * MB_BLOCK, 0, MB_BLOCK)
    return flat(off), flat(blk), flat(valid), flat(cum[:, -1])


def _moba(xs, vec, w_qkv, w_o, rel_bias):
    qf, k, v, selt, prevt, cnt = _qkv(xs, vec, w_qkv)
    cnt = cnt[:, 0].astype(jnp.int32).reshape(MB_HEADS, N_KBLOCKS)
    padded = (cnt + ROUTE_ALIGN - 1) // ROUTE_ALIGN * ROUTE_ALIGN
    start = jnp.cumsum(padded, axis=1) - padded
    base = jnp.broadcast_to(start.reshape(-1, 1).astype(F32), (MB_HEADS * N_KBLOCKS, LANES))
    pos = _route(selt, base).reshape(MB_HEADS, MB_TOPK, NSTEPS, STEP_ROWS)
    pos = pos.transpose(0, 2, 1, 3).reshape(-1)
    b_own, b_prev, c31 = _moba_bias_tables(rel_bias)
    a = _moba_attention(_chunk_tables(cnt, start), c31, qf, pos, k, v, prevt.T, b_own, b_prev)
    return _out_proj(xs, a, vec, w_o)


def _vec(norm_g, mod3):
    return jnp.concatenate([norm_g[None, :], mod3, jnp.zeros((4, D_MODEL), F32)], axis=0)


def kernel(x, c, rel_bias, mod_w, mod_b, norm_g, ffn_w_in, ffn_w_out, gmlp_w_in, gmlp_v_norm,
           gmlp_w_s, gmlp_b_s, gmlp_w_out, moba_w_qkv, moba_w_o, final_norm):
    assert x.shape == (1, SEQ, D_MODEL)
    xs = x.reshape(SEQ, D_MODEL)
    mod = _modulation(c.reshape(D_MODEL, 1), mod_w, mod_b).reshape(DEPTH, N_SUBLAYERS, 3, D_MODEL)
    fin = final_norm.reshape(1, D_MODEL)

    for i in range(DEPTH):
        li = i // 2
        xs = _ffn(xs, _vec(norm_g[i, 0], mod[i, 0]), ffn_w_in[i, 0].astype(BF16),
                  ffn_w_out[i, 0].astype(BF16), fin, False)
        vec = _vec(norm_g[i, 1], mod[i, 1])
        if i % 2 == 0:
            bias = jnp.repeat(gmlp_b_s[li].T, GM_GROUP_DIM, axis=1)
            xs = _gmlp(xs, vec, gmlp_w_in[li].astype(BF16), gmlp_v_norm[li].reshape(1, GM_D),
                       gmlp_w_s[li].astype(BF16), bias, gmlp_w_out[li].astype(BF16))
        else:
            xs = _moba(xs, vec, moba_w_qkv[li].astype(BF16), moba_w_o[li].astype(BF16), rel_bias)
        xs = _ffn(xs, _vec(norm_g[i, 2], mod[i, 2]), ffn_w_in[i, 1].astype(BF16),
                  ffn_w_out[i, 1].astype(BF16), fin, i == DEPTH - 1)
    return xs.reshape(1, SEQ, D_MODEL)
```

```python
import functools
import math

import numpy as np
import jax
import jax.numpy as jnp
from jax import lax
from jax.experimental import pallas as pl
from jax.experimental.pallas import tpu as pltpu

F32 = jnp.float32
BF16 = jnp.bfloat16

D_MODEL = 1024
SEQ = 16384
DEPTH = 2
N_SUBLAYERS = 3
RMS_EPS = 1e-6
D_FF = 2816

GM_D = 3 * D_MODEL
GM_GROUPS = 16
GM_GROUP_DIM = GM_D // GM_GROUPS
GM_CHUNK = 128
GM_PAIR = 2 * GM_GROUP_DIM

MB_HEADS = 16
MB_HEAD_DIM = 64
MB_BLOCK = 256
MB_TOPK = 3
MB_SCALE = MB_HEAD_DIM ** -0.5
N_KBLOCKS = SEQ // MB_BLOCK

T5_NUM_BUCKETS = 32
T5_MAX_EXACT = 16
T5_MAX_DISTANCE = 128
NEG_INF = -1e30
BELOW_NEG_INF = -3e38

LANES = 128
VMEM_LIMIT = 56 * 1024 * 1024

MOD_TN = 1536
FFN_TM = 512
FFN_CHUNK = 1408
GM_TM = 256
GM_VCHUNK = 768
QKV_TM = 512
PROJ_TM = 512

ROUTE_T = 512
ROUTE_ALIGN = 8
STEP_ROWS = 1024
NSTEPS = SEQ // STEP_ROWS
BUF_DATA_ROWS = MB_TOPK * SEQ + N_KBLOCKS * ROUTE_ALIGN
BUF_ROWS = BUF_DATA_ROWS + 3 * MB_BLOCK
NULL_ROW = BUF_ROWS - ROUTE_ALIGN
ROUTE_NONE = 1e9
MAX_CHUNKS = MB_TOPK * SEQ // MB_BLOCK + N_KBLOCKS
POS_BLOCK = MB_TOPK * STEP_ROWS
CHUNKS_PER_BODY = 8


def _resident(shape):
    nd = len(shape)
    return pl.BlockSpec(shape, lambda *_: (0,) * nd, pipeline_mode=pl.Buffered(1))


def _params(*sem):
    return pltpu.CompilerParams(dimension_semantics=sem, vmem_limit_bytes=VMEM_LIMIT)


def _rms(x):
    return x * lax.rsqrt(jnp.mean(x * x, axis=-1, keepdims=True) + RMS_EPS)


def _adaln(x, vec_ref):
    y = _rms(x) * vec_ref[0:1, :]
    return y * (1.0 + vec_ref[2:3, :]) + vec_ref[1:2, :]


def _gelu(x):
    return 0.5 * x * (1.0 + lax.erf(x * np.float32(math.sqrt(0.5))))


def _mod_kernel(c_ref, w_ref, b_ref, o_ref):
    c = c_ref[...]
    c_act = c * jax.nn.sigmoid(c)
    o_ref[...] = jnp.sum(c_act * w_ref[...], axis=0, keepdims=True) + b_ref[...]


def _modulation(c_col, mod_w, mod_b):
    n = mod_w.shape[-1]
    return pl.pallas_call(
        _mod_kernel,
        grid=(DEPTH, n // MOD_TN),
        in_specs=[
            pl.BlockSpec((D_MODEL, 1), lambda i, j: (0, 0)),
            pl.BlockSpec((None, D_MODEL, MOD_TN), lambda i, j: (i, 0, j)),
            pl.BlockSpec((None, 1, MOD_TN), lambda i, j: (i, 0, j)),
        ],
        out_specs=pl.BlockSpec((None, 1, MOD_TN), lambda i, j: (i, 0, j)),
        out_shape=jax.ShapeDtypeStruct((DEPTH, 1, n), F32),
        compiler_params=_params("arbitrary", "arbitrary"),
        name="modulation",
    )(c_col, mod_w, mod_b.reshape(DEPTH, 1, n))


def _ffn_kernel(x_ref, vec_ref, win_ref, wout_ref, fin_ref, o_ref, act_ref, *, final):
    x = x_ref[...]
    h = _adaln(x, vec_ref).astype(BF16)
    for c in range(D_FF // FFN_CHUNK):
        lo = c * FFN_CHUNK
        g = jnp.dot(h, win_ref[:, lo:lo + FFN_CHUNK], preferred_element_type=F32)
        u = jnp.dot(h, win_ref[:, D_FF + lo:D_FF + lo + FFN_CHUNK], preferred_element_type=F32)
        act_ref[:, lo:lo + FFN_CHUNK] = (g * jax.nn.sigmoid(g) * u).astype(BF16)
    y = jnp.dot(act_ref[...], wout_ref[...], preferred_element_type=F32)
    out = x + (0.5 * vec_ref[3:4, :]) * y
    if final:
        out = _rms(out) * fin_ref[...]
    o_ref[...] = out


def _ffn(x, vec, w_in, w_out, fin, final):
    return pl.pallas_call(
        functools.partial(_ffn_kernel, final=final),
        grid=(SEQ // FFN_TM,),
        in_specs=[
            pl.BlockSpec((FFN_TM, D_MODEL), lambda i: (i, 0)),
            _resident((8, D_MODEL)),
            _resident((D_MODEL, 2 * D_FF)),
            _resident((D_FF, D_MODEL)),
            _resident((1, D_MODEL)),
        ],
        out_specs=pl.BlockSpec((FFN_TM, D_MODEL), lambda i: (i, 0)),
        out_shape=jax.ShapeDtypeStruct((SEQ, D_MODEL), F32),
        scratch_shapes=[pltpu.VMEM((FFN_TM, D_FF), BF16)],
        compiler_params=_params("parallel"),
        name="ffn_final" if final else "ffn",
    )(x, vec, w_in, w_out, fin)


def _gmlp_kernel(x_ref, vec_ref, win_ref, vnorm_ref, ws_ref, bias_ref, wout_ref,
                 o_ref, v_ref, g_ref):
    x = x_ref[...]
    h = _adaln(x, vec_ref).astype(BF16)

    ssq = jnp.zeros((GM_TM, 1), F32)
    for c in range(GM_D // GM_VCHUNK):
        lo = c * GM_VCHUNK
        v = _gelu(jnp.dot(h, win_ref[:, GM_D + lo:GM_D + lo + GM_VCHUNK],
                          preferred_element_type=F32))
        v_ref[:, lo:lo + GM_VCHUNK] = v
        ssq = ssq + jnp.sum(v * v, axis=-1, keepdims=True)
    r = lax.rsqrt(ssq * (1.0 / GM_D) + RMS_EPS)

    row = lax.broadcasted_iota(jnp.int32, (GM_CHUNK, GM_CHUNK), 0)
    col = lax.broadcasted_iota(jnp.int32, (GM_CHUNK, GM_CHUNK), 1)
    causal = row >= col
    low_half = lax.broadcasted_iota(jnp.int32, (GM_CHUNK, LANES), 1) < (GM_GROUP_DIM - LANES)

    for p in range(GM_GROUPS // 2):
        lo = p * GM_PAIR
        vn = (v_ref[:, lo:lo + GM_PAIR] * r * vnorm_ref[:, lo:lo + GM_PAIR]).astype(BF16)
        u = _gelu(jnp.dot(h, win_ref[:, lo:lo + GM_PAIR], preferred_element_type=F32))
        w0 = jnp.where(causal, ws_ref[2 * p], jnp.zeros((), BF16))
        w1 = jnp.where(causal, ws_ref[2 * p + 1], jnp.zeros((), BF16))
        for c in range(GM_TM // GM_CHUNK):
            rows = slice(c * GM_CHUNK, (c + 1) * GM_CHUNK)
            vc = vn[rows]
            a = jnp.dot(w0, vc[:, :2 * LANES], preferred_element_type=F32)
            b = jnp.dot(w1, vc[:, LANES:], preferred_element_type=F32)
            mid = jnp.where(low_half, a[:, LANES:], b[:, :LANES])
            sv = jnp.concatenate([a[:, :LANES], mid, b[:, LANES:]], axis=1)
            sv = sv + bias_ref[:, lo:lo + GM_PAIR]
            g_ref[rows, lo:lo + GM_PAIR] = (u[rows] * sv).astype(BF16)

    y = jnp.dot(g_ref[...], wout_ref[...], preferred_element_type=F32)
    o_ref[...] = x + vec_ref[3:4, :] * y


def _gmlp(x, vec, w_in, v_norm, w_s, bias, w_out):
    return pl.pallas_call(
        _gmlp_kernel,
        grid=(SEQ // GM_TM,),
        in_specs=[
            pl.BlockSpec((GM_TM, D_MODEL), lambda i: (i, 0)),
            _resident((8, D_MODEL)),
            _resident((D_MODEL, 2 * GM_D)),
            _resident((1, GM_D)),
            _resident((GM_GROUPS, GM_CHUNK, GM_CHUNK)),
            _resident((GM_CHUNK, GM_D)),
            _resident((GM_D, D_MODEL)),
        ],
        out_specs=pl.BlockSpec((GM_TM, D_MODEL), lambda i: (i, 0)),
        out_shape=jax.ShapeDtypeStruct((SEQ, D_MODEL), F32),
        scratch_shapes=[pltpu.VMEM((GM_TM, GM_D), F32), pltpu.VMEM((GM_TM, GM_D), BF16)],
        compiler_params=_params("parallel"),
        name="gmlp",
    )(x, vec, w_in, v_norm, w_s, bias, w_out)


def _qkv_kernel(x_ref, vec_ref, w_ref, q_ref, k_ref, v_ref, selt_ref, prevt_ref, cnt_ref,
                kmean_ref):
    i = pl.program_id(0)

    @pl.when(i == 0)
    def _init():
        kmean_ref[...] = jnp.zeros_like(kmean_ref)
        cnt_ref[...] = jnp.zeros_like(cnt_ref)

    h = _adaln(x_ref[...], vec_ref).astype(BF16)
    q = jnp.dot(h, w_ref[:, :D_MODEL], preferred_element_type=F32) * MB_SCALE
    q_ref[...] = q
    k = jnp.dot(h, w_ref[:, D_MODEL:2 * D_MODEL], preferred_element_type=F32)
    k_ref[...] = k.astype(BF16)
    nb = QKV_TM // MB_BLOCK
    for b in range(nb):
        kmean_ref[pl.ds(nb * i + b, 1), :] = jnp.mean(
            k[b * MB_BLOCK:(b + 1) * MB_BLOCK], axis=0, keepdims=True)
    v = jnp.dot(h, w_ref[:, 2 * D_MODEL:], preferred_element_type=F32)
    v_ref[...] = v.astype(BF16)

    lane = lax.broadcasted_iota(jnp.int32, (1, LANES), 1)
    blk = lax.broadcasted_iota(jnp.int32, (N_KBLOCKS, MB_BLOCK), 0)
    contract_last = (((1,), (1,)), ((), ()))
    for b in range(nb):
        j = nb * i + b
        cols = slice(b * MB_BLOCK, (b + 1) * MB_BLOCK)
        for pair in range(MB_HEADS // 2):
            km = kmean_ref[:, pair * LANES:(pair + 1) * LANES]
            qp = q[cols, pair * LANES:(pair + 1) * LANES]
            for hh in range(2):
                head = 2 * pair + hh
                km_h = jnp.where((lane // MB_HEAD_DIM) == hh, km, 0.0)
                g = lax.dot_general(km_h, qp, contract_last, preferred_element_type=F32)
                g = jnp.where(blk < j, g, NEG_INF)
                sel = jnp.zeros((N_KBLOCKS, MB_BLOCK), F32)
                for r in range(MB_TOPK):
                    m = jnp.max(g, axis=0, keepdims=True)
                    first = jnp.min(jnp.where(g == m, blk, N_KBLOCKS), axis=0, keepdims=True)
                    hit = blk == first
                    sel = jnp.where(jnp.logical_and(hit, r < j), 1.0, sel)
                    g = jnp.where(hit, BELOW_NEG_INF, g)
                prevt_ref[head:head + 1, cols] = jnp.max(
                    jnp.where(blk == j - 1, sel, 0.0), axis=0, keepdims=True)
                routed = jnp.where(blk < j - 1, sel, 0.0)
                selt_ref[head * N_KBLOCKS:(head + 1) * N_KBLOCKS, cols] = routed.astype(BF16)
                rows = slice(head * N_KBLOCKS, (head + 1) * N_KBLOCKS)
                cnt_ref[rows, :] = cnt_ref[rows, :] + jnp.sum(routed, axis=1, keepdims=True)


def _qkv(x, vec, w_qkv):
    row = pl.BlockSpec((QKV_TM, D_MODEL), lambda i: (i, 0))
    n_items = MB_HEADS * N_KBLOCKS
    return pl.pallas_call(
        _qkv_kernel,
        grid=(SEQ // QKV_TM,),
        in_specs=[row, _resident((8, D_MODEL)), _resident((D_MODEL, 3 * D_MODEL))],
        out_specs=[row, row, row,
                   pl.BlockSpec((n_items, QKV_TM), lambda i: (0, i)),
                   pl.BlockSpec((MB_HEADS, QKV_TM), lambda i: (0, i)),
                   pl.BlockSpec((n_items, LANES), lambda i: (0, 0))],
        out_shape=[jax.ShapeDtypeStruct((SEQ, D_MODEL), F32),
                   jax.ShapeDtypeStruct((SEQ, D_MODEL), BF16),
                   jax.ShapeDtypeStruct((SEQ, D_MODEL), BF16),
                   jax.ShapeDtypeStruct((n_items, SEQ), BF16),
                   jax.ShapeDtypeStruct((MB_HEADS, SEQ), F32),
                   jax.ShapeDtypeStruct((n_items, LANES), F32)],
        scratch_shapes=[pltpu.VMEM((N_KBLOCKS, D_MODEL), F32)],
        compiler_params=_params("arbitrary"),
        name="moba_qkv_gate",
    )(x, vec, w_qkv)


def _route_kernel(selt_ref, base_ref, pos_ref, carry_ref):
    i = pl.program_id(0)

    @pl.when(i == 0)
    def _init():
        carry_ref[...] = jnp.zeros_like(carry_ref)

    sel = selt_ref[...]
    before = (lax.broadcasted_iota(jnp.int32, (ROUTE_T, ROUTE_T), 0)
              < lax.broadcasted_iota(jnp.int32, (ROUTE_T, ROUTE_T), 1))
    rank = jnp.dot(sel, jnp.where(before, 1.0, 0.0).astype(BF16), preferred_element_type=F32)
    offset = base_ref[...] + carry_ref[...]
    rank = rank + jnp.concatenate([offset] * (ROUTE_T // LANES), axis=1)
    carry_ref[...] = carry_ref[...] + jnp.sum(sel.astype(F32), axis=1, keepdims=True)
    val = jnp.where(sel > 0, rank, ROUTE_NONE)
    for head in range(MB_HEADS):
        vh = val[head * N_KBLOCKS:(head + 1) * N_KBLOCKS]
        for r in range(MB_TOPK):
            m = jnp.min(vh, axis=0, keepdims=True)
            slot = head * MB_TOPK + r
            pos_ref[slot:slot + 1, :] = jnp.where(m < ROUTE_NONE, m, float(NULL_ROW)).astype(jnp.int32)
            vh = jnp.where(vh == m, ROUTE_NONE, vh)


def _route(selt, base):
    n_items = MB_HEADS * N_KBLOCKS
    return pl.pallas_call(
        _route_kernel,
        grid=(SEQ // ROUTE_T,),
        in_specs=[pl.BlockSpec((n_items, ROUTE_T), lambda i: (0, i)),
                  pl.BlockSpec((n_items, LANES), lambda i: (0, 0))],
        out_specs=pl.BlockSpec((MB_HEADS * MB_TOPK, ROUTE_T), lambda i: (0, i)),
        out_shape=jax.ShapeDtypeStruct((MB_HEADS * MB_TOPK, SEQ), jnp.int32),
        scratch_shapes=[pltpu.VMEM((n_items, LANES), F32)],
        compiler_params=_params("arbitrary"),
        name="moba_route",
    )(selt, base)


def _moba_kernel(choff_ref, chblk_ref, chvalid_ref, nch_ref, c31_ref,
                 qf_ref, pos_hbm, k_ref, v_ref, prev_ref, bown_ref, bprev_ref,
                 o_ref, buf_ref, stage_ref, hold_ref, s_ref, p_ref, m_ref,
                 pos_even_ref, pos_odd_ref, pos_sem):
    head = pl.program_id(0)
    t = pl.program_id(1)
    hh = head % 2
    lane = lax.broadcasted_iota(jnp.int32, (1, LANES), 1)
    in_head = (lane // MB_HEAD_DIM) == hh
    stat0 = (1 - hh) * MB_HEAD_DIM
    is_m_lane = lane == stat0
    is_l_lane = lane == stat0 + 1
    contract_last = (((1,), (1,)), ((), ()))
    ones_bf = jnp.ones((), BF16)

    def queries(rows):
        return jnp.where(in_head, rows, 0.0).astype(BF16)

    def values(rows):
        return jnp.where(in_head, rows, ones_bf)

    @pl.when(t == 0)
    def _init():
        def zero(c, _):
            buf_ref[pl.ds(pl.multiple_of(c * MB_BLOCK, MB_BLOCK), MB_BLOCK), :] = (
                jnp.zeros((MB_BLOCK, LANES), F32))
            return 0
        lax.fori_loop(0, BUF_ROWS // MB_BLOCK, zero, 0)

    def dispatch(pos_ref):
        for a in range(STEP_ROWS):
            row = qf_ref[a:a + 1, :]
            for r in range(MB_TOPK):
                buf_ref[pl.ds(pos_ref[r * STEP_ROWS + a], 1), :] = row

    @pl.when(t == NSTEPS)
    def _routed():
        buf_ref[NULL_ROW:NULL_ROW + ROUTE_ALIGN, :] = jnp.where(
            is_m_lane, NEG_INF, jnp.zeros((ROUTE_ALIGN, LANES), F32))
        n = nch_ref[head]
        far_bias = c31_ref[head]
        row_id = lax.broadcasted_iota(jnp.int32, (MB_BLOCK, LANES), 0)

        def meta(i):
            idx = head * MAX_CHUNKS + jnp.clip(i, 0, n - 1)
            off = pl.multiple_of(choff_ref[idx], ROUTE_ALIGN)
            blk = pl.multiple_of(chblk_ref[idx] * MB_BLOCK, MB_BLOCK)
            return off, blk, jnp.where(jnp.logical_and(i >= 0, i < n), chvalid_ref[idx], 0)

        def chunk_queries(off):
            return queries(buf_ref[pl.ds(off, MB_BLOCK), :])

        def scores(qh, blk):
            return lax.dot_general(qh, k_ref[pl.ds(blk, MB_BLOCK), :], contract_last,
                                   preferred_element_type=F32)

        def probs(u):
            s = s_ref[u]
            m = jnp.max(s, axis=1, keepdims=True)
            p_ref[u] = jnp.exp(s - m).astype(BF16)
            m_ref[u] = jnp.broadcast_to(m, (MB_BLOCK, LANES))

        def finish(u, off, blk, valid):
            acc = jnp.dot(p_ref[u], values(v_ref[pl.ds(blk, MB_BLOCK), :]), preferred_element_type=F32)
            new = jnp.where(is_m_lane, m_ref[u] + far_bias, acc)
            pltpu.store(buf_ref.at[pl.ds(off, MB_BLOCK), :], new, mask=row_id < valid)

        @pl.when(n > 0)
        def _chunks():
            slots = range(CHUNKS_PER_BODY)
            p_ref[...] = jnp.zeros_like(p_ref)
            m_ref[...] = jnp.zeros_like(m_ref)
            for u in slots:
                off, blk, _ = meta(u)
                s_ref[u] = scores(chunk_queries(off), blk)

            def chunk_group(c, _):
                i = CHUNKS_PER_BODY * c
                nxt = [meta(i + CHUNKS_PER_BODY + u) for u in slots]
                done = [meta(i - CHUNKS_PER_BODY + u) for u in slots]
                rows = [chunk_queries(off) for off, _, _ in nxt]
                for u in slots:
                    finish(u, *done[u])
                for u in slots:
                    probs(u)
                for u in slots:
                    s_ref[u] = scores(rows[u], nxt[u][1])
                return 0
            lax.fori_loop(0, (n + CHUNKS_PER_BODY - 1) // CHUNKS_PER_BODY + 1, chunk_group, 0)

    def combine(pos_ref):
        step = t - (NSTEPS + 1)
        lane16 = lax.broadcasted_iota(jnp.int32, (1, MB_HEADS), 1)

        n_sub = STEP_ROWS // MB_BLOCK
        windows, all_scores = [], []
        for jj in range(n_sub):
            j = step * n_sub + jj
            rs = slice(jj * MB_BLOCK, (jj + 1) * MB_BLOCK)
            win = pl.ds(pl.multiple_of(jnp.maximum(j - 1, 0) * MB_BLOCK, MB_BLOCK), 2 * MB_BLOCK)
            windows.append(win)
            all_scores.append(lax.dot_general(queries(qf_ref[rs, :]), k_ref[win, :], contract_last,
                                              preferred_element_type=F32))

        for jj in range(n_sub):
            j = step * n_sub + jj
            rs = slice(jj * MB_BLOCK, (jj + 1) * MB_BLOCK)
            s = all_scores[jj]
            picked = jnp.sum(jnp.where(lane16 == head, prev_ref[rs, :], 0.0),
                             axis=1, keepdims=True) > 0.0
            bias_lo, bias_hi = bprev_ref[...], bown_ref[...]
            if jj == 0:
                first = step == 0
                bias_lo = jnp.where(first, bown_ref[...], bias_lo)
                bias_hi = jnp.where(first, NEG_INF, bias_hi)
                picked = jnp.logical_or(picked, first)
            s_lo = jnp.where(picked, s[:, :MB_BLOCK] + bias_lo, NEG_INF)
            s_hi = s[:, MB_BLOCK:] + bias_hi
            m_d = jnp.maximum(jnp.max(s_lo, axis=1, keepdims=True), jnp.max(s_hi, axis=1, keepdims=True))
            p = jnp.concatenate([jnp.exp(s_lo - m_d), jnp.exp(s_hi - m_d)], axis=1).astype(BF16)
            acc = jnp.dot(p, values(v_ref[windows[jj], :]), preferred_element_type=F32)

            for a in range(jj * MB_BLOCK, (jj + 1) * MB_BLOCK):
                for r in range(MB_TOPK):
                    stage_ref[r, a:a + 1, :] = buf_ref[pl.ds(pos_ref[r * STEP_ROWS + a], 1), :]

            parts = [stage_ref[r, rs, :] for r in range(MB_TOPK)]
            ms = [jnp.sum(jnp.where(is_m_lane, part, 0.0), axis=1, keepdims=True) for part in parts]
            m_all = m_d
            for m_r in ms:
                m_all = jnp.maximum(m_all, m_r)
            total = jnp.exp(m_d - m_all) * acc
            for m_r, part in zip(ms, parts):
                total = total + jnp.exp(m_r - m_all) * part
            den = jnp.sum(jnp.where(is_l_lane, total, 0.0), axis=1, keepdims=True)
            out = (total / den).astype(BF16)
            hold_rows = pl.ds(pl.multiple_of(j * MB_BLOCK, MB_BLOCK), MB_BLOCK)

            @pl.when(hh == 0)
            def _keep():
                hold_ref[hold_rows, :] = out

            @pl.when(hh == 1)
            def _emit():
                o_ref[rs, :] = jnp.where(lane < MB_HEAD_DIM, hold_ref[hold_rows, :], out)

    steps_per_head = 2 * NSTEPS
    use = head * steps_per_head + jnp.where(t < NSTEPS, t, t - 1)
    pos_bufs = (pos_even_ref, pos_odd_ref)

    def pos_copy(u, parity):
        block = (u // steps_per_head) * NSTEPS + u % NSTEPS
        src = pos_hbm.at[pl.ds(pl.multiple_of(block * POS_BLOCK, POS_BLOCK), POS_BLOCK)]
        return pltpu.make_async_copy(src, pos_bufs[parity], pos_sem.at[parity])

    for parity in range(2):
        @pl.when(jnp.logical_and(t != NSTEPS, use % 2 == parity))
        def _positions_step(parity=parity):
            @pl.when(use == 0)
            def _first():
                pos_copy(use, parity).start()

            @pl.when(use + 1 < MB_HEADS * steps_per_head)
            def _prefetch():
                pos_copy(use + 1, 1 - parity).start()

            pos_copy(use, parity).wait()

            @pl.when(t < NSTEPS)
            def _dispatch():
                dispatch(pos_bufs[parity])

            @pl.when(t > NSTEPS)
            def _combine():
                combine(pos_bufs[parity])


def _moba_attention(chunk_meta, c31, qf, pos, k, v, prevc, bias_own, bias_prev):
    def seq_step(t):
        return jnp.where(t < NSTEPS, t, jnp.maximum(t - (NSTEPS + 1), 0) + jnp.where(t == NSTEPS, NSTEPS - 1, 0))

    def combine_step(t):
        return jnp.maximum(t - (NSTEPS + 1), 0)

    pair_resident = pl.BlockSpec((SEQ, LANES), lambda h, t, *_: (0, h // 2), pipeline_mode=pl.Buffered(1))
    bias = pl.BlockSpec((None, MB_BLOCK, MB_BLOCK), lambda h, t, *_: (h, 0, 0))
    return pl.pallas_call(
        _moba_kernel,
        grid_spec=pltpu.PrefetchScalarGridSpec(
            num_scalar_prefetch=5,
            grid=(MB_HEADS, 2 * NSTEPS + 1),
            in_specs=[
                pl.BlockSpec((STEP_ROWS, LANES), lambda h, t, *_: (seq_step(t), h // 2)),
                pl.BlockSpec(memory_space=pl.ANY),
                pair_resident, pair_resident,
                pl.BlockSpec((STEP_ROWS, MB_HEADS), lambda h, t, *_: (combine_step(t), 0)),
                bias, bias,
            ],
            out_specs=pl.BlockSpec(
                (STEP_ROWS, LANES),
                lambda h, t, *_: (jnp.where(h % 2 == 1, combine_step(t), 0), h // 2)),
            scratch_shapes=[pltpu.VMEM((BUF_ROWS, LANES), F32),
                            pltpu.VMEM((MB_TOPK, STEP_ROWS, LANES), F32),
                            pltpu.VMEM((SEQ, LANES), BF16),
                            pltpu.VMEM((CHUNKS_PER_BODY, MB_BLOCK, MB_BLOCK), F32),
                            pltpu.VMEM((CHUNKS_PER_BODY, MB_BLOCK, MB_BLOCK), BF16),
                            pltpu.VMEM((CHUNKS_PER_BODY, MB_BLOCK, LANES), F32),
                            pltpu.SMEM((POS_BLOCK,), jnp.int32), pltpu.SMEM((POS_BLOCK,), jnp.int32),
                            pltpu.SemaphoreType.DMA((2,))],
        ),
        out_shape=jax.ShapeDtypeStruct((SEQ, D_MODEL), BF16),
        compiler_params=_params("arbitrary", "arbitrary"),
        name="moba_attention",
    )(*chunk_meta, c31, qf, pos, k, v, prevc, bias_own, bias_prev)


def _proj_kernel(x_ref, a_ref, vec_ref, w_ref, o_ref):
    y = jnp.dot(a_ref[...], w_ref[...], preferred_element_type=F32)
    o_ref[...] = x_ref[...] + vec_ref[3:4, :] * y


def _out_proj(x, a, vec, w_o):
    row = pl.BlockSpec((PROJ_TM, D_MODEL), lambda i: (i, 0))
    return pl.pallas_call(
        _proj_kernel,
        grid=(SEQ // PROJ_TM,),
        in_specs=[row, row, _resident((8, D_MODEL)), _resident((D_MODEL, D_MODEL))],
        out_specs=row,
        out_shape=jax.ShapeDtypeStruct((SEQ, D_MODEL), F32),
        compiler_params=_params("parallel"),
        name="moba_out_proj",
    )(x, a, vec, w_o)


def _t5_bucket_np(rel):
    n = np.maximum(rel, 0)
    nf = np.maximum(n, T5_MAX_EXACT).astype(np.float32)
    large = T5_MAX_EXACT + (np.log(nf / np.float32(T5_MAX_EXACT))
                            / np.float32(math.log(T5_MAX_DISTANCE / T5_MAX_EXACT))
                            * np.float32(T5_NUM_BUCKETS - T5_MAX_EXACT)).astype(np.int32)
    large = np.minimum(large, T5_NUM_BUCKETS - 1)
    return np.where(n < T5_MAX_EXACT, n, large)


def _moba_bias_tables(rel_bias):
    a = np.arange(MB_BLOCK)
    rel_own = a[:, None] - a[None, :]
    table = rel_bias.astype(F32).T

    def expand(rel):
        bucket = jnp.asarray(_t5_bucket_np(rel).reshape(-1).astype(np.int32))
        onehot = (bucket[None, :] == jnp.arange(T5_NUM_BUCKETS)[:, None]).astype(F32)
        return jnp.dot(table, onehot, precision=lax.Precision.HIGHEST).reshape(
            MB_HEADS, MB_BLOCK, MB_BLOCK)

    b_own = jnp.where(jnp.asarray(rel_own >= 0)[None], expand(rel_own), NEG_INF)
    return b_own, expand(rel_own + MB_BLOCK), table[:, T5_NUM_BUCKETS - 1]


def _chunk_tables(cnt, start):
    nch = (cnt + MB_BLOCK - 1) // MB_BLOCK
    cum = jnp.cumsum(nch, axis=1)
    i = jnp.arange(MAX_CHUNKS)
    blk = jnp.sum(cum[:, None, :] <= i[None, :, None], axis=2)
    blk = jnp.minimum(blk, N_KBLOCKS - 1)
    onehot = blk[:, :, None] == jnp.arange(N_KBLOCKS)[None, None, :]

    def pick(x):
        return jnp.sum(jnp.where(onehot, x[:, None, :], 0), axis=2)

    def flat(x):
        return x.reshape(-1).astype(jnp.int32)
    within = i[None, :] - pick(cum - nch)
    off = pick(start) + within * MB_BLOCK
    valid = jnp.clip(pick(cnt) - within * MB_BLOCK, 0, MB_BLOCK)
    return flat(off), flat(blk), flat(valid), flat(cum[:, -1])


def _moba(xs, vec, w_qkv, w_o, rel_bias):
    qf, k, v, selt, prevt, cnt = _qkv(xs, vec, w_qkv)
    cnt = cnt[:, 0].astype(jnp.int32).reshape(MB_HEADS, N_KBLOCKS)
    padded = (cnt + ROUTE_ALIGN - 1) // ROUTE_ALIGN * ROUTE_ALIGN
    start = jnp.cumsum(padded, axis=1) - padded
    base = jnp.broadcast_to(start.reshape(-1, 1).astype(F32), (MB_HEADS * N_KBLOCKS, LANES))
    pos = _route(selt, base).reshape(MB_HEADS, MB_TOPK, NSTEPS, STEP_ROWS)
    pos = pos.transpose(0, 2, 1, 3).reshape(-1)
    b_own, b_prev, c31 = _moba_bias_tables(rel_bias)
    a = _moba_attention(_chunk_tables(cnt, start), c31, qf, pos, k, v, prevt.T, b_own, b_prev)
    return _out_proj(xs, a, vec, w_o)


def _vec(norm_g, mod3):
    return jnp.concatenate([norm_g[None, :], mod3, jnp.zeros((4, D_MODEL), F32)], axis=0)


def kernel(x, c, rel_bias, mod_w, mod_b, norm_g, ffn_w_in, ffn_w_out, gmlp_w_in, gmlp_v_norm,
           gmlp_w_s, gmlp_b_s, gmlp_w_out, moba_w_qkv, moba_w_o, final_norm):
    assert x.shape == (1, SEQ, D_MODEL)
    xs = x.reshape(SEQ, D_MODEL)
    mod = _modulation(c.reshape(D_MODEL, 1), mod_w, mod_b).reshape(DEPTH, N_SUBLAYERS, 3, D_MODEL)
    fin = final_norm.reshape(1, D_MODEL)

    for i in range(DEPTH):
        li = i // 2
        xs = _ffn(xs, _vec(norm_g[i, 0], mod[i, 0]), ffn_w_in[i, 0].astype(BF16),
                  ffn_w_out[i, 0].astype(BF16), fin, False)
        vec = _vec(norm_g[i, 1], mod[i, 1])
        if i % 2 == 0:
            bias = jnp.repeat(gmlp_b_s[li].T, GM_GROUP_DIM, axis=1)
            xs = _gmlp(xs, vec, gmlp_w_in[li].astype(BF16), gmlp_v_norm[li].reshape(1, GM_D),
                       gmlp_w_s[li].astype(BF16), bias, gmlp_w_out[li].astype(BF16))
        else:
            xs = _moba(xs, vec, moba_w_qkv[li].astype(BF16), moba_w_o[li].astype(BF16), rel_bias)
        xs = _ffn(xs, _vec(norm_g[i, 2], mod[i, 2]), ffn_w_in[i, 1].astype(BF16),
                  ffn_w_out[i, 1].astype(BF16), fin, i == DEPTH - 1)
    return xs.reshape(1, SEQ, D_MODEL)
```

```python
import functools
import math

import numpy as np
import jax
import jax.numpy as jnp
from jax import lax
from jax.experimental import pallas as pl
from jax.experimental.pallas import tpu as pltpu

F32 = jnp.float32
BF16 = jnp.bfloat16

D_MODEL = 1024
SEQ = 16384
DEPTH = 2
N_SUBLAYERS = 3
RMS_EPS = 1e-6
D_FF = 2816

GM_D = 3 * D_MODEL
GM_GROUPS = 16
GM_GROUP_DIM = GM_D // GM_GROUPS
GM_CHUNK = 128
GM_PAIR = 2 * GM_GROUP_DIM

MB_HEADS = 16
MB_HEAD_DIM = 64
MB_BLOCK = 256
MB_TOPK = 3
MB_SCALE = MB_HEAD_DIM ** -0.5
N_KBLOCKS = SEQ // MB_BLOCK

T5_NUM_BUCKETS = 32
T5_MAX_EXACT = 16
T5_MAX_DISTANCE = 128
NEG_INF = -1e30
BELOW_NEG_INF = -3e38

LANES = 128
VMEM_LIMIT = 56 * 1024 * 1024

MOD_TN = 1536
FFN_TM = 512
FFN_CHUNK = 1408
GM_TM = 256
GM_VCHUNK = 768
QKV_TM = 512
PROJ_TM = 512

ROUTE_T = 512
ROUTE_ALIGN = 8
STEP_ROWS = 1024
NSTEPS = SEQ // STEP_ROWS
BUF_DATA_ROWS = MB_TOPK * SEQ + N_KBLOCKS * ROUTE_ALIGN
BUF_ROWS = BUF_DATA_ROWS + 3 * MB_BLOCK
NULL_ROW = BUF_ROWS - ROUTE_ALIGN
ROUTE_NONE = 1e9
MAX_CHUNKS = MB_TOPK * SEQ // MB_BLOCK + N_KBLOCKS
POS_BLOCK = MB_TOPK * STEP_ROWS
CHUNKS_PER_BODY = 8


def _resident(shape):
    nd = len(shape)
    return pl.BlockSpec(shape, lambda *_: (0,) * nd, pipeline_mode=pl.Buffered(1))


def _params(*sem):
    return pltpu.CompilerParams(dimension_semantics=sem, vmem_limit_bytes=VMEM_LIMIT)


def _rms(x):
    return x * lax.rsqrt(jnp.mean(x * x, axis=-1, keepdims=True) + RMS_EPS)


def _adaln(x, vec_ref):
    y = _rms(x) * vec_ref[0:1, :]
    return y * (1.0 + vec_ref[2:3, :]) + vec_ref[1:2, :]


def _gelu(x):
    return 0.5 * x * (1.0 + lax.erf(x * np.float32(math.sqrt(0.5))))


def _mod_kernel(c_ref, w_ref, b_ref, o_ref):
    c = c_ref[...]
    c_act = c * jax.nn.sigmoid(c)
    o_ref[...] = jnp.sum(c_act * w_ref[...], axis=0, keepdims=True) + b_ref[...]


def _modulation(c_col, mod_w, mod_b):
    n = mod_w.shape[-1]
    return pl.pallas_call(
        _mod_kernel,
        grid=(DEPTH, n // MOD_TN),
        in_specs=[
            pl.BlockSpec((D_MODEL, 1), lambda i, j: (0, 0)),
            pl.BlockSpec((None, D_MODEL, MOD_TN), lambda i, j: (i, 0, j)),
            pl.BlockSpec((None, 1, MOD_TN), lambda i, j: (i, 0, j)),
        ],
        out_specs=pl.BlockSpec((None, 1, MOD_TN), lambda i, j: (i, 0, j)),
        out_shape=jax.ShapeDtypeStruct((DEPTH, 1, n), F32),
        compiler_params=_params("arbitrary", "arbitrary"),
        name="modulation",
    )(c_col, mod_w, mod_b.reshape(DEPTH, 1, n))


def _ffn_kernel(x_ref, vec_ref, win_ref, wout_ref, fin_ref, o_ref, act_ref, *, final):
    x = x_ref[...]
    h = _adaln(x, vec_ref).astype(BF16)
    for c in range(D_FF // FFN_CHUNK):
        lo = c * FFN_CHUNK
        g = jnp.dot(h, win_ref[:, lo:lo + FFN_CHUNK], preferred_element_type=F32)
        u = jnp.dot(h, win_ref[:, D_FF + lo:D_FF + lo + FFN_CHUNK], preferred_element_type=F32)
        act_ref[:, lo:lo + FFN_CHUNK] = (g * jax.nn.sigmoid(g) * u).astype(BF16)
    y = jnp.dot(act_ref[...], wout_ref[...], preferred_element_type=F32)
    out = x + (0.5 * vec_ref[3:4, :]) * y
    if final:
        out = _rms(out) * fin_ref[...]
    o_ref[...] = out


def _ffn(x, vec, w_in, w_out, fin, final):
    return pl.pallas_call(
        functools.partial(_ffn_kernel, final=final),
        grid=(SEQ // FFN_TM,),
        in_specs=[
            pl.BlockSpec((FFN_TM, D_MODEL), lambda i: (i, 0)),
            _resident((8, D_MODEL)),
            _resident((D_MODEL, 2 * D_FF)),
            _resident((D_FF, D_MODEL)),
            _resident((1, D_MODEL)),
        ],
        out_specs=pl.BlockSpec((FFN_TM, D_MODEL), lambda i: (i, 0)),
        out_shape=jax.ShapeDtypeStruct((SEQ, D_MODEL), F32),
        scratch_shapes=[pltpu.VMEM((FFN_TM, D_FF), BF16)],
        compiler_params=_params("parallel"),
        name="ffn_final" if final else "ffn",
    )(x, vec, w_in, w_out, fin)


def _gmlp_kernel(x_ref, vec_ref, win_ref, vnorm_ref, ws_ref, bias_ref, wout_ref,
                 o_ref, v_ref, g_ref):
    x = x_ref[...]
    h = _adaln(x, vec_ref).astype(BF16)

    ssq = jnp.zeros((GM_TM, 1), F32)
    for c in range(GM_D // GM_VCHUNK):
        lo = c * GM_VCHUNK
        v = _gelu(jnp.dot(h, win_ref[:, GM_D + lo:GM_D + lo + GM_VCHUNK],
                          preferred_element_type=F32))
        v_ref[:, lo:lo + GM_VCHUNK] = v
        ssq = ssq + jnp.sum(v * v, axis=-1, keepdims=True)
    r = lax.rsqrt(ssq * (1.0 / GM_D) + RMS_EPS)

    row = lax.broadcasted_iota(jnp.int32, (GM_CHUNK, GM_CHUNK), 0)
    col = lax.broadcasted_iota(jnp.int32, (GM_CHUNK, GM_CHUNK), 1)
    causal = row >= col
    low_half = lax.broadcasted_iota(jnp.int32, (GM_CHUNK, LANES), 1) < (GM_GROUP_DIM - LANES)

    for p in range(GM_GROUPS // 2):
        lo = p * GM_PAIR
        vn = (v_ref[:, lo:lo + GM_PAIR] * r * vnorm_ref[:, lo:lo + GM_PAIR]).astype(BF16)
        u = _gelu(jnp.dot(h, win_ref[:, lo:lo + GM_PAIR], preferred_element_type=F32))
        w0 = jnp.where(causal, ws_ref[2 * p], jnp.zeros((), BF16))
        w1 = jnp.where(causal, ws_ref[2 * p + 1], jnp.zeros((), BF16))
        for c in range(GM_TM // GM_CHUNK):
            rows = slice(c * GM_CHUNK, (c + 1) * GM_CHUNK)
            vc = vn[rows]
            a = jnp.dot(w0, vc[:, :2 * LANES], preferred_element_type=F32)
            b = jnp.dot(w1, vc[:, LANES:], preferred_element_type=F32)
            mid = jnp.where(low_half, a[:, LANES:], b[:, :LANES])
            sv = jnp.concatenate([a[:, :LANES], mid, b[:, LANES:]], axis=1)
            sv = sv + bias_ref[:, lo:lo + GM_PAIR]
            g_ref[rows, lo:lo + GM_PAIR] = (u[rows] * sv).astype(BF16)

    y = jnp.dot(g_ref[...], wout_ref[...], preferred_element_type=F32)
    o_ref[...] = x + vec_ref[3:4, :] * y


def _gmlp(x, vec, w_in, v_norm, w_s, bias, w_out):
    return pl.pallas_call(
        _gmlp_kernel,
        grid=(SEQ // GM_TM,),
        in_specs=[
            pl.BlockSpec((GM_TM, D_MODEL), lambda i: (i, 0)),
            _resident((8, D_MODEL)),
            _resident((D_MODEL, 2 * GM_D)),
            _resident((1, GM_D)),
            _resident((GM_GROUPS, GM_CHUNK, GM_CHUNK)),
            _resident((GM_CHUNK, GM_D)),
            _resident((GM_D, D_MODEL)),
        ],
        out_specs=pl.BlockSpec((GM_TM, D_MODEL), lambda i: (i, 0)),
        out_shape=jax.ShapeDtypeStruct((SEQ, D_MODEL), F32),
        scratch_shapes=[pltpu.VMEM((GM_TM, GM_D), F32), pltpu.VMEM((GM_TM, GM_D), BF16)],
        compiler_params=_params("parallel"),
        name="gmlp",
    )(x, vec, w_in, v_norm, w_s, bias, w_out)


def _qkv_kernel(x_ref, vec_ref, w_ref, q_ref, k_ref, v_ref, selt_ref, prevt_ref, cnt_ref,
                kmean_ref):
    i = pl.program_id(0)

    @pl.when(i == 0)
    def _init():
        kmean_ref[...] = jnp.zeros_like(kmean_ref)
        cnt_ref[...] = jnp.zeros_like(cnt_ref)

    h = _adaln(x_ref[...], vec_ref).astype(BF16)
    q = jnp.dot(h, w_ref[:, :D_MODEL], preferred_element_type=F32) * MB_SCALE
    q_ref[...] = q
    k = jnp.dot(h, w_ref[:, D_MODEL:2 * D_MODEL], preferred_element_type=F32)
    k_ref[...] = k.astype(BF16)
    nb = QKV_TM // MB_BLOCK
    for b in range(nb):
        kmean_ref[pl.ds(nb * i + b, 1), :] = jnp.mean(
            k[b * MB_BLOCK:(b + 1) * MB_BLOCK], axis=0, keepdims=True)
    v = jnp.dot(h, w_ref[:, 2 * D_MODEL:], preferred_element_type=F32)
    v_ref[...] = v.astype(BF16)

    lane = lax.broadcasted_iota(jnp.int32, (1, LANES), 1)
    blk = lax.broadcasted_iota(jnp.int32, (N_KBLOCKS, MB_BLOCK), 0)
    contract_last = (((1,), (1,)), ((), ()))
    for b in range(nb):
        j = nb * i + b
        cols = slice(b * MB_BLOCK, (b + 1) * MB_BLOCK)
        for pair in range(MB_HEADS // 2):
            km = kmean_ref[:, pair * LANES:(pair + 1) * LANES]
            qp = q[cols, pair * LANES:(pair + 1) * LANES]
            for hh in range(2):
                head = 2 * pair + hh
                km_h = jnp.where((lane // MB_HEAD_DIM) == hh, km, 0.0)
                g = lax.dot_general(km_h, qp, contract_last, preferred_element_type=F32)
                g = jnp.where(blk < j, g, NEG_INF)
                sel = jnp.zeros((N_KBLOCKS, MB_BLOCK), F32)
                for r in range(MB_TOPK):
                    m = jnp.max(g, axis=0, keepdims=True)
                    first = jnp.min(jnp.where(g == m, blk, N_KBLOCKS), axis=0, keepdims=True)
                    hit = blk == first
                    sel = jnp.where(jnp.logical_and(hit, r < j), 1.0, sel)
                    g = jnp.where(hit, BELOW_NEG_INF, g)
                prevt_ref[head:head + 1, cols] = jnp.max(
                    jnp.where(blk == j - 1, sel, 0.0), axis=0, keepdims=True)
                routed = jnp.where(blk < j - 1, sel, 0.0)
                selt_ref[head * N_KBLOCKS:(head + 1) * N_KBLOCKS, cols] = routed.astype(BF16)
                rows = slice(head * N_KBLOCKS, (head + 1) * N_KBLOCKS)
                cnt_ref[rows, :] = cnt_ref[rows, :] + jnp.sum(routed, axis=1, keepdims=True)


def _qkv(x, vec, w_qkv):
    row = pl.BlockSpec((QKV_TM, D_MODEL), lambda i: (i, 0))
    n_items = MB_HEADS * N_KBLOCKS
    return pl.pallas_call(
        _qkv_kernel,
        grid=(SEQ // QKV_TM,),
        in_specs=[row, _resident((8, D_MODEL)), _resident((D_MODEL, 3 * D_MODEL))],
        out_specs=[row, row, row,
                   pl.BlockSpec((n_items, QKV_TM), lambda i: (0, i)),
                   pl.BlockSpec((MB_HEADS, QKV_TM), lambda i: (0, i)),
                   pl.BlockSpec((n_items, LANES), lambda i: (0, 0))],
        out_shape=[jax.ShapeDtypeStruct((SEQ, D_MODEL), F32),
                   jax.ShapeDtypeStruct((SEQ, D_MODEL), BF16),
                   jax.ShapeDtypeStruct((SEQ, D_MODEL), BF16),
                   jax.ShapeDtypeStruct((n_items, SEQ), BF16),
                   jax.ShapeDtypeStruct((MB_HEADS, SEQ), F32),
                   jax.ShapeDtypeStruct((n_items, LANES), F32)],
        scratch_shapes=[pltpu.VMEM((N_KBLOCKS, D_MODEL), F32)],
        compiler_params=_params("arbitrary"),
        name="moba_qkv_gate",
    )(x, vec, w_qkv)


def _route_kernel(selt_ref, base_ref, pos_ref, carry_ref):
    i = pl.program_id(0)

    @pl.when(i == 0)
    def _init():
        carry_ref[...] = jnp.zeros_like(carry_ref)

    sel = selt_ref[...]
    before = (lax.broadcasted_iota(jnp.int32, (ROUTE_T, ROUTE_T), 0)
              < lax.broadcasted_iota(jnp.int32, (ROUTE_T, ROUTE_T), 1))
    rank = jnp.dot(sel, jnp.where(before, 1.0, 0.0).astype(BF16), preferred_element_type=F32)
    offset = base_ref[...] + carry_ref[...]
    rank = rank + jnp.concatenate([offset] * (ROUTE_T // LANES), axis=1)
    carry_ref[...] = carry_ref[...] + jnp.sum(sel.astype(F32), axis=1, keepdims=True)
    val = jnp.where(sel > 0, rank, ROUTE_NONE)
    for head in range(MB_HEADS):
        vh = val[head * N_KBLOCKS:(head + 1) * N_KBLOCKS]
        for r in range(MB_TOPK):
            m = jnp.min(vh, axis=0, keepdims=True)
            slot = head * MB_TOPK + r
            pos_ref[slot:slot + 1, :] = jnp.where(m < ROUTE_NONE, m, float(NULL_ROW)).astype(jnp.int32)
            vh = jnp.where(vh == m, ROUTE_NONE, vh)


def _route(selt, base):
    n_items = MB_HEADS * N_KBLOCKS
    return pl.pallas_call(
        _route_kernel,
        grid=(SEQ // ROUTE_T,),
        in_specs=[pl.BlockSpec((n_items, ROUTE_T), lambda i: (0, i)),
                  pl.BlockSpec((n_items, LANES), lambda i: (0, 0))],
        out_specs=pl.BlockSpec((MB_HEADS * MB_TOPK, ROUTE_T), lambda i: (0, i)),
        out_shape=jax.ShapeDtypeStruct((MB_HEADS * MB_TOPK, SEQ), jnp.int32),
        scratch_shapes=[pltpu.VMEM((n_items, LANES), F32)],
        compiler_params=_params("arbitrary"),
        name="moba_route",
    )(selt, base)


def _moba_kernel(choff_ref, chblk_ref, chvalid_ref, nch_ref, c31_ref,
                 qf_ref, pos_hbm, k_ref, v_ref, prev_ref, bown_ref, bprev_ref,
                 o_ref, buf_ref, stage_ref, hold_ref, s_ref, p_ref, m_ref,
                 pos_even_ref, pos_odd_ref, pos_sem):
    head = pl.program_id(0)
    t = pl.program_id(1)
    hh = head % 2
    lane = lax.broadcasted_iota(jnp.int32, (1, LANES), 1)
    in_head = (lane // MB_HEAD_DIM) == hh
    stat0 = (1 - hh) * MB_HEAD_DIM
    is_m_lane = lane == stat0
    is_l_lane = lane == stat0 + 1
    contract_last = (((1,), (1,)), ((), ()))
    ones_bf = jnp.ones((), BF16)

    def queries(rows):
        return jnp.where(in_head, rows, 0.0).astype(BF16)

    def values(rows):
        return jnp.where(in_head, rows, ones_bf)

    @pl.when(t == 0)
    def _init():
        def zero(c, _):
            buf_ref[pl.ds(pl.multiple_of(c * MB_BLOCK, MB_BLOCK), MB_BLOCK), :] = (
                jnp.zeros((MB_BLOCK, LANES), F32))
            return 0
        lax.fori_loop(0, BUF_ROWS // MB_BLOCK, zero, 0)

        @pl.when(head == 0)
        def _init_hold():
            def zero_hold(c, _):
                hold_ref[pl.ds(pl.multiple_of(c * MB_BLOCK, MB_BLOCK), MB_BLOCK), :] = (
                    jnp.zeros((MB_BLOCK, LANES), BF16))
                return 0
            lax.fori_loop(0, SEQ // MB_BLOCK, zero_hold, 0)

    def dispatch(pos_ref):
        for a in range(STEP_ROWS):
            row = qf_ref[a:a + 1, :]
            for r in range(MB_TOPK):
                buf_ref[pl.ds(pos_ref[r * STEP_ROWS + a], 1), :] = row

    @pl.when(t == NSTEPS)
    def _routed():
        buf_ref[NULL_ROW:NULL_ROW + ROUTE_ALIGN, :] = jnp.where(
            is_m_lane, NEG_INF, jnp.zeros((ROUTE_ALIGN, LANES), F32))
        n = nch_ref[head]
        far_bias = c31_ref[head]
        row_id = lax.broadcasted_iota(jnp.int32, (MB_BLOCK, LANES), 0)

        def meta(i):
            idx = head * MAX_CHUNKS + jnp.clip(i, 0, n - 1)
            off = pl.multiple_of(choff_ref[idx], ROUTE_ALIGN)
            blk = pl.multiple_of(chblk_ref[idx] * MB_BLOCK, MB_BLOCK)
            return off, blk, jnp.where(jnp.logical_and(i >= 0, i < n), chvalid_ref[idx], 0)

        def chunk_queries(off):
            return queries(buf_ref[pl.ds(off, MB_BLOCK), :])

        def scores(qh, blk):
            return lax.dot_general(qh, k_ref[pl.ds(blk, MB_BLOCK), :], contract_last,
                                   preferred_element_type=F32)

        def probs(u):
            s = s_ref[u]
            m = jnp.max(s, axis=1, keepdims=True)
            p_ref[u] = jnp.exp(s - m).astype(BF16)
            m_ref[u] = jnp.broadcast_to(m, (MB_BLOCK, LANES))

        def finish(u, off, blk, valid):
            acc = jnp.dot(p_ref[u], values(v_ref[pl.ds(blk, MB_BLOCK), :]), preferred_element_type=F32)
            new = jnp.where(is_m_lane, m_ref[u] + far_bias, acc)
            pltpu.store(buf_ref.at[pl.ds(off, MB_BLOCK), :], new, mask=row_id < valid)

        @pl.when(n > 0)
        def _chunks():
            slots = range(CHUNKS_PER_BODY)
            p_ref[...] = jnp.zeros_like(p_ref)
            m_ref[...] = jnp.zeros_like(m_ref)
            for u in slots:
                off, blk, _ = meta(u)
                s_ref[u] = scores(chunk_queries(off), blk)

            def chunk_group(c, _):
                i = CHUNKS_PER_BODY * c
                nxt = [meta(i + CHUNKS_PER_BODY + u) for u in slots]
                done = [meta(i - CHUNKS_PER_BODY + u) for u in slots]
                rows = [chunk_queries(off) for off, _, _ in nxt]
                for u in slots:
                    finish(u, *done[u])
                for u in slots:
                    probs(u)
                for u in slots:
                    s_ref[u] = scores(rows[u], nxt[u][1])
                return 0
            lax.fori_loop(0, (n + CHUNKS_PER_BODY - 1) // CHUNKS_PER_BODY + 1, chunk_group, 0)

    def combine(pos_ref):
        step = t - (NSTEPS + 1)
        lane16 = lax.broadcasted_iota(jnp.int32, (1, MB_HEADS), 1)

        n_sub = STEP_ROWS // MB_BLOCK
        for jj in range(n_sub):
            j = step * n_sub + jj
            rs = slice(jj * MB_BLOCK, (jj + 1) * MB_BLOCK)
            win = pl.ds(pl.multiple_of(jnp.maximum(j - 1, 0) * MB_BLOCK, MB_BLOCK), 2 * MB_BLOCK)
            s = lax.dot_general(queries(qf_ref[rs, :]), k_ref[win, :], contract_last,
                                preferred_element_type=F32)
            picked = jnp.sum(jnp.where(lane16 == head, prev_ref[rs, :], 0.0),
                             axis=1, keepdims=True) > 0.0
            bias_lo, bias_hi = bprev_ref[...], bown_ref[...]
            if jj == 0:
                first = step == 0
                bias_lo = jnp.where(first, bown_ref[...], bias_lo)
                bias_hi = jnp.where(first, NEG_INF, bias_hi)
                picked = jnp.logical_or(picked, first)
            s_lo = jnp.where(picked, s[:, :MB_BLOCK] + bias_lo, NEG_INF)
            s_hi = s[:, MB_BLOCK:] + bias_hi
            m_d = jnp.maximum(jnp.max(s_lo, axis=1, keepdims=True), jnp.max(s_hi, axis=1, keepdims=True))
            p = jnp.concatenate([jnp.exp(s_lo - m_d), jnp.exp(s_hi - m_d)], axis=1).astype(BF16)
            acc = jnp.dot(p, values(v_ref[win, :]), preferred_element_type=F32)

            for a in range(jj * MB_BLOCK, (jj + 1) * MB_BLOCK):
                for r in range(MB_TOPK):
                    stage_ref[r, a:a + 1, :] = buf_ref[pl.ds(pos_ref[r * STEP_ROWS + a], 1), :]

            parts = [stage_ref[r, rs, :] for r in range(MB_TOPK)]
            ms = [jnp.sum(jnp.where(is_m_lane, part, 0.0), axis=1, keepdims=True) for part in parts]
            m_all = m_d
            for m_r in ms:
                m_all = jnp.maximum(m_all, m_r)
            total = jnp.exp(m_d - m_all) * acc
            for m_r, part in zip(ms, parts):
                total = total + jnp.exp(m_r - m_all) * part
            den = jnp.sum(jnp.where(is_l_lane, total, 0.0), axis=1, keepdims=True)
            out = (total / den).astype(BF16)
            hold_rows = pl.ds(pl.multiple_of(j * MB_BLOCK, MB_BLOCK), MB_BLOCK)
            take_parked = jnp.logical_and(lane < MB_HEAD_DIM, hh == 1)
            merged = jnp.where(take_parked, hold_ref[hold_rows, :], out)
            hold_ref[hold_rows, :] = merged
            o_ref[rs, :] = merged

    steps_per_head = 2 * NSTEPS
    use = head * steps_per_head + jnp.where(t < NSTEPS, t, t - 1)
    pos_bufs = (pos_even_ref, pos_odd_ref)

    def pos_copy(u, parity):
        block = (u // steps_per_head) * NSTEPS + u % NSTEPS
        src = pos_hbm.at[pl.ds(pl.multiple_of(block * POS_BLOCK, POS_BLOCK), POS_BLOCK)]
        return pltpu.make_async_copy(src, pos_bufs[parity], pos_sem.at[parity])

    for parity in range(2):
        @pl.when(jnp.logical_and(t != NSTEPS, use % 2 == parity))
        def _positions_step(parity=parity):
            @pl.when(use == 0)
            def _first():
                pos_copy(use, parity).start()

            @pl.when(use + 1 < MB_HEADS * steps_per_head)
            def _prefetch():
                pos_copy(use + 1, 1 - parity).start()

            pos_copy(use, parity).wait()

            @pl.when(t < NSTEPS)
            def _dispatch():
                dispatch(pos_bufs[parity])

            @pl.when(t > NSTEPS)
            def _combine():
                combine(pos_bufs[parity])


def _moba_attention(chunk_meta, c31, qf, pos, k, v, prevc, bias_own, bias_prev):
    def seq_step(t):
        return jnp.where(t < NSTEPS, t, jnp.maximum(t - (NSTEPS + 1), 0) + jnp.where(t == NSTEPS, NSTEPS - 1, 0))

    def combine_step(t):
        return jnp.maximum(t - (NSTEPS + 1), 0)

    pair_resident = pl.BlockSpec((SEQ, LANES), lambda h, t, *_: (0, h // 2), pipeline_mode=pl.Buffered(1))
    bias = pl.BlockSpec((None, MB_BLOCK, MB_BLOCK), lambda h, t, *_: (h, 0, 0))
    return pl.pallas_call(
        _moba_kernel,
        grid_spec=pltpu.PrefetchScalarGridSpec(
            num_scalar_prefetch=5,
            grid=(MB_HEADS, 2 * NSTEPS + 1),
            in_specs=[
                pl.BlockSpec((STEP_ROWS, LANES), lambda h, t, *_: (seq_step(t), h // 2)),
                pl.BlockSpec(memory_space=pl.ANY),
                pair_resident, pair_resident,
                pl.BlockSpec((STEP_ROWS, MB_HEADS), lambda h, t, *_: (combine_step(t), 0)),
                bias, bias,
            ],
            out_specs=pl.BlockSpec(
                (STEP_ROWS, LANES),
                lambda h, t, *_: (jnp.where(h % 2 == 1, combine_step(t), 0), h // 2)),
            scratch_shapes=[pltpu.VMEM((BUF_ROWS, LANES), F32),
                            pltpu.VMEM((MB_TOPK, STEP_ROWS, LANES), F32),
                            pltpu.VMEM((SEQ, LANES), BF16),
                            pltpu.VMEM((CHUNKS_PER_BODY, MB_BLOCK, MB_BLOCK), F32),
                            pltpu.VMEM((CHUNKS_PER_BODY, MB_BLOCK, MB_BLOCK), BF16),
                            pltpu.VMEM((CHUNKS_PER_BODY, MB_BLOCK, LANES), F32),
                            pltpu.SMEM((POS_BLOCK,), jnp.int32), pltpu.SMEM((POS_BLOCK,), jnp.int32),
                            pltpu.SemaphoreType.DMA((2,))],
        ),
        out_shape=jax.ShapeDtypeStruct((SEQ, D_MODEL), BF16),
        compiler_params=_params("arbitrary", "arbitrary"),
        name="moba_attention",
    )(*chunk_meta, c31, qf, pos, k, v, prevc, bias_own, bias_prev)


def _proj_kernel(x_ref, a_ref, vec_ref, w_ref, o_ref):
    y = jnp.dot(a_ref[...], w_ref[...], preferred_element_type=F32)
    o_ref[...] = x_ref[...] + vec_ref[3:4, :] * y


def _out_proj(x, a, vec, w_o):
    row = pl.BlockSpec((PROJ_TM, D_MODEL), lambda i: (i, 0))
    return pl.pallas_call(
        _proj_kernel,
        grid=(SEQ // PROJ_TM,),
        in_specs=[row, row, _resident((8, D_MODEL)), _resident((D_MODEL, D_MODEL))],
        out_specs=row,
        out_shape=jax.ShapeDtypeStruct((SEQ, D_MODEL), F32),
        compiler_params=_params("parallel"),
        name="moba_out_proj",
    )(x, a, vec, w_o)


def _t5_bucket_np(rel):
    n = np.maximum(rel, 0)
    nf = np.maximum(n, T5_MAX_EXACT).astype(np.float32)
    large = T5_MAX_EXACT + (np.log(nf / np.float32(T5_MAX_EXACT))
                            / np.float32(math.log(T5_MAX_DISTANCE / T5_MAX_EXACT))
                            * np.float32(T5_NUM_BUCKETS - T5_MAX_EXACT)).astype(np.int32)
    large = np.minimum(large, T5_NUM_BUCKETS - 1)
    return np.where(n < T5_MAX_EXACT, n, large)


def _moba_bias_tables(rel_bias):
    a = np.arange(MB_BLOCK)
    rel_own = a[:, None] - a[None, :]
    table = rel_bias.astype(F32).T

    def expand(rel):
        bucket = jnp.asarray(_t5_bucket_np(rel).reshape(-1).astype(np.int32))
        onehot = (bucket[None, :] == jnp.arange(T5_NUM_BUCKETS)[:, None]).astype(F32)
        return jnp.dot(table, onehot, precision=lax.Precision.HIGHEST).reshape(
            MB_HEADS, MB_BLOCK, MB_BLOCK)

    b_own = jnp.where(jnp.asarray(rel_own >= 0)[None], expand(rel_own), NEG_INF)
    return b_own, expand(rel_own + MB_BLOCK), table[:, T5_NUM_BUCKETS - 1]


def _chunk_tables(cnt, start):
    nch = (cnt + MB_BLOCK - 1) // MB_BLOCK
    cum = jnp.cumsum(nch, axis=1)
    i = jnp.arange(MAX_CHUNKS)
    blk = jnp.sum(cum[:, None, :] <= i[None, :, None], axis=2)
    blk = jnp.minimum(blk, N_KBLOCKS - 1)
    onehot = blk[:, :, None] == jnp.arange(N_KBLOCKS)[None, None, :]

    def pick(x):
        return jnp.sum(jnp.where(onehot, x[:, None, :], 0), axis=2)

    def flat(x):
        return x.reshape(-1).astype(jnp.int32)
    within = i[None, :] - pick(cum - nch)
    off = pick(start) + within * MB_BLOCK
    valid = jnp.clip(pick(cnt) - within * MB_BLOCK, 0, MB_BLOCK)
    return flat(off), flat(blk), flat(valid), flat(cum[:, -1])


def _moba(xs, vec, w_qkv, w_o, rel_bias):
    qf, k, v, selt, prevt, cnt = _qkv(xs, vec, w_qkv)
    cnt = cnt[:, 0].astype(jnp.int32).reshape(MB_HEADS, N_KBLOCKS)
    padded = (cnt + ROUTE_ALIGN - 1) // ROUTE_ALIGN * ROUTE_ALIGN
    start = jnp.cumsum(padded, axis=1) - padded
    base = jnp.broadcast_to(start.reshape(-1, 1).astype(F32), (MB_HEADS * N_KBLOCKS, LANES))
    pos = _route(selt, base).reshape(MB_HEADS, MB_TOPK, NSTEPS, STEP_ROWS)
    pos = pos.transpose(0, 2, 1, 3).reshape(-1)
    b_own, b_prev, c31 = _moba_bias_tables(rel_bias)
    a = _moba_attention(_chunk_tables(cnt, start), c31, qf, pos, k, v, prevt.T, b_own, b_prev)
    return _out_proj(xs, a, vec, w_o)


def _vec(norm_g, mod3):
    return jnp.concatenate([norm_g[None, :], mod3, jnp.zeros((4, D_MODEL), F32)], axis=0)


def kernel(x, c, rel_bias, mod_w, mod_b, norm_g, ffn_w_in, ffn_w_out, gmlp_w_in, gmlp_v_norm,
           gmlp_w_s, gmlp_b_s, gmlp_w_out, moba_w_qkv, moba_w_o, final_norm):
    assert x.shape == (1, SEQ, D_MODEL)
    xs = x.reshape(SEQ, D_MODEL)
    mod = _modulation(c.reshape(D_MODEL, 1), mod_w, mod_b).reshape(DEPTH, N_SUBLAYERS, 3, D_MODEL)
    fin = final_norm.reshape(1, D_MODEL)

    for i in range(DEPTH):
        li = i // 2
        xs = _ffn(xs, _vec(norm_g[i, 0], mod[i, 0]), ffn_w_in[i, 0].astype(BF16),
                  ffn_w_out[i, 0].astype(BF16), fin, False)
        vec = _vec(norm_g[i, 1], mod[i, 1])
        if i % 2 == 0:
            bias = jnp.repeat(gmlp_b_s[li].T, GM_GROUP_DIM, axis=1)
            xs = _gmlp(xs, vec, gmlp_w_in[li].astype(BF16), gmlp_v_norm[li].reshape(1, GM_D),
                       gmlp_w_s[li].astype(BF16), bias, gmlp_w_out[li].astype(BF16))
        else:
            xs = _moba(xs, vec, moba_w_qkv[li].astype(BF16), moba_w_o[li].astype(BF16), rel_bias)
        xs = _ffn(xs, _vec(norm_g[i, 2], mod[i, 2]), ffn_w_in[i, 1].astype(BF16),
                  ffn_w_out[i, 1].astype(BF16), fin, i == DEPTH - 1)
    return xs.reshape(1, SEQ, D_MODEL)
```

```python
import functools
import math

import numpy as np
import jax
import jax.numpy as jnp
from jax import lax
from jax.experimental import pallas as pl
from jax.experimental.pallas import tpu as pltpu

F32 = jnp.float32
BF16 = jnp.bfloat16

D_MODEL = 1024
SEQ = 16384
DEPTH = 2
N_SUBLAYERS = 3
RMS_EPS = 1e-6
D_FF = 2816

GM_D = 3 * D_MODEL
GM_GROUPS = 16
GM_GROUP_DIM = GM_D // GM_GROUPS
GM_CHUNK = 128
GM_PAIR = 2 * GM_GROUP_DIM

MB_HEADS = 16
MB_HEAD_DIM = 64
MB_BLOCK = 256
MB_TOPK = 3
MB_SCALE = MB_HEAD_DIM ** -0.5
N_KBLOCKS = SEQ // MB_BLOCK

T5_NUM_BUCKETS = 32
T5_MAX_EXACT = 16
T5_MAX_DISTANCE = 128
NEG_INF = -1e30
BELOW_NEG_INF = -3e38

LANES = 128
VMEM_LIMIT = 56 * 1024 * 1024

MOD_TN = 1536
FFN_TM = 512
FFN_CHUNK = 1408
GM_TM = 256
GM_VCHUNK = 768
QKV_TM = 512
PROJ_TM = 512

ROUTE_T = 512
ROUTE_ALIGN = 8
STEP_ROWS = 1024
NSTEPS = SEQ // STEP_ROWS
BUF_DATA_ROWS = MB_TOPK * SEQ + N_KBLOCKS * ROUTE_ALIGN
BUF_ROWS = BUF_DATA_ROWS + 3 * MB_BLOCK
NULL_ROW = BUF_ROWS - ROUTE_ALIGN
ROUTE_NONE = 1e9
MAX_CHUNKS = MB_TOPK * SEQ // MB_BLOCK + N_KBLOCKS
POS_BLOCK = MB_TOPK * STEP_ROWS
CHUNKS_PER_BODY = 8


def _resident(shape):
    nd = len(shape)
    return pl.BlockSpec(shape, lambda *_: (0,) * nd, pipeline_mode=pl.Buffered(1))


def _params(*sem):
    return pltpu.CompilerParams(dimension_semantics=sem, vmem_limit_bytes=VMEM_LIMIT)


def _rms(x):
    return x * lax.rsqrt(jnp.mean(x * x, axis=-1, keepdims=True) + RMS_EPS)


def _adaln(x, vec_ref):
    y = _rms(x) * vec_ref[0:1, :]
    return y * (1.0 + vec_ref[2:3, :]) + vec_ref[1:2, :]


def _gelu(x):
    return 0.5 * x * (1.0 + lax.erf(x * np.float32(math.sqrt(0.5))))


def _mod_kernel(c_ref, w_ref, b_ref, o_ref):
    c = c_ref[...]
    c_act = c * jax.nn.sigmoid(c)
    o_ref[...] = jnp.sum(c_act * w_ref[...], axis=0, keepdims=True) + b_ref[...]


def _modulation(c_col, mod_w, mod_b):
    n = mod_w.shape[-1]
    return pl.pallas_call(
        _mod_kernel,
        grid=(DEPTH, n // MOD_TN),
        in_specs=[
            pl.BlockSpec((D_MODEL, 1), lambda i, j: (0, 0)),
            pl.BlockSpec((None, D_MODEL, MOD_TN), lambda i, j: (i, 0, j)),
            pl.BlockSpec((None, 1, MOD_TN), lambda i, j: (i, 0, j)),
        ],
        out_specs=pl.BlockSpec((None, 1, MOD_TN), lambda i, j: (i, 0, j)),
        out_shape=jax.ShapeDtypeStruct((DEPTH, 1, n), F32),
        compiler_params=_params("arbitrary", "arbitrary"),
        name="modulation",
    )(c_col, mod_w, mod_b.reshape(DEPTH, 1, n))


def _ffn_kernel(x_ref, vec_ref, win_ref, wout_ref, fin_ref, o_ref, act_ref, *, final):
    x = x_ref[...]
    h = _adaln(x, vec_ref).astype(BF16)
    for c in range(D_FF // FFN_CHUNK):
        lo = c * FFN_CHUNK
        g = jnp.dot(h, win_ref[:, lo:lo + FFN_CHUNK], preferred_element_type=F32)
        u = jnp.dot(h, win_ref[:, D_FF + lo:D_FF + lo + FFN_CHUNK], preferred_element_type=F32)
        act_ref[:, lo:lo + FFN_CHUNK] = (g * jax.nn.sigmoid(g) * u).astype(BF16)
    y = jnp.dot(act_ref[...], wout_ref[...], preferred_element_type=F32)
    out = x + (0.5 * vec_ref[3:4, :]) * y
    if final:
        out = _rms(out) * fin_ref[...]
    o_ref[...] = out


def _ffn(x, vec, w_in, w_out, fin, final):
    return pl.pallas_call(
        functools.partial(_ffn_kernel, final=final),
        grid=(SEQ // FFN_TM,),
        in_specs=[
            pl.BlockSpec((FFN_TM, D_MODEL), lambda i: (i, 0)),
            _resident((8, D_MODEL)),
            _resident((D_MODEL, 2 * D_FF)),
            _resident((D_FF, D_MODEL)),
            _resident((1, D_MODEL)),
        ],
        out_specs=pl.BlockSpec((FFN_TM, D_MODEL), lambda i: (i, 0)),
        out_shape=jax.ShapeDtypeStruct((SEQ, D_MODEL), F32),
        scratch_shapes=[pltpu.VMEM((FFN_TM, D_FF), BF16)],
        compiler_params=_params("parallel"),
        name="ffn_final" if final else "ffn",
    )(x, vec, w_in, w_out, fin)


def _gmlp_kernel(x_ref, vec_ref, win_ref, vnorm_ref, ws_ref, bias_ref, wout_ref,
                 o_ref, v_ref, g_ref):
    x = x_ref[...]
    h = _adaln(x, vec_ref).astype(BF16)

    ssq = jnp.zeros((GM_TM, 1), F32)
    for c in range(GM_D // GM_VCHUNK):
        lo = c * GM_VCHUNK
        v = _gelu(jnp.dot(h, win_ref[:, GM_D + lo:GM_D + lo + GM_VCHUNK],
                          preferred_element_type=F32))
        v_ref[:, lo:lo + GM_VCHUNK] = v
        ssq = ssq + jnp.sum(v * v, axis=-1, keepdims=True)
    r = lax.rsqrt(ssq * (1.0 / GM_D) + RMS_EPS)

    row = lax.broadcasted_iota(jnp.int32, (GM_CHUNK, GM_CHUNK), 0)
    col = lax.broadcasted_iota(jnp.int32, (GM_CHUNK, GM_CHUNK), 1)
    causal = row >= col
    low_half = lax.broadcasted_iota(jnp.int32, (GM_CHUNK, LANES), 1) < (GM_GROUP_DIM - LANES)

    for p in range(GM_GROUPS // 2):
        lo = p * GM_PAIR
        vn = (v_ref[:, lo:lo + GM_PAIR] * r * vnorm_ref[:, lo:lo + GM_PAIR]).astype(BF16)
        u = _gelu(jnp.dot(h, win_ref[:, lo:lo + GM_PAIR], preferred_element_type=F32))
        w0 = jnp.where(causal, ws_ref[2 * p], jnp.zeros((), BF16))
        w1 = jnp.where(causal, ws_ref[2 * p + 1], jnp.zeros((), BF16))
        for c in range(GM_TM // GM_CHUNK):
            rows = slice(c * GM_CHUNK, (c + 1) * GM_CHUNK)
            vc = vn[rows]
            a = jnp.dot(w0, vc[:, :2 * LANES], preferred_element_type=F32)
            b = jnp.dot(w1, vc[:, LANES:], preferred_element_type=F32)
            mid = jnp.where(low_half, a[:, LANES:], b[:, :LANES])
            sv = jnp.concatenate([a[:, :LANES], mid, b[:, LANES:]], axis=1)
            sv = sv + bias_ref[:, lo:lo + GM_PAIR]
            g_ref[rows, lo:lo + GM_PAIR] = (u[rows] * sv).astype(BF16)

    y = jnp.dot(g_ref[...], wout_ref[...], preferred_element_type=F32)
    o_ref[...] = x + vec_ref[3:4, :] * y


def _gmlp(x, vec, w_in, v_norm, w_s, bias, w_out):
    return pl.pallas_call(
        _gmlp_kernel,
        grid=(SEQ // GM_TM,),
        in_specs=[
            pl.BlockSpec((GM_TM, D_MODEL), lambda i: (i, 0)),
            _resident((8, D_MODEL)),
            _resident((D_MODEL, 2 * GM_D)),
            _resident((1, GM_D)),
            _resident((GM_GROUPS, GM_CHUNK, GM_CHUNK)),
            _resident((GM_CHUNK, GM_D)),
            _resident((GM_D, D_MODEL)),
        ],
        out_specs=pl.BlockSpec((GM_TM, D_MODEL), lambda i: (i, 0)),
        out_shape=jax.ShapeDtypeStruct((SEQ, D_MODEL), F32),
        scratch_shapes=[pltpu.VMEM((GM_TM, GM_D), F32), pltpu.VMEM((GM_TM, GM_D), BF16)],
        compiler_params=_params("parallel"),
        name="gmlp",
    )(x, vec, w_in, v_norm, w_s, bias, w_out)


def _qkv_kernel(x_ref, vec_ref, w_ref, q_ref, k_ref, v_ref, selt_ref, prevt_ref, cnt_ref,
                kmean_ref):
    i = pl.program_id(0)

    @pl.when(i == 0)
    def _init():
        kmean_ref[...] = jnp.zeros_like(kmean_ref)
        cnt_ref[...] = jnp.zeros_like(cnt_ref)

    h = _adaln(x_ref[...], vec_ref).astype(BF16)
    q = jnp.dot(h, w_ref[:, :D_MODEL], preferred_element_type=F32) * MB_SCALE
    q_ref[...] = q
    k = jnp.dot(h, w_ref[:, D_MODEL:2 * D_MODEL], preferred_element_type=F32)
    k_ref[...] = k.astype(BF16)
    nb = QKV_TM // MB_BLOCK
    for b in range(nb):
        kmean_ref[pl.ds(nb * i + b, 1), :] = jnp.mean(
            k[b * MB_BLOCK:(b + 1) * MB_BLOCK], axis=0, keepdims=True)
    v = jnp.dot(h, w_ref[:, 2 * D_MODEL:], preferred_element_type=F32)
    v_ref[...] = v.astype(BF16)

    lane = lax.broadcasted_iota(jnp.int32, (1, LANES), 1)
    blk = lax.broadcasted_iota(jnp.int32, (N_KBLOCKS, MB_BLOCK), 0)
    contract_last = (((1,), (1,)), ((), ()))
    for b in range(nb):
        j = nb * i + b
        cols = slice(b * MB_BLOCK, (b + 1) * MB_BLOCK)
        for pair in range(MB_HEADS // 2):
            km = kmean_ref[:, pair * LANES:(pair + 1) * LANES]
            qp = q[cols, pair * LANES:(pair + 1) * LANES]
            for hh in range(2):
                head = 2 * pair + hh
                km_h = jnp.where((lane // MB_HEAD_DIM) == hh, km, 0.0)
                g = lax.dot_general(km_h, qp, contract_last, preferred_element_type=F32)
                g = jnp.where(blk < j, g, NEG_INF)
                sel = jnp.zeros((N_KBLOCKS, MB_BLOCK), F32)
                for r in range(MB_TOPK):
                    m = jnp.max(g, axis=0, keepdims=True)
                    first = jnp.min(jnp.where(g == m, blk, N_KBLOCKS), axis=0, keepdims=True)
                    hit = blk == first
                    sel = jnp.where(jnp.logical_and(hit, r < j), 1.0, sel)
                    g = jnp.where(hit, BELOW_NEG_INF, g)
                prevt_ref[head:head + 1, cols] = jnp.max(
                    jnp.where(blk == j - 1, sel, 0.0), axis=0, keepdims=True)
                routed = jnp.where(blk < j - 1, sel, 0.0)
                selt_ref[head * N_KBLOCKS:(head + 1) * N_KBLOCKS, cols] = routed.astype(BF16)
                rows = slice(head * N_KBLOCKS, (head + 1) * N_KBLOCKS)
                cnt_ref[rows, :] = cnt_ref[rows, :] + jnp.sum(routed, axis=1, keepdims=True)


def _qkv(x, vec, w_qkv):
    row = pl.BlockSpec((QKV_TM, D_MODEL), lambda i: (i, 0))
    n_items = MB_HEADS * N_KBLOCKS
    return pl.pallas_call(
        _qkv_kernel,
        grid=(SEQ // QKV_TM,),
        in_specs=[row, _resident((8, D_MODEL)), _resident((D_MODEL, 3 * D_MODEL))],
        out_specs=[row, row, row,
                   pl.BlockSpec((n_items, QKV_TM), lambda i: (0, i)),
                   pl.BlockSpec((MB_HEADS, QKV_TM), lambda i: (0, i)),
                   pl.BlockSpec((n_items, LANES), lambda i: (0, 0))],
        out_shape=[jax.ShapeDtypeStruct((SEQ, D_MODEL), F32),
                   jax.ShapeDtypeStruct((SEQ, D_MODEL), BF16),
                   jax.ShapeDtypeStruct((SEQ, D_MODEL), BF16),
                   jax.ShapeDtypeStruct((n_items, SEQ), BF16),
                   jax.ShapeDtypeStruct((MB_HEADS, SEQ), F32),
                   jax.ShapeDtypeStruct((n_items, LANES), F32)],
        scratch_shapes=[pltpu.VMEM((N_KBLOCKS, D_MODEL), F32)],
        compiler_params=_params("arbitrary"),
        name="moba_qkv_gate",
    )(x, vec, w_qkv)


def _route_kernel(selt_ref, base_ref, pos_ref, carry_ref):
    i = pl.program_id(0)

    @pl.when(i == 0)
    def _init():
        carry_ref[...] = jnp.zeros_like(carry_ref)

    sel = selt_ref[...]
    before = (lax.broadcasted_iota(jnp.int32, (ROUTE_T, ROUTE_T), 0)
              < lax.broadcasted_iota(jnp.int32, (ROUTE_T, ROUTE_T), 1))
    rank = jnp.dot(sel, jnp.where(before, 1.0, 0.0).astype(BF16), preferred_element_type=F32)
    offset = base_ref[...] + carry_ref[...]
    rank = rank + jnp.concatenate([offset] * (ROUTE_T // LANES), axis=1)
    carry_ref[...] = carry_ref[...] + jnp.sum(sel.astype(F32), axis=1, keepdims=True)
    val = jnp.where(sel > 0, rank, ROUTE_NONE)
    for head in range(MB_HEADS):
        vh = val[head * N_KBLOCKS:(head + 1) * N_KBLOCKS]
        for r in range(MB_TOPK):
            m = jnp.min(vh, axis=0, keepdims=True)
            slot = head * MB_TOPK + r
            pos_ref[slot:slot + 1, :] = jnp.where(m < ROUTE_NONE, m, float(NULL_ROW)).astype(jnp.int32)
            vh = jnp.where(vh == m, ROUTE_NONE, vh)


def _route(selt, base):
    n_items = MB_HEADS * N_KBLOCKS
    return pl.pallas_call(
        _route_kernel,
        grid=(SEQ // ROUTE_T,),
        in_specs=[pl.BlockSpec((n_items, ROUTE_T), lambda i: (0, i)),
                  pl.BlockSpec((n_items, LANES), lambda i: (0, 0))],
        out_specs=pl.BlockSpec((MB_HEADS * MB_TOPK, ROUTE_T), lambda i: (0, i)),
        out_shape=jax.ShapeDtypeStruct((MB_HEADS * MB_TOPK, SEQ), jnp.int32),
        scratch_shapes=[pltpu.VMEM((n_items, LANES), F32)],
        compiler_params=_params("arbitrary"),
        name="moba_route",
    )(selt, base)


def _moba_kernel(choff_ref, chblk_ref, chvalid_ref, nch_ref, c31_ref,
                 qf_ref, pos_hbm, k_ref, v_ref, prev_ref, bown_ref, bprev_ref,
                 o_ref, buf_ref, stage_ref, hold_ref, s_ref, p_ref, m_ref,
                 pos_even_ref, pos_odd_ref, pos_sem):
    head = pl.program_id(0)
    t = pl.program_id(1)
    hh = head % 2
    lane = lax.broadcasted_iota(jnp.int32, (1, LANES), 1)
    in_head = (lane // MB_HEAD_DIM) == hh
    stat0 = (1 - hh) * MB_HEAD_DIM
    is_m_lane = lane == stat0
    is_l_lane = lane == stat0 + 1
    contract_last = (((1,), (1,)), ((), ()))
    ones_bf = jnp.ones((), BF16)

    def queries(rows):
        return jnp.where(in_head, rows, 0.0).astype(BF16)

    def values(rows):
        return jnp.where(in_head, rows, ones_bf)

    @pl.when(jnp.logical_and(t == 0, head == 0))
    def _init():
        def zero(c, _):
            buf_ref[pl.ds(pl.multiple_of(c * MB_BLOCK, MB_BLOCK), MB_BLOCK), :] = (
                jnp.zeros((MB_BLOCK, LANES), F32))
            return 0
        lax.fori_loop(0, BUF_ROWS // MB_BLOCK, zero, 0)

        def zero_hold(c, _):
            hold_ref[pl.ds(pl.multiple_of(c * MB_BLOCK, MB_BLOCK), MB_BLOCK), :] = (
                jnp.zeros((MB_BLOCK, LANES), BF16))
            return 0
        lax.fori_loop(0, SEQ // MB_BLOCK, zero_hold, 0)

    def dispatch(pos_ref):
        for a in range(STEP_ROWS):
            row = qf_ref[a:a + 1, :]
            for r in range(MB_TOPK):
                buf_ref[pl.ds(pos_ref[r * STEP_ROWS + a], 1), :] = row

    @pl.when(t == NSTEPS)
    def _routed():
        buf_ref[NULL_ROW:NULL_ROW + ROUTE_ALIGN, :] = jnp.where(
            is_m_lane, NEG_INF, jnp.zeros((ROUTE_ALIGN, LANES), F32))
        n = nch_ref[head]
        far_bias = c31_ref[head]
        row_id = lax.broadcasted_iota(jnp.int32, (MB_BLOCK, LANES), 0)

        def meta(i):
            idx = head * MAX_CHUNKS + jnp.clip(i, 0, n - 1)
            off = pl.multiple_of(choff_ref[idx], ROUTE_ALIGN)
            blk = pl.multiple_of(chblk_ref[idx] * MB_BLOCK, MB_BLOCK)
            return off, blk, jnp.where(jnp.logical_and(i >= 0, i < n), chvalid_ref[idx], 0)

        def chunk_queries(off):
            return queries(buf_ref[pl.ds(off, MB_BLOCK), :])

        def scores(qh, blk):
            return lax.dot_general(qh, k_ref[pl.ds(blk, MB_BLOCK), :], contract_last,
                                   preferred_element_type=F32)

        def probs(u):
            s = s_ref[u]
            m = jnp.max(s, axis=1, keepdims=True)
            p_ref[u] = jnp.exp(s - m).astype(BF16)
            m_ref[u] = jnp.broadcast_to(m, (MB_BLOCK, LANES))

        def finish(u, off, blk, valid):
            acc = jnp.dot(p_ref[u], values(v_ref[pl.ds(blk, MB_BLOCK), :]), preferred_element_type=F32)
            new = jnp.where(is_m_lane, m_ref[u] + far_bias, acc)
            pltpu.store(buf_ref.at[pl.ds(off, MB_BLOCK), :], new, mask=row_id < valid)

        @pl.when(n > 0)
        def _chunks():
            slots = range(CHUNKS_PER_BODY)
            p_ref[...] = jnp.zeros_like(p_ref)
            m_ref[...] = jnp.zeros_like(m_ref)
            for u in slots:
                off, blk, _ = meta(u)
                s_ref[u] = scores(chunk_queries(off), blk)

            def chunk_group(c, _):
                i = CHUNKS_PER_BODY * c
                nxt = [meta(i + CHUNKS_PER_BODY + u) for u in slots]
                done = [meta(i - CHUNKS_PER_BODY + u) for u in slots]
                rows = [chunk_queries(off) for off, _, _ in nxt]
                for u in slots:
                    finish(u, *done[u])
                for u in slots:
                    probs(u)
                for u in slots:
                    s_ref[u] = scores(rows[u], nxt[u][1])
                return 0
            lax.fori_loop(0, (n + CHUNKS_PER_BODY - 1) // CHUNKS_PER_BODY + 1, chunk_group, 0)

    def combine(pos_ref):
        step = t - (NSTEPS + 1)

        n_sub = STEP_ROWS // MB_BLOCK
        for jj in range(n_sub):
            j = step * n_sub + jj
            rs = slice(jj * MB_BLOCK, (jj + 1) * MB_BLOCK)
            win = pl.ds(pl.multiple_of(jnp.maximum(j - 1, 0) * MB_BLOCK, MB_BLOCK), 2 * MB_BLOCK)
            s = lax.dot_general(queries(qf_ref[rs, :]), k_ref[win, :], contract_last,
                                preferred_element_type=F32)
            picked = jnp.sum(jnp.where(lane == head, prev_ref[rs, :], 0.0),
                             axis=1, keepdims=True) > 0.0
            bias_lo, bias_hi = bprev_ref[...], bown_ref[...]
            if jj == 0:
                first = step == 0
                bias_lo = jnp.where(first, bown_ref[...], bias_lo)
                bias_hi = jnp.where(first, NEG_INF, bias_hi)
                picked = jnp.logical_or(picked, first)
            s_lo = jnp.where(picked, s[:, :MB_BLOCK] + bias_lo, NEG_INF)
            s_hi = s[:, MB_BLOCK:] + bias_hi
            m_d = jnp.maximum(jnp.max(s_lo, axis=1, keepdims=True), jnp.max(s_hi, axis=1, keepdims=True))
            p = jnp.concatenate([jnp.exp(s_lo - m_d), jnp.exp(s_hi - m_d)], axis=1).astype(BF16)
            acc = jnp.dot(p, values(v_ref[win, :]), preferred_element_type=F32)

            for a in range(jj * MB_BLOCK, (jj + 1) * MB_BLOCK):
                for r in range(MB_TOPK):
                    stage_ref[r, a:a + 1, :] = buf_ref[pl.ds(pos_ref[r * STEP_ROWS + a], 1), :]

            parts = [stage_ref[r, rs, :] for r in range(MB_TOPK)]
            ms = [jnp.sum(jnp.where(is_m_lane, part, 0.0), axis=1, keepdims=True) for part in parts]
            m_all = m_d
            for m_r in ms:
                m_all = jnp.maximum(m_all, m_r)
            total = jnp.exp(m_d - m_all) * acc
            for m_r, part in zip(ms, parts):
                total = total + jnp.exp(m_r - m_all) * part
            den = jnp.sum(jnp.where(is_l_lane, total, 0.0), axis=1, keepdims=True)
            out = (total / den).astype(BF16)
            hold_rows = pl.ds(pl.multiple_of(j * MB_BLOCK, MB_BLOCK), MB_BLOCK)
            take_parked = jnp.logical_and(lane < MB_HEAD_DIM, hh == 1)
            merged = jnp.where(take_parked, hold_ref[hold_rows, :], out)
            hold_ref[hold_rows, :] = merged
            o_ref[rs, :] = merged

    steps_per_head = 2 * NSTEPS
    use = head * steps_per_head + jnp.where(t < NSTEPS, t, t - 1)
    pos_bufs = (pos_even_ref, pos_odd_ref)

    def pos_copy(u, parity):
        block = (u // steps_per_head) * NSTEPS + u % NSTEPS
        src = pos_hbm.at[pl.ds(pl.multiple_of(block * POS_BLOCK, POS_BLOCK), POS_BLOCK)]
        return pltpu.make_async_copy(src, pos_bufs[parity], pos_sem.at[parity])

    for parity in range(2):
        @pl.when(jnp.logical_and(t != NSTEPS, use % 2 == parity))
        def _positions_step(parity=parity):
            @pl.when(use == 0)
            def _first():
                pos_copy(use, parity).start()

            @pl.when(use + 1 < MB_HEADS * steps_per_head)
            def _prefetch():
                pos_copy(use + 1, 1 - parity).start()

            pos_copy(use, parity).wait()

            @pl.when(t < NSTEPS)
            def _dispatch():
                dispatch(pos_bufs[parity])

            @pl.when(t > NSTEPS)
            def _combine():
                combine(pos_bufs[parity])


def _moba_attention(chunk_meta, c31, qf, pos, k, v, prevc, bias_own, bias_prev):
    def seq_step(t):
        return jnp.where(t < NSTEPS, t, jnp.maximum(t - (NSTEPS + 1), 0) + jnp.where(t == NSTEPS, NSTEPS - 1, 0))

    def combine_step(t):
        return jnp.maximum(t - (NSTEPS + 1), 0)

    pair_resident = pl.BlockSpec((SEQ, LANES), lambda h, t, *_: (0, h // 2), pipeline_mode=pl.Buffered(1))
    bias = pl.BlockSpec((None, MB_BLOCK, MB_BLOCK), lambda h, t, *_: (h, 0, 0))
    return pl.pallas_call(
        _moba_kernel,
        grid_spec=pltpu.PrefetchScalarGridSpec(
            num_scalar_prefetch=5,
            grid=(MB_HEADS, 2 * NSTEPS + 1),
            in_specs=[
                pl.BlockSpec((STEP_ROWS, LANES), lambda h, t, *_: (seq_step(t), h // 2)),
                pl.BlockSpec(memory_space=pl.ANY),
                pair_resident, pair_resident,
                pl.BlockSpec((STEP_ROWS, LANES), lambda h, t, *_: (combine_step(t), 0)),
                bias, bias,
            ],
            out_specs=pl.BlockSpec(
                (STEP_ROWS, LANES),
                lambda h, t, *_: (jnp.where(h % 2 == 1, combine_step(t), 0), h // 2)),
            scratch_shapes=[pltpu.VMEM((BUF_ROWS, LANES), F32),
                            pltpu.VMEM((MB_TOPK, STEP_ROWS, LANES), F32),
                            pltpu.VMEM((SEQ, LANES), BF16),
                            pltpu.VMEM((CHUNKS_PER_BODY, MB_BLOCK, MB_BLOCK), F32),
                            pltpu.VMEM((CHUNKS_PER_BODY, MB_BLOCK, MB_BLOCK), BF16),
                            pltpu.VMEM((CHUNKS_PER_BODY, MB_BLOCK, LANES), F32),
                            pltpu.SMEM((POS_BLOCK,), jnp.int32), pltpu.SMEM((POS_BLOCK,), jnp.int32),
                            pltpu.SemaphoreType.DMA((2,))],
        ),
        out_shape=jax.ShapeDtypeStruct((SEQ, D_MODEL), BF16),
        compiler_params=_params("arbitrary", "arbitrary"),
        name="moba_attention",
    )(*chunk_meta, c31, qf, pos, k, v, prevc, bias_own, bias_prev)


def _proj_kernel(x_ref, a_ref, vec_ref, w_ref, o_ref):
    y = jnp.dot(a_ref[...], w_ref[...], preferred_element_type=F32)
    o_ref[...] = x_ref[...] + vec_ref[3:4, :] * y


def _out_proj(x, a, vec, w_o):
    row = pl.BlockSpec((PROJ_TM, D_MODEL), lambda i: (i, 0))
    return pl.pallas_call(
        _proj_kernel,
        grid=(SEQ // PROJ_TM,),
        in_specs=[row, row, _resident((8, D_MODEL)), _resident((D_MODEL, D_MODEL))],
        out_specs=row,
        out_shape=jax.ShapeDtypeStruct((SEQ, D_MODEL), F32),
        compiler_params=_params("parallel"),
        name="moba_out_proj",
    )(x, a, vec, w_o)


def _t5_bucket_np(rel):
    n = np.maximum(rel, 0)
    nf = np.maximum(n, T5_MAX_EXACT).astype(np.float32)
    large = T5_MAX_EXACT + (np.log(nf / np.float32(T5_MAX_EXACT))
                            / np.float32(math.log(T5_MAX_DISTANCE / T5_MAX_EXACT))
                            * np.float32(T5_NUM_BUCKETS - T5_MAX_EXACT)).astype(np.int32)
    large = np.minimum(large, T5_NUM_BUCKETS - 1)
    return np.where(n < T5_MAX_EXACT, n, large)


def _moba_bias_tables(rel_bias):
    a = np.arange(MB_BLOCK)
    rel_own = a[:, None] - a[None, :]
    table = rel_bias.astype(F32).T

    def expand(rel):
        bucket = jnp.asarray(_t5_bucket_np(rel).reshape(-1).astype(np.int32))
        onehot = (bucket[None, :] == jnp.arange(T5_NUM_BUCKETS)[:, None]).astype(F32)
        return jnp.dot(table, onehot, precision=lax.Precision.HIGHEST).reshape(
            MB_HEADS, MB_BLOCK, MB_BLOCK)

    b_own = jnp.where(jnp.asarray(rel_own >= 0)[None], expand(rel_own), NEG_INF)
    return b_own, expand(rel_own + MB_BLOCK), table[:, T5_NUM_BUCKETS - 1]


def _chunk_tables(cnt, start):
    nch = (cnt + MB_BLOCK - 1) // MB_BLOCK
    cum = jnp.cumsum(nch, axis=1)
    i = jnp.arange(MAX_CHUNKS)
    blk = jnp.sum(cum[:, None, :] <= i[None, :, None], axis=2)
    blk = jnp.minimum(blk, N_KBLOCKS - 1)
    onehot = blk[:, :, None] == jnp.arange(N_KBLOCKS)[None, None, :]

    def pick(x):
        return jnp.sum(jnp.where(onehot, x[:, None, :], 0), axis=2)

    def flat(x):
        return x.reshape(-1).astype(jnp.int32)
    within = i[None, :] - pick(cum - nch)
    off = pick(start) + within * MB_BLOCK
    valid = jnp.clip(pick(cnt) - within * MB_BLOCK, 0, MB_BLOCK)
    return flat(off), flat(blk), flat(valid), flat(cum[:, -1])


def _moba(xs, vec, w_qkv, w_o, rel_bias):
    qf, k, v, selt, prevt, cnt = _qkv(xs, vec, w_qkv)
    cnt = cnt[:, 0].astype(jnp.int32).reshape(MB_HEADS, N_KBLOCKS)
    padded = (cnt + ROUTE_ALIGN - 1) // ROUTE_ALIGN * ROUTE_ALIGN
    start = jnp.cumsum(padded, axis=1) - padded
    base = jnp.broadcast_to(start.reshape(-1, 1).astype(F32), (MB_HEADS * N_KBLOCKS, LANES))
    pos = _route(selt, base).reshape(MB_HEADS, MB_TOPK, NSTEPS, STEP_ROWS)
    pos = pos.transpose(0, 2, 1, 3).reshape(-1)
    b_own, b_prev, c31 = _moba_bias_tables(rel_bias)
    prevc = jnp.pad(prevt.T, ((0, 0), (0, LANES - MB_HEADS)))
    a = _moba_attention(_chunk_tables(cnt, start), c31, qf, pos, k, v, prevc, b_own, b_prev)
    return _out_proj(xs, a, vec, w_o)


def _vec(norm_g, mod3):
    return jnp.concatenate([norm_g[None, :], mod3, jnp.zeros((4, D_MODEL), F32)], axis=0)


def kernel(x, c, rel_bias, mod_w, mod_b, norm_g, ffn_w_in, ffn_w_out, gmlp_w_in, gmlp_v_norm,
           gmlp_w_s, gmlp_b_s, gmlp_w_out, moba_w_qkv, moba_w_o, final_norm):
    assert x.shape == (1, SEQ, D_MODEL)
    xs = x.reshape(SEQ, D_MODEL)
    mod = _modulation(c.reshape(D_MODEL, 1), mod_w, mod_b).reshape(DEPTH, N_SUBLAYERS, 3, D_MODEL)
    fin = final_norm.reshape(1, D_MODEL)

    for i in range(DEPTH):
        li = i // 2
        xs = _ffn(xs, _vec(norm_g[i, 0], mod[i, 0]), ffn_w_in[i, 0].astype(BF16),
                  ffn_w_out[i, 0].astype(BF16), fin, False)
        vec = _vec(norm_g[i, 1], mod[i, 1])
        if i % 2 == 0:
            bias = jnp.repeat(gmlp_b_s[li].T, GM_GROUP_DIM, axis=1)
            xs = _gmlp(xs, vec, gmlp_w_in[li].astype(BF16), gmlp_v_norm[li].reshape(1, GM_D),
                       gmlp_w_s[li].astype(BF16), bias, gmlp_w_out[li].astype(BF16))
        else:
            xs = _moba(xs, vec, moba_w_qkv[li].astype(BF16), moba_w_o[li].astype(BF16), rel_bias)
        xs = _ffn(xs, _vec(norm_g[i, 2], mod[i, 2]), ffn_w_in[i, 1].astype(BF16),
                  ffn_w_out[i, 1].astype(BF16), fin, i == DEPTH - 1)
    return xs.reshape(1, SEQ, D_MODEL)
```

```python
import functools
import math

import numpy as np
import jax
import jax.numpy as jnp
from jax import lax
from jax.experimental import pallas as pl
from jax.experimental.pallas import tpu as pltpu

F32 = jnp.float32
BF16 = jnp.bfloat16

D_MODEL = 1024
SEQ = 16384
DEPTH = 2
N_SUBLAYERS = 3
RMS_EPS = 1e-6
D_FF = 2816

GM_D = 3 * D_MODEL
GM_GROUPS = 16
GM_GROUP_DIM = GM_D // GM_GROUPS
GM_CHUNK = 128
GM_PAIR = 2 * GM_GROUP_DIM

MB_HEADS = 16
MB_HEAD_DIM = 64
MB_BLOCK = 256
MB_TOPK = 3
MB_SCALE = MB_HEAD_DIM ** -0.5
N_KBLOCKS = SEQ // MB_BLOCK

T5_NUM_BUCKETS = 32
T5_MAX_EXACT = 16
T5_MAX_DISTANCE = 128
NEG_INF = -1e30
BELOW_NEG_INF = -3e38

LANES = 128
VMEM_LIMIT = 56 * 1024 * 1024

MOD_TN = 1536
FFN_TM = 512
FFN_CHUNK = 1408
GM_TM = 256
GM_VCHUNK = 768
QKV_TM = 512
PROJ_TM = 512

ROUTE_T = 512
ROUTE_ALIGN = 8
STEP_ROWS = 1024
NSTEPS = SEQ // STEP_ROWS
BUF_DATA_ROWS = MB_TOPK * SEQ + N_KBLOCKS * ROUTE_ALIGN
BUF_ROWS = BUF_DATA_ROWS + 3 * MB_BLOCK
NULL_ROW = BUF_ROWS - ROUTE_ALIGN
ROUTE_NONE = 1e9
MAX_CHUNKS = MB_TOPK * SEQ // MB_BLOCK + N_KBLOCKS
POS_BLOCK = MB_TOPK * STEP_ROWS
CHUNKS_PER_BODY = 8


def _resident(shape):
    nd = len(shape)
    return pl.BlockSpec(shape, lambda *_: (0,) * nd, pipeline_mode=pl.Buffered(1))


def _params(*sem):
    return pltpu.CompilerParams(dimension_semantics=sem, vmem_limit_bytes=VMEM_LIMIT)


def _rms(x):
    return x * lax.rsqrt(jnp.mean(x * x, axis=-1, keepdims=True) + RMS_EPS)


def _adaln(x, vec_ref):
    y = _rms(x) * vec_ref[0:1, :]
    return y * (1.0 + vec_ref[2:3, :]) + vec_ref[1:2, :]


def _gelu(x):
    return 0.5 * x * (1.0 + lax.erf(x * np.float32(math.sqrt(0.5))))


def _mod_kernel(c_ref, w_ref, b_ref, o_ref):
    c = c_ref[...]
    c_act = c * jax.nn.sigmoid(c)
    o_ref[...] = jnp.sum(c_act * w_ref[...], axis=0, keepdims=True) + b_ref[...]


def _modulation(c_col, mod_w, mod_b):
    n = mod_w.shape[-1]
    return pl.pallas_call(
        _mod_kernel,
        grid=(DEPTH, n // MOD_TN),
        in_specs=[
            pl.BlockSpec((D_MODEL, 1), lambda i, j: (0, 0)),
            pl.BlockSpec((None, D_MODEL, MOD_TN), lambda i, j: (i, 0, j)),
            pl.BlockSpec((None, 1, MOD_TN), lambda i, j: (i, 0, j)),
        ],
        out_specs=pl.BlockSpec((None, 1, MOD_TN), lambda i, j: (i, 0, j)),
        out_shape=jax.ShapeDtypeStruct((DEPTH, 1, n), F32),
        compiler_params=_params("arbitrary", "arbitrary"),
        name="modulation",
    )(c_col, mod_w, mod_b.reshape(DEPTH, 1, n))


def _ffn_kernel(x_ref, vec_ref, win_ref, wout_ref, fin_ref, o_ref, act_ref, *, final):
    x = x_ref[...]
    h = _adaln(x, vec_ref).astype(BF16)
    for c in range(D_FF // FFN_CHUNK):
        lo = c * FFN_CHUNK
        g = jnp.dot(h, win_ref[:, lo:lo + FFN_CHUNK], preferred_element_type=F32)
        u = jnp.dot(h, win_ref[:, D_FF + lo:D_FF + lo + FFN_CHUNK], preferred_element_type=F32)
        act_ref[:, lo:lo + FFN_CHUNK] = (g * jax.nn.sigmoid(g) * u).astype(BF16)
    y = jnp.dot(act_ref[...], wout_ref[...], preferred_element_type=F32)
    out = x + (0.5 * vec_ref[3:4, :]) * y
    if final:
        out = _rms(out) * fin_ref[...]
    o_ref[...] = out


def _ffn(x, vec, w_in_all, w_out_all, layer, which, fin, final):
    def one_matrix(rows, cols):
        return pl.BlockSpec((None, None, rows, cols), lambda i: (layer, which, 0, 0),
                            pipeline_mode=pl.Buffered(1))
    return pl.pallas_call(
        functools.partial(_ffn_kernel, final=final),
        grid=(SEQ // FFN_TM,),
        in_specs=[
            pl.BlockSpec((FFN_TM, D_MODEL), lambda i: (i, 0)),
            _resident((8, D_MODEL)),
            one_matrix(D_MODEL, 2 * D_FF),
            one_matrix(D_FF, D_MODEL),
            _resident((1, D_MODEL)),
        ],
        out_specs=pl.BlockSpec((FFN_TM, D_MODEL), lambda i: (i, 0)),
        out_shape=jax.ShapeDtypeStruct((SEQ, D_MODEL), F32),
        scratch_shapes=[pltpu.VMEM((FFN_TM, D_FF), BF16)],
        compiler_params=_params("parallel"),
        name="ffn_final" if final else "ffn",
    )(x, vec, w_in_all, w_out_all, fin)


def _gmlp_kernel(x_ref, vec_ref, win_ref, vnorm_ref, ws_ref, bias_ref, wout_ref,
                 o_ref, v_ref, g_ref):
    x = x_ref[...]
    h = _adaln(x, vec_ref).astype(BF16)

    ssq = jnp.zeros((GM_TM, 1), F32)
    for c in range(GM_D // GM_VCHUNK):
        lo = c * GM_VCHUNK
        v = _gelu(jnp.dot(h, win_ref[:, GM_D + lo:GM_D + lo + GM_VCHUNK],
                          preferred_element_type=F32))
        v_ref[:, lo:lo + GM_VCHUNK] = v
        ssq = ssq + jnp.sum(v * v, axis=-1, keepdims=True)
    r = lax.rsqrt(ssq * (1.0 / GM_D) + RMS_EPS)

    row = lax.broadcasted_iota(jnp.int32, (GM_CHUNK, GM_CHUNK), 0)
    col = lax.broadcasted_iota(jnp.int32, (GM_CHUNK, GM_CHUNK), 1)
    causal = row >= col
    low_half = lax.broadcasted_iota(jnp.int32, (GM_CHUNK, LANES), 1) < (GM_GROUP_DIM - LANES)

    for p in range(GM_GROUPS // 2):
        lo = p * GM_PAIR
        vn = (v_ref[:, lo:lo + GM_PAIR] * r * vnorm_ref[:, lo:lo + GM_PAIR]).astype(BF16)
        u = _gelu(jnp.dot(h, win_ref[:, lo:lo + GM_PAIR], preferred_element_type=F32))
        w0 = jnp.where(causal, ws_ref[2 * p], jnp.zeros((), BF16))
        w1 = jnp.where(causal, ws_ref[2 * p + 1], jnp.zeros((), BF16))
        for c in range(GM_TM // GM_CHUNK):
            rows = slice(c * GM_CHUNK, (c + 1) * GM_CHUNK)
            vc = vn[rows]
            a = jnp.dot(w0, vc[:, :2 * LANES], preferred_element_type=F32)
            b = jnp.dot(w1, vc[:, LANES:], preferred_element_type=F32)
            mid = jnp.where(low_half, a[:, LANES:], b[:, :LANES])
            sv = jnp.concatenate([a[:, :LANES], mid, b[:, LANES:]], axis=1)
            sv = sv + bias_ref[:, lo:lo + GM_PAIR]
            g_ref[rows, lo:lo + GM_PAIR] = (u[rows] * sv).astype(BF16)

    y = jnp.dot(g_ref[...], wout_ref[...], preferred_element_type=F32)
    o_ref[...] = x + vec_ref[3:4, :] * y


def _gmlp(x, vec, w_in, v_norm, w_s, bias, w_out):
    return pl.pallas_call(
        _gmlp_kernel,
        grid=(SEQ // GM_TM,),
        in_specs=[
            pl.BlockSpec((GM_TM, D_MODEL), lambda i: (i, 0)),
            _resident((8, D_MODEL)),
            _resident((D_MODEL, 2 * GM_D)),
            _resident((1, GM_D)),
            _resident((GM_GROUPS, GM_CHUNK, GM_CHUNK)),
            _resident((GM_CHUNK, GM_D)),
            _resident((GM_D, D_MODEL)),
        ],
        out_specs=pl.BlockSpec((GM_TM, D_MODEL), lambda i: (i, 0)),
        out_shape=jax.ShapeDtypeStruct((SEQ, D_MODEL), F32),
        scratch_shapes=[pltpu.VMEM((GM_TM, GM_D), F32), pltpu.VMEM((GM_TM, GM_D), BF16)],
        compiler_params=_params("parallel"),
        name="gmlp",
    )(x, vec, w_in, v_norm, w_s, bias, w_out)


def _qkv_kernel(x_ref, vec_ref, w_ref, q_ref, k_ref, v_ref, selt_ref, prevt_ref, cnt_ref,
                kmean_ref):
    i = pl.program_id(0)

    @pl.when(i == 0)
    def _init():
        kmean_ref[...] = jnp.zeros_like(kmean_ref)
        cnt_ref[...] = jnp.zeros_like(cnt_ref)

    h = _adaln(x_ref[...], vec_ref).astype(BF16)
    q = jnp.dot(h, w_ref[:, :D_MODEL], preferred_element_type=F32) * MB_SCALE
    q_ref[...] = q
    k = jnp.dot(h, w_ref[:, D_MODEL:2 * D_MODEL], preferred_element_type=F32)
    k_ref[...] = k.astype(BF16)
    nb = QKV_TM // MB_BLOCK
    for b in range(nb):
        kmean_ref[pl.ds(nb * i + b, 1), :] = jnp.mean(
            k[b * MB_BLOCK:(b + 1) * MB_BLOCK], axis=0, keepdims=True)
    v = jnp.dot(h, w_ref[:, 2 * D_MODEL:], preferred_element_type=F32)
    v_ref[...] = v.astype(BF16)

    lane = lax.broadcasted_iota(jnp.int32, (1, LANES), 1)
    blk = lax.broadcasted_iota(jnp.int32, (N_KBLOCKS, MB_BLOCK), 0)
    contract_last = (((1,), (1,)), ((), ()))
    for b in range(nb):
        j = nb * i + b
        cols = slice(b * MB_BLOCK, (b + 1) * MB_BLOCK)
        for pair in range(MB_HEADS // 2):
            km = kmean_ref[:, pair * LANES:(pair + 1) * LANES]
            qp = q[cols, pair * LANES:(pair + 1) * LANES]
            for hh in range(2):
                head = 2 * pair + hh
                km_h = jnp.where((lane // MB_HEAD_DIM) == hh, km, 0.0)
                g = lax.dot_general(km_h, qp, contract_last, preferred_element_type=F32)
                g = jnp.where(blk < j, g, NEG_INF)
                sel = jnp.zeros((N_KBLOCKS, MB_BLOCK), F32)
                for r in range(MB_TOPK):
                    m = jnp.max(g, axis=0, keepdims=True)
                    first = jnp.min(jnp.where(g == m, blk, N_KBLOCKS), axis=0, keepdims=True)
                    hit = blk == first
                    sel = jnp.where(jnp.logical_and(hit, r < j), 1.0, sel)
                    g = jnp.where(hit, BELOW_NEG_INF, g)
                prevt_ref[head:head + 1, cols] = jnp.max(
                    jnp.where(blk == j - 1, sel, 0.0), axis=0, keepdims=True)
                routed = jnp.where(blk < j - 1, sel, 0.0)
                selt_ref[head * N_KBLOCKS:(head + 1) * N_KBLOCKS, cols] = routed.astype(BF16)
                rows = slice(head * N_KBLOCKS, (head + 1) * N_KBLOCKS)
                cnt_ref[rows, :] = cnt_ref[rows, :] + jnp.sum(routed, axis=1, keepdims=True)


def _qkv(x, vec, w_qkv):
    row = pl.BlockSpec((QKV_TM, D_MODEL), lambda i: (i, 0))
    n_items = MB_HEADS * N_KBLOCKS
    return pl.pallas_call(
        _qkv_kernel,
        grid=(SEQ // QKV_TM,),
        in_specs=[row, _resident((8, D_MODEL)), _resident((D_MODEL, 3 * D_MODEL))],
        out_specs=[row, row, row,
                   pl.BlockSpec((n_items, QKV_TM), lambda i: (0, i)),
                   pl.BlockSpec((MB_HEADS, QKV_TM), lambda i: (0, i)),
                   pl.BlockSpec((n_items, LANES), lambda i: (0, 0))],
        out_shape=[jax.ShapeDtypeStruct((SEQ, D_MODEL), F32),
                   jax.ShapeDtypeStruct((SEQ, D_MODEL), BF16),
                   jax.ShapeDtypeStruct((SEQ, D_MODEL), BF16),
                   jax.ShapeDtypeStruct((n_items, SEQ), BF16),
                   jax.ShapeDtypeStruct((MB_HEADS, SEQ), F32),
                   jax.ShapeDtypeStruct((n_items, LANES), F32)],
        scratch_shapes=[pltpu.VMEM((N_KBLOCKS, D_MODEL), F32)],
        compiler_params=_params("arbitrary"),
        name="moba_qkv_gate",
    )(x, vec, w_qkv)


def _route_kernel(selt_ref, base_ref, pos_ref, carry_ref):
    i = pl.program_id(0)

    @pl.when(i == 0)
    def _init():
        carry_ref[...] = jnp.zeros_like(carry_ref)

    sel = selt_ref[...]
    before = (lax.broadcasted_iota(jnp.int32, (ROUTE_T, ROUTE_T), 0)
              < lax.broadcasted_iota(jnp.int32, (ROUTE_T, ROUTE_T), 1))
    rank = jnp.dot(sel, jnp.where(before, 1.0, 0.0).astype(BF16), preferred_element_type=F32)
    offset = base_ref[...] + carry_ref[...]
    rank = rank + jnp.concatenate([offset] * (ROUTE_T // LANES), axis=1)
    carry_ref[...] = carry_ref[...] + jnp.sum(sel.astype(F32), axis=1, keepdims=True)
    val = jnp.where(sel > 0, rank, ROUTE_NONE)
    for head in range(MB_HEADS):
        vh = val[head * N_KBLOCKS:(head + 1) * N_KBLOCKS]
        for r in range(MB_TOPK):
            m = jnp.min(vh, axis=0, keepdims=True)
            slot = head * MB_TOPK + r
            pos_ref[slot:slot + 1, :] = jnp.where(m < ROUTE_NONE, m, float(NULL_ROW)).astype(jnp.int32)
            vh = jnp.where(vh == m, ROUTE_NONE, vh)


def _route(selt, base):
    n_items = MB_HEADS * N_KBLOCKS
    return pl.pallas_call(
        _route_kernel,
        grid=(SEQ // ROUTE_T,),
        in_specs=[pl.BlockSpec((n_items, ROUTE_T), lambda i: (0, i)),
                  pl.BlockSpec((n_items, LANES), lambda i: (0, 0))],
        out_specs=pl.BlockSpec((MB_HEADS * MB_TOPK, ROUTE_T), lambda i: (0, i)),
        out_shape=jax.ShapeDtypeStruct((MB_HEADS * MB_TOPK, SEQ), jnp.int32),
        scratch_shapes=[pltpu.VMEM((n_items, LANES), F32)],
        compiler_params=_params("arbitrary"),
        name="moba_route",
    )(selt, base)


def _moba_kernel(choff_ref, chblk_ref, chvalid_ref, nch_ref, c31_ref,
                 qf_ref, pos_hbm, k_ref, v_ref, prev_ref, bown_ref, bprev_ref,
                 o_ref, buf_ref, stage_ref, hold_ref, s_ref, p_ref, m_ref,
                 pos_even_ref, pos_odd_ref, pos_sem):
    head = pl.program_id(0)
    t = pl.program_id(1)
    hh = head % 2
    lane = lax.broadcasted_iota(jnp.int32, (1, LANES), 1)
    in_head = (lane // MB_HEAD_DIM) == hh
    stat0 = (1 - hh) * MB_HEAD_DIM
    is_m_lane = lane == stat0
    is_l_lane = lane == stat0 + 1
    contract_last = (((1,), (1,)), ((), ()))
    ones_bf = jnp.ones((), BF16)

    def queries(rows):
        return jnp.where(in_head, rows, 0.0).astype(BF16)

    def values(rows):
        return jnp.where(in_head, rows, ones_bf)

    @pl.when(jnp.logical_and(t == 0, head == 0))
    def _init():
        def zero(c, _):
            buf_ref[pl.ds(pl.multiple_of(c * MB_BLOCK, MB_BLOCK), MB_BLOCK), :] = (
                jnp.zeros((MB_BLOCK, LANES), F32))
            return 0
        lax.fori_loop(0, BUF_ROWS // MB_BLOCK, zero, 0)

        def zero_hold(c, _):
            hold_ref[pl.ds(pl.multiple_of(c * MB_BLOCK, MB_BLOCK), MB_BLOCK), :] = (
                jnp.zeros((MB_BLOCK, LANES), BF16))
            return 0
        lax.fori_loop(0, SEQ // MB_BLOCK, zero_hold, 0)

    def dispatch(pos_ref):
        for a in range(STEP_ROWS):
            row = qf_ref[a:a + 1, :]
            for r in range(MB_TOPK):
                buf_ref[pl.ds(pos_ref[r * STEP_ROWS + a], 1), :] = row

    @pl.when(t == NSTEPS)
    def _routed():
        buf_ref[NULL_ROW:NULL_ROW + ROUTE_ALIGN, :] = jnp.where(
            is_m_lane, NEG_INF, jnp.zeros((ROUTE_ALIGN, LANES), F32))
        n = nch_ref[head]
        far_bias = c31_ref[head]
        row_id = lax.broadcasted_iota(jnp.int32, (MB_BLOCK, LANES), 0)

        def meta(i):
            idx = head * MAX_CHUNKS + jnp.clip(i, 0, n - 1)
            off = pl.multiple_of(choff_ref[idx], ROUTE_ALIGN)
            blk = pl.multiple_of(chblk_ref[idx] * MB_BLOCK, MB_BLOCK)
            return off, blk, jnp.where(jnp.logical_and(i >= 0, i < n), chvalid_ref[idx], 0)

        def chunk_queries(off):
            return queries(buf_ref[pl.ds(off, MB_BLOCK), :])

        def scores(qh, blk):
            return lax.dot_general(qh, k_ref[pl.ds(blk, MB_BLOCK), :], contract_last,
                                   preferred_element_type=F32)

        def probs(u):
            s = s_ref[u]
            m = jnp.max(s, axis=1, keepdims=True)
            p_ref[u] = jnp.exp(s - m).astype(BF16)
            m_ref[u] = jnp.broadcast_to(m, (MB_BLOCK, LANES))

        def finish(u, off, blk, valid):
            acc = jnp.dot(p_ref[u], values(v_ref[pl.ds(blk, MB_BLOCK), :]), preferred_element_type=F32)
            new = jnp.where(is_m_lane, m_ref[u] + far_bias, acc)
            pltpu.store(buf_ref.at[pl.ds(off, MB_BLOCK), :], new, mask=row_id < valid)

        @pl.when(n > 0)
        def _chunks():
            slots = range(CHUNKS_PER_BODY)
            p_ref[...] = jnp.zeros_like(p_ref)
            m_ref[...] = jnp.zeros_like(m_ref)
            for u in slots:
                off, blk, _ = meta(u)
                s_ref[u] = scores(chunk_queries(off), blk)

            def chunk_group(c, _):
                i = CHUNKS_PER_BODY * c
                nxt = [meta(i + CHUNKS_PER_BODY + u) for u in slots]
                done = [meta(i - CHUNKS_PER_BODY + u) for u in slots]
                rows = [chunk_queries(off) for off, _, _ in nxt]
                for u in slots:
                    finish(u, *done[u])
                for u in slots:
                    probs(u)
                for u in slots:
                    s_ref[u] = scores(rows[u], nxt[u][1])
                return 0
            lax.fori_loop(0, (n + CHUNKS_PER_BODY - 1) // CHUNKS_PER_BODY + 1, chunk_group, 0)

    def combine(pos_ref):
        step = t - (NSTEPS + 1)

        n_sub = STEP_ROWS // MB_BLOCK
        for jj in range(n_sub):
            j = step * n_sub + jj
            rs = slice(jj * MB_BLOCK, (jj + 1) * MB_BLOCK)
            win = pl.ds(pl.multiple_of(jnp.maximum(j - 1, 0) * MB_BLOCK, MB_BLOCK), 2 * MB_BLOCK)
            s = lax.dot_general(queries(qf_ref[rs, :]), k_ref[win, :], contract_last,
                                preferred_element_type=F32)
            picked = jnp.sum(jnp.where(lane == head, prev_ref[rs, :], 0.0),
                             axis=1, keepdims=True) > 0.0
            bias_lo, bias_hi = bprev_ref[...], bown_ref[...]
            if jj == 0:
                first = step == 0
                bias_lo = jnp.where(first, bown_ref[...], bias_lo)
                bias_hi = jnp.where(first, NEG_INF, bias_hi)
                picked = jnp.logical_or(picked, first)
            s_lo = jnp.where(picked, s[:, :MB_BLOCK] + bias_lo, NEG_INF)
            s_hi = s[:, MB_BLOCK:] + bias_hi
            m_d = jnp.maximum(jnp.max(s_lo, axis=1, keepdims=True), jnp.max(s_hi, axis=1, keepdims=True))
            p = jnp.concatenate([jnp.exp(s_lo - m_d), jnp.exp(s_hi - m_d)], axis=1).astype(BF16)
            acc = jnp.dot(p, values(v_ref[win, :]), preferred_element_type=F32)

            for a in range(jj * MB_BLOCK, (jj + 1) * MB_BLOCK):
                for r in range(MB_TOPK):
                    stage_ref[r, a:a + 1, :] = buf_ref[pl.ds(pos_ref[r * STEP_ROWS + a], 1), :]

            parts = [stage_ref[r, rs, :] for r in range(MB_TOPK)]
            ms = [jnp.sum(jnp.where(is_m_lane, part, 0.0), axis=1, keepdims=True) for part in parts]
            m_all = m_d
            for m_r in ms:
                m_all = jnp.maximum(m_all, m_r)
            total = jnp.exp(m_d - m_all) * acc
            for m_r, part in zip(ms, parts):
                total = total + jnp.exp(m_r - m_all) * part
            den = jnp.sum(jnp.where(is_l_lane, total, 0.0), axis=1, keepdims=True)
            out = (total / den).astype(BF16)
            hold_rows = pl.ds(pl.multiple_of(j * MB_BLOCK, MB_BLOCK), MB_BLOCK)
            take_parked = jnp.logical_and(lane < MB_HEAD_DIM, hh == 1)
            merged = jnp.where(take_parked, hold_ref[hold_rows, :], out)
            hold_ref[hold_rows, :] = merged
            o_ref[rs, :] = merged

    steps_per_head = 2 * NSTEPS
    use = head * steps_per_head + jnp.where(t < NSTEPS, t, t - 1)
    pos_bufs = (pos_even_ref, pos_odd_ref)

    def pos_copy(u, parity):
        block = (u // steps_per_head) * NSTEPS + u % NSTEPS
        src = pos_hbm.at[pl.ds(pl.multiple_of(block * POS_BLOCK, POS_BLOCK), POS_BLOCK)]
        return pltpu.make_async_copy(src, pos_bufs[parity], pos_sem.at[parity])

    for parity in range(2):
        @pl.when(jnp.logical_and(t != NSTEPS, use % 2 == parity))
        def _positions_step(parity=parity):
            @pl.when(use == 0)
            def _first():
                pos_copy(use, parity).start()

            @pl.when(use + 1 < MB_HEADS * steps_per_head)
            def _prefetch():
                pos_copy(use + 1, 1 - parity).start()

            pos_copy(use, parity).wait()

            @pl.when(t < NSTEPS)
            def _dispatch():
                dispatch(pos_bufs[parity])

            @pl.when(t > NSTEPS)
            def _combine():
                combine(pos_bufs[parity])


def _moba_attention(chunk_meta, c31, qf, pos, k, v, prevc, bias_own, bias_prev):
    def seq_step(t):
        return jnp.where(t < NSTEPS, t, jnp.maximum(t - (NSTEPS + 1), 0) + jnp.where(t == NSTEPS, NSTEPS - 1, 0))

    def combine_step(t):
        return jnp.maximum(t - (NSTEPS + 1), 0)

    pair_resident = pl.BlockSpec((SEQ, LANES), lambda h, t, *_: (0, h // 2), pipeline_mode=pl.Buffered(1))
    bias = pl.BlockSpec((None, MB_BLOCK, MB_BLOCK), lambda h, t, *_: (h, 0, 0))
    return pl.pallas_call(
        _moba_kernel,
        grid_spec=pltpu.PrefetchScalarGridSpec(
            num_scalar_prefetch=5,
            grid=(MB_HEADS, 2 * NSTEPS + 1),
            in_specs=[
                pl.BlockSpec((STEP_ROWS, LANES), lambda h, t, *_: (seq_step(t), h // 2)),
                pl.BlockSpec(memory_space=pl.ANY),
                pair_resident, pair_resident,
                pl.BlockSpec((STEP_ROWS, LANES), lambda h, t, *_: (combine_step(t), 0)),
                bias, bias,
            ],
            out_specs=pl.BlockSpec(
                (STEP_ROWS, LANES),
                lambda h, t, *_: (jnp.where(h % 2 == 1, combine_step(t), 0), h // 2)),
            scratch_shapes=[pltpu.VMEM((BUF_ROWS, LANES), F32),
                            pltpu.VMEM((MB_TOPK, STEP_ROWS, LANES), F32),
                            pltpu.VMEM((SEQ, LANES), BF16),
                            pltpu.VMEM((CHUNKS_PER_BODY, MB_BLOCK, MB_BLOCK), F32),
                            pltpu.VMEM((CHUNKS_PER_BODY, MB_BLOCK, MB_BLOCK), BF16),
                            pltpu.VMEM((CHUNKS_PER_BODY, MB_BLOCK, LANES), F32),
                            pltpu.SMEM((POS_BLOCK,), jnp.int32), pltpu.SMEM((POS_BLOCK,), jnp.int32),
                            pltpu.SemaphoreType.DMA((2,))],
        ),
        out_shape=jax.ShapeDtypeStruct((SEQ, D_MODEL), BF16),
        compiler_params=_params("arbitrary", "arbitrary"),
        name="moba_attention",
    )(*chunk_meta, c31, qf, pos, k, v, prevc, bias_own, bias_prev)


def _proj_kernel(x_ref, a_ref, vec_ref, w_ref, o_ref):
    y = jnp.dot(a_ref[...], w_ref[...], preferred_element_type=F32)
    o_ref[...] = x_ref[...] + vec_ref[3:4, :] * y


def _out_proj(x, a, vec, w_o):
    row = pl.BlockSpec((PROJ_TM, D_MODEL), lambda i: (i, 0))
    return pl.pallas_call(
        _proj_kernel,
        grid=(SEQ // PROJ_TM,),
        in_specs=[row, row, _resident((8, D_MODEL)), _resident((D_MODEL, D_MODEL))],
        out_specs=row,
        out_shape=jax.ShapeDtypeStruct((SEQ, D_MODEL), F32),
        compiler_params=_params("parallel"),
        name="moba_out_proj",
    )(x, a, vec, w_o)


def _t5_bucket_np(rel):
    n = np.maximum(rel, 0)
    nf = np.maximum(n, T5_MAX_EXACT).astype(np.float32)
    large = T5_MAX_EXACT + (np.log(nf / np.float32(T5_MAX_EXACT))
                            / np.float32(math.log(T5_MAX_DISTANCE / T5_MAX_EXACT))
                            * np.float32(T5_NUM_BUCKETS - T5_MAX_EXACT)).astype(np.int32)
    large = np.minimum(large, T5_NUM_BUCKETS - 1)
    return np.where(n < T5_MAX_EXACT, n, large)


def _moba_bias_tables(rel_bias):
    a = np.arange(MB_BLOCK)
    rel_own = a[:, None] - a[None, :]
    table = rel_bias.astype(F32).T

    def expand(rel):
        bucket = jnp.asarray(_t5_bucket_np(rel).reshape(-1).astype(np.int32))
        onehot = (bucket[None, :] == jnp.arange(T5_NUM_BUCKETS)[:, None]).astype(F32)
        return jnp.dot(table, onehot, precision=lax.Precision.HIGHEST).reshape(
            MB_HEADS, MB_BLOCK, MB_BLOCK)

    b_own = jnp.where(jnp.asarray(rel_own >= 0)[None], expand(rel_own), NEG_INF)
    return b_own, expand(rel_own + MB_BLOCK), table[:, T5_NUM_BUCKETS - 1]


def _chunk_tables(cnt, start):
    nch = (cnt + MB_BLOCK - 1) // MB_BLOCK
    cum = jnp.cumsum(nch, axis=1)
    i = jnp.arange(MAX_CHUNKS)
    blk = jnp.sum(cum[:, None, :] <= i[None, :, None], axis=2)
    blk = jnp.minimum(blk, N_KBLOCKS - 1)
    onehot = blk[:, :, None] == jnp.arange(N_KBLOCKS)[None, None, :]

    def pick(x):
        return jnp.sum(jnp.where(onehot, x[:, None, :], 0), axis=2)

    def flat(x):
        return x.reshape(-1).astype(jnp.int32)
    within = i[None, :] - pick(cum - nch)
    off = pick(start) + within * MB_BLOCK
    valid = jnp.clip(pick(cnt) - within * MB_BLOCK, 0, MB_BLOCK)
    return flat(off), flat(blk), flat(valid), flat(cum[:, -1])


def _moba(xs, vec, w_qkv, w_o, rel_bias):
    qf, k, v, selt, prevt, cnt = _qkv(xs, vec, w_qkv)
    cnt = cnt[:, 0].astype(jnp.int32).reshape(MB_HEADS, N_KBLOCKS)
    padded = (cnt + ROUTE_ALIGN - 1) // ROUTE_ALIGN * ROUTE_ALIGN
    start = jnp.cumsum(padded, axis=1) - padded
    base = jnp.broadcast_to(start.reshape(-1, 1).astype(F32), (MB_HEADS * N_KBLOCKS, LANES))
    pos = _route(selt, base).reshape(MB_HEADS, MB_TOPK, NSTEPS, STEP_ROWS)
    pos = pos.transpose(0, 2, 1, 3).reshape(-1)
    b_own, b_prev, c31 = _moba_bias_tables(rel_bias)
    prevc = jnp.pad(prevt.T, ((0, 0), (0, LANES - MB_HEADS)))
    a = _moba_attention(_chunk_tables(cnt, start), c31, qf, pos, k, v, prevc, b_own, b_prev)
    return _out_proj(xs, a, vec, w_o)


def _vec(norm_g, mod3):
    return jnp.concatenate([norm_g[None, :], mod3, jnp.zeros((4, D_MODEL), F32)], axis=0)


def kernel(x, c, rel_bias, mod_w, mod_b, norm_g, ffn_w_in, ffn_w_out, gmlp_w_in, gmlp_v_norm,
           gmlp_w_s, gmlp_b_s, gmlp_w_out, moba_w_qkv, moba_w_o, final_norm):
    assert x.shape == (1, SEQ, D_MODEL)
    xs = x.reshape(SEQ, D_MODEL)
    mod = _modulation(c.reshape(D_MODEL, 1), mod_w, mod_b).reshape(DEPTH, N_SUBLAYERS, 3, D_MODEL)
    fin = final_norm.reshape(1, D_MODEL)
    w_in_all, w_out_all = ffn_w_in.astype(BF16), ffn_w_out.astype(BF16)

    for i in range(DEPTH):
        li = i // 2
        xs = _ffn(xs, _vec(norm_g[i, 0], mod[i, 0]), w_in_all, w_out_all, i, 0, fin, False)
        vec = _vec(norm_g[i, 1], mod[i, 1])
        if i % 2 == 0:
            bias = jnp.repeat(gmlp_b_s[li].T, GM_GROUP_DIM, axis=1)
            xs = _gmlp(xs, vec, gmlp_w_in[li].astype(BF16), gmlp_v_norm[li].reshape(1, GM_D),
                       gmlp_w_s[li].astype(BF16), bias, gmlp_w_out[li].astype(BF16))
        else:
            xs = _moba(xs, vec, moba_w_qkv[li].astype(BF16), moba_w_o[li].astype(BF16), rel_bias)
        xs = _ffn(xs, _vec(norm_g[i, 2], mod[i, 2]), w_in_all, w_out_all, i, 1, fin, i == DEPTH - 1)
    return xs.reshape(1, SEQ, D_MODEL)
```

```python
import functools
import math

import numpy as np
import jax
import jax.numpy as jnp
from jax import lax
from jax.experimental import pallas as pl
from jax.experimental.pallas import tpu as pltpu

F32 = jnp.float32
BF16 = jnp.bfloat16

D_MODEL = 1024
SEQ = 16384
DEPTH = 2
N_SUBLAYERS = 3
RMS_EPS = 1e-6
D_FF = 2816

GM_D = 3 * D_MODEL
GM_GROUPS = 16
GM_GROUP_DIM = GM_D // GM_GROUPS
GM_CHUNK = 128
GM_PAIR = 2 * GM_GROUP_DIM

MB_HEADS = 16
MB_HEAD_DIM = 64
MB_BLOCK = 256
MB_TOPK = 3
MB_SCALE = MB_HEAD_DIM ** -0.5
N_KBLOCKS = SEQ // MB_BLOCK

T5_NUM_BUCKETS = 32
T5_MAX_EXACT = 16
T5_MAX_DISTANCE = 128
NEG_INF = -1e30
BELOW_NEG_INF = -3e38

LANES = 128
VMEM_LIMIT = 56 * 1024 * 1024

MOD_TN = 1536
FFN_TM = 512
FFN_CHUNK = 1408
GM_TM = 256
GM_VCHUNK = 768
QKV_TM = 512
PROJ_TM = 512

ROUTE_T = 512
ROUTE_ALIGN = 8
STEP_ROWS = 2048
NSTEPS = SEQ // STEP_ROWS
BUF_DATA_ROWS = MB_TOPK * SEQ + N_KBLOCKS * ROUTE_ALIGN
BUF_ROWS = BUF_DATA_ROWS + 3 * MB_BLOCK
NULL_ROW = BUF_ROWS - ROUTE_ALIGN
ROUTE_NONE = 1e9
MAX_CHUNKS = MB_TOPK * SEQ // MB_BLOCK + N_KBLOCKS
POS_BLOCK = MB_TOPK * STEP_ROWS
CHUNKS_PER_BODY = 8


def _resident(shape):
    nd = len(shape)
    return pl.BlockSpec(shape, lambda *_: (0,) * nd, pipeline_mode=pl.Buffered(1))


def _params(*sem):
    return pltpu.CompilerParams(dimension_semantics=sem, vmem_limit_bytes=VMEM_LIMIT)


def _rms(x):
    return x * lax.rsqrt(jnp.mean(x * x, axis=-1, keepdims=True) + RMS_EPS)


def _adaln(x, vec_ref):
    y = _rms(x) * vec_ref[0:1, :]
    return y * (1.0 + vec_ref[2:3, :]) + vec_ref[1:2, :]


def _gelu(x):
    return 0.5 * x * (1.0 + lax.erf(x * np.float32(math.sqrt(0.5))))


def _mod_kernel(c_ref, w_ref, b_ref, o_ref):
    c = c_ref[...]
    c_act = c * jax.nn.sigmoid(c)
    o_ref[...] = jnp.sum(c_act * w_ref[...], axis=0, keepdims=True) + b_ref[...]


def _modulation(c_col, mod_w, mod_b):
    n = mod_w.shape[-1]
    return pl.pallas_call(
        _mod_kernel,
        grid=(DEPTH, n // MOD_TN),
        in_specs=[
            pl.BlockSpec((D_MODEL, 1), lambda i, j: (0, 0)),
            pl.BlockSpec((None, D_MODEL, MOD_TN), lambda i, j: (i, 0, j)),
            pl.BlockSpec((None, 1, MOD_TN), lambda i, j: (i, 0, j)),
        ],
        out_specs=pl.BlockSpec((None, 1, MOD_TN), lambda i, j: (i, 0, j)),
        out_shape=jax.ShapeDtypeStruct((DEPTH, 1, n), F32),
        compiler_params=_params("arbitrary", "arbitrary"),
        name="modulation",
    )(c_col, mod_w, mod_b.reshape(DEPTH, 1, n))


def _ffn_kernel(x_ref, vec_ref, win_ref, wout_ref, fin_ref, o_ref, act_ref, *, final):
    x = x_ref[...]
    h = _adaln(x, vec_ref).astype(BF16)
    for c in range(D_FF // FFN_CHUNK):
        lo = c * FFN_CHUNK
        g = jnp.dot(h, win_ref[:, lo:lo + FFN_CHUNK], preferred_element_type=F32)
        u = jnp.dot(h, win_ref[:, D_FF + lo:D_FF + lo + FFN_CHUNK], preferred_element_type=F32)
        act_ref[:, lo:lo + FFN_CHUNK] = (g * jax.nn.sigmoid(g) * u).astype(BF16)
    y = jnp.dot(act_ref[...], wout_ref[...], preferred_element_type=F32)
    out = x + (0.5 * vec_ref[3:4, :]) * y
    if final:
        out = _rms(out) * fin_ref[...]
    o_ref[...] = out


def _ffn(x, vec, w_in_all, w_out_all, layer, which, fin, final):
    def one_matrix(rows, cols):
        return pl.BlockSpec((None, None, rows, cols), lambda i: (layer, which, 0, 0),
                            pipeline_mode=pl.Buffered(1))
    return pl.pallas_call(
        functools.partial(_ffn_kernel, final=final),
        grid=(SEQ // FFN_TM,),
        in_specs=[
            pl.BlockSpec((FFN_TM, D_MODEL), lambda i: (i, 0)),
            _resident((8, D_MODEL)),
            one_matrix(D_MODEL, 2 * D_FF),
            one_matrix(D_FF, D_MODEL),
            _resident((1, D_MODEL)),
        ],
        out_specs=pl.BlockSpec((FFN_TM, D_MODEL), lambda i: (i, 0)),
        out_shape=jax.ShapeDtypeStruct((SEQ, D_MODEL), F32),
        scratch_shapes=[pltpu.VMEM((FFN_TM, D_FF), BF16)],
        compiler_params=_params("parallel"),
        name="ffn_final" if final else "ffn",
    )(x, vec, w_in_all, w_out_all, fin)


def _gmlp_kernel(x_ref, vec_ref, win_ref, vnorm_ref, ws_ref, bias_ref, wout_ref,
                 o_ref, v_ref, g_ref):
    x = x_ref[...]
    h = _adaln(x, vec_ref).astype(BF16)

    ssq = jnp.zeros((GM_TM, 1), F32)
    for c in range(GM_D // GM_VCHUNK):
        lo = c * GM_VCHUNK
        v = _gelu(jnp.dot(h, win_ref[:, GM_D + lo:GM_D + lo + GM_VCHUNK],
                          preferred_element_type=F32))
        v_ref[:, lo:lo + GM_VCHUNK] = v
        ssq = ssq + jnp.sum(v * v, axis=-1, keepdims=True)
    r = lax.rsqrt(ssq * (1.0 / GM_D) + RMS_EPS)

    row = lax.broadcasted_iota(jnp.int32, (GM_CHUNK, GM_CHUNK), 0)
    col = lax.broadcasted_iota(jnp.int32, (GM_CHUNK, GM_CHUNK), 1)
    causal = row >= col
    low_half = lax.broadcasted_iota(jnp.int32, (GM_CHUNK, LANES), 1) < (GM_GROUP_DIM - LANES)

    for p in range(GM_GROUPS // 2):
        lo = p * GM_PAIR
        vn = (v_ref[:, lo:lo + GM_PAIR] * r * vnorm_ref[:, lo:lo + GM_PAIR]).astype(BF16)
        u = _gelu(jnp.dot(h, win_ref[:, lo:lo + GM_PAIR], preferred_element_type=F32))
        w0 = jnp.where(causal, ws_ref[2 * p], jnp.zeros((), BF16))
        w1 = jnp.where(causal, ws_ref[2 * p + 1], jnp.zeros((), BF16))
        for c in range(GM_TM // GM_CHUNK):
            rows = slice(c * GM_CHUNK, (c + 1) * GM_CHUNK)
            vc = vn[rows]
            a = jnp.dot(w0, vc[:, :2 * LANES], preferred_element_type=F32)
            b = jnp.dot(w1, vc[:, LANES:], preferred_element_type=F32)
            mid = jnp.where(low_half, a[:, LANES:], b[:, :LANES])
            sv = jnp.concatenate([a[:, :LANES], mid, b[:, LANES:]], axis=1)
            sv = sv + bias_ref[:, lo:lo + GM_PAIR]
            g_ref[rows, lo:lo + GM_PAIR] = (u[rows] * sv).astype(BF16)

    y = jnp.dot(g_ref[...], wout_ref[...], preferred_element_type=F32)
    o_ref[...] = x + vec_ref[3:4, :] * y


def _gmlp(x, vec, w_in, v_norm, w_s, bias, w_out):
    return pl.pallas_call(
        _gmlp_kernel,
        grid=(SEQ // GM_TM,),
        in_specs=[
            pl.BlockSpec((GM_TM, D_MODEL), lambda i: (i, 0)),
            _resident((8, D_MODEL)),
            _resident((D_MODEL, 2 * GM_D)),
            _resident((1, GM_D)),
            _resident((GM_GROUPS, GM_CHUNK, GM_CHUNK)),
            _resident((GM_CHUNK, GM_D)),
            _resident((GM_D, D_MODEL)),
        ],
        out_specs=pl.BlockSpec((GM_TM, D_MODEL), lambda i: (i, 0)),
        out_shape=jax.ShapeDtypeStruct((SEQ, D_MODEL), F32),
        scratch_shapes=[pltpu.VMEM((GM_TM, GM_D), F32), pltpu.VMEM((GM_TM, GM_D), BF16)],
        compiler_params=_params("parallel"),
        name="gmlp",
    )(x, vec, w_in, v_norm, w_s, bias, w_out)


def _qkv_kernel(x_ref, vec_ref, w_ref, q_ref, k_ref, v_ref, selt_ref, prevt_ref, cnt_ref,
                kmean_ref):
    i = pl.program_id(0)

    @pl.when(i == 0)
    def _init():
        kmean_ref[...] = jnp.zeros_like(kmean_ref)
        cnt_ref[...] = jnp.zeros_like(cnt_ref)

    h = _adaln(x_ref[...], vec_ref).astype(BF16)
    q = jnp.dot(h, w_ref[:, :D_MODEL], preferred_element_type=F32) * MB_SCALE
    q_ref[...] = q
    k = jnp.dot(h, w_ref[:, D_MODEL:2 * D_MODEL], preferred_element_type=F32)
    k_ref[...] = k.astype(BF16)
    nb = QKV_TM // MB_BLOCK
    for b in range(nb):
        kmean_ref[pl.ds(nb * i + b, 1), :] = jnp.mean(
            k[b * MB_BLOCK:(b + 1) * MB_BLOCK], axis=0, keepdims=True)
    v = jnp.dot(h, w_ref[:, 2 * D_MODEL:], preferred_element_type=F32)
    v_ref[...] = v.astype(BF16)

    lane = lax.broadcasted_iota(jnp.int32, (1, LANES), 1)
    blk = lax.broadcasted_iota(jnp.int32, (N_KBLOCKS, MB_BLOCK), 0)
    contract_last = (((1,), (1,)), ((), ()))
    for b in range(nb):
        j = nb * i + b
        cols = slice(b * MB_BLOCK, (b + 1) * MB_BLOCK)
        for pair in range(MB_HEADS // 2):
            km = kmean_ref[:, pair * LANES:(pair + 1) * LANES]
            qp = q[cols, pair * LANES:(pair + 1) * LANES]
            for hh in range(2):
                head = 2 * pair + hh
                km_h = jnp.where((lane // MB_HEAD_DIM) == hh, km, 0.0)
                g = lax.dot_general(km_h, qp, contract_last, preferred_element_type=F32)
                g = jnp.where(blk < j, g, NEG_INF)
                sel = jnp.zeros((N_KBLOCKS, MB_BLOCK), F32)
                for r in range(MB_TOPK):
                    m = jnp.max(g, axis=0, keepdims=True)
                    first = jnp.min(jnp.where(g == m, blk, N_KBLOCKS), axis=0, keepdims=True)
                    hit = blk == first
                    sel = jnp.where(jnp.logical_and(hit, r < j), 1.0, sel)
                    g = jnp.where(hit, BELOW_NEG_INF, g)
                prevt_ref[head:head + 1, cols] = jnp.max(
                    jnp.where(blk == j - 1, sel, 0.0), axis=0, keepdims=True)
                routed = jnp.where(blk < j - 1, sel, 0.0)
                selt_ref[head * N_KBLOCKS:(head + 1) * N_KBLOCKS, cols] = routed.astype(BF16)
                rows = slice(head * N_KBLOCKS, (head + 1) * N_KBLOCKS)
                cnt_ref[rows, :] = cnt_ref[rows, :] + jnp.sum(routed, axis=1, keepdims=True)


def _qkv(x, vec, w_qkv):
    row = pl.BlockSpec((QKV_TM, D_MODEL), lambda i: (i, 0))
    n_items = MB_HEADS * N_KBLOCKS
    return pl.pallas_call(
        _qkv_kernel,
        grid=(SEQ // QKV_TM,),
        in_specs=[row, _resident((8, D_MODEL)), _resident((D_MODEL, 3 * D_MODEL))],
        out_specs=[row, row, row,
                   pl.BlockSpec((n_items, QKV_TM), lambda i: (0, i)),
                   pl.BlockSpec((MB_HEADS, QKV_TM), lambda i: (0, i)),
                   pl.BlockSpec((n_items, LANES), lambda i: (0, 0))],
        out_shape=[jax.ShapeDtypeStruct((SEQ, D_MODEL), F32),
                   jax.ShapeDtypeStruct((SEQ, D_MODEL), BF16),
                   jax.ShapeDtypeStruct((SEQ, D_MODEL), BF16),
                   jax.ShapeDtypeStruct((n_items, SEQ), BF16),
                   jax.ShapeDtypeStruct((MB_HEADS, SEQ), F32),
                   jax.ShapeDtypeStruct((n_items, LANES), F32)],
        scratch_shapes=[pltpu.VMEM((N_KBLOCKS, D_MODEL), F32)],
        compiler_params=_params("arbitrary"),
        name="moba_qkv_gate",
    )(x, vec, w_qkv)


def _route_kernel(selt_ref, base_ref, pos_ref, carry_ref):
    i = pl.program_id(0)

    @pl.when(i == 0)
    def _init():
        carry_ref[...] = jnp.zeros_like(carry_ref)

    sel = selt_ref[...]
    before = (lax.broadcasted_iota(jnp.int32, (ROUTE_T, ROUTE_T), 0)
              < lax.broadcasted_iota(jnp.int32, (ROUTE_T, ROUTE_T), 1))
    rank = jnp.dot(sel, jnp.where(before, 1.0, 0.0).astype(BF16), preferred_element_type=F32)
    offset = base_ref[...] + carry_ref[...]
    rank = rank + jnp.concatenate([offset] * (ROUTE_T // LANES), axis=1)
    carry_ref[...] = carry_ref[...] + jnp.sum(sel.astype(F32), axis=1, keepdims=True)
    val = jnp.where(sel > 0, rank, ROUTE_NONE)
    for head in range(MB_HEADS):
        vh = val[head * N_KBLOCKS:(head + 1) * N_KBLOCKS]
        for r in range(MB_TOPK):
            m = jnp.min(vh, axis=0, keepdims=True)
            slot = head * MB_TOPK + r
            pos_ref[slot:slot + 1, :] = jnp.where(m < ROUTE_NONE, m, float(NULL_ROW)).astype(jnp.int32)
            vh = jnp.where(vh == m, ROUTE_NONE, vh)


def _route(selt, base):
    n_items = MB_HEADS * N_KBLOCKS
    return pl.pallas_call(
        _route_kernel,
        grid=(SEQ // ROUTE_T,),
        in_specs=[pl.BlockSpec((n_items, ROUTE_T), lambda i: (0, i)),
                  pl.BlockSpec((n_items, LANES), lambda i: (0, 0))],
        out_specs=pl.BlockSpec((MB_HEADS * MB_TOPK, ROUTE_T), lambda i: (0, i)),
        out_shape=jax.ShapeDtypeStruct((MB_HEADS * MB_TOPK, SEQ), jnp.int32),
        scratch_shapes=[pltpu.VMEM((n_items, LANES), F32)],
        compiler_params=_params("arbitrary"),
        name="moba_route",
    )(selt, base)


def _moba_kernel(choff_ref, chblk_ref, chvalid_ref, nch_ref, c31_ref,
                 qf_ref, pos_hbm, k_ref, v_ref, prev_ref, bown_ref, bprev_ref,
                 o_ref, buf_ref, stage_ref, hold_ref, s_ref, p_ref, m_ref,
                 pos_even_ref, pos_odd_ref, pos_sem):
    head = pl.program_id(0)
    t = pl.program_id(1)
    hh = head % 2
    lane = lax.broadcasted_iota(jnp.int32, (1, LANES), 1)
    in_head = (lane // MB_HEAD_DIM) == hh
    stat0 = (1 - hh) * MB_HEAD_DIM
    is_m_lane = lane == stat0
    is_l_lane = lane == stat0 + 1
    contract_last = (((1,), (1,)), ((), ()))
    ones_bf = jnp.ones((), BF16)

    def queries(rows):
        return jnp.where(in_head, rows, 0.0).astype(BF16)

    def values(rows):
        return jnp.where(in_head, rows, ones_bf)

    @pl.when(jnp.logical_and(t == 0, head == 0))
    def _init():
        def zero(c, _):
            buf_ref[pl.ds(pl.multiple_of(c * MB_BLOCK, MB_BLOCK), MB_BLOCK), :] = (
                jnp.zeros((MB_BLOCK, LANES), F32))
            return 0
        lax.fori_loop(0, BUF_ROWS // MB_BLOCK, zero, 0)

        def zero_hold(c, _):
            hold_ref[pl.ds(pl.multiple_of(c * MB_BLOCK, MB_BLOCK), MB_BLOCK), :] = (
                jnp.zeros((MB_BLOCK, LANES), BF16))
            return 0
        lax.fori_loop(0, SEQ // MB_BLOCK, zero_hold, 0)

    def dispatch(pos_ref):
        for a in range(STEP_ROWS):
            row = qf_ref[a:a + 1, :]
            for r in range(MB_TOPK):
                buf_ref[pl.ds(pos_ref[r * STEP_ROWS + a], 1), :] = row

    @pl.when(t == NSTEPS)
    def _routed():
        buf_ref[NULL_ROW:NULL_ROW + ROUTE_ALIGN, :] = jnp.where(
            is_m_lane, NEG_INF, jnp.zeros((ROUTE_ALIGN, LANES), F32))
        n = nch_ref[head]
        far_bias = c31_ref[head]
        row_id = lax.broadcasted_iota(jnp.int32, (MB_BLOCK, LANES), 0)

        def meta(i):
            idx = head * MAX_CHUNKS + jnp.clip(i, 0, n - 1)
            off = pl.multiple_of(choff_ref[idx], ROUTE_ALIGN)
            blk = pl.multiple_of(chblk_ref[idx] * MB_BLOCK, MB_BLOCK)
            return off, blk, jnp.where(jnp.logical_and(i >= 0, i < n), chvalid_ref[idx], 0)

        def chunk_queries(off):
            return queries(buf_ref[pl.ds(off, MB_BLOCK), :])

        def scores(qh, blk):
            return lax.dot_general(qh, k_ref[pl.ds(blk, MB_BLOCK), :], contract_last,
                                   preferred_element_type=F32)

        def probs(u):
            s = s_ref[u]
            m = jnp.max(s, axis=1, keepdims=True)
            p_ref[u] = jnp.exp(s - m).astype(BF16)
            m_ref[u] = jnp.broadcast_to(m, (MB_BLOCK, LANES))

        def finish(u, off, blk, valid):
            acc = jnp.dot(p_ref[u], values(v_ref[pl.ds(blk, MB_BLOCK), :]), preferred_element_type=F32)
            new = jnp.where(is_m_lane, m_ref[u] + far_bias, acc)
            pltpu.store(buf_ref.at[pl.ds(off, MB_BLOCK), :], new, mask=row_id < valid)

        @pl.when(n > 0)
        def _chunks():
            slots = range(CHUNKS_PER_BODY)
            p_ref[...] = jnp.zeros_like(p_ref)
            m_ref[...] = jnp.zeros_like(m_ref)
            for u in slots:
                off, blk, _ = meta(u)
                s_ref[u] = scores(chunk_queries(off), blk)

            def chunk_group(c, _):
                i = CHUNKS_PER_BODY * c
                nxt = [meta(i + CHUNKS_PER_BODY + u) for u in slots]
                done = [meta(i - CHUNKS_PER_BODY + u) for u in slots]
                rows = [chunk_queries(off) for off, _, _ in nxt]
                for u in slots:
                    finish(u, *done[u])
                for u in slots:
                    probs(u)
                for u in slots:
                    s_ref[u] = scores(rows[u], nxt[u][1])
                return 0
            lax.fori_loop(0, (n + CHUNKS_PER_BODY - 1) // CHUNKS_PER_BODY + 1, chunk_group, 0)

    def combine(pos_ref):
        step = t - (NSTEPS + 1)

        n_sub = STEP_ROWS // MB_BLOCK
        for jj in range(n_sub):
            j = step * n_sub + jj
            rs = slice(jj * MB_BLOCK, (jj + 1) * MB_BLOCK)
            win = pl.ds(pl.multiple_of(jnp.maximum(j - 1, 0) * MB_BLOCK, MB_BLOCK), 2 * MB_BLOCK)
            s = lax.dot_general(queries(qf_ref[rs, :]), k_ref[win, :], contract_last,
                                preferred_element_type=F32)
            picked = jnp.sum(jnp.where(lane == head, prev_ref[rs, :], 0.0),
                             axis=1, keepdims=True) > 0.0
            bias_lo, bias_hi = bprev_ref[...], bown_ref[...]
            if jj == 0:
                first = step == 0
                bias_lo = jnp.where(first, bown_ref[...], bias_lo)
                bias_hi = jnp.where(first, NEG_INF, bias_hi)
                picked = jnp.logical_or(picked, first)
            s_lo = jnp.where(picked, s[:, :MB_BLOCK] + bias_lo, NEG_INF)
            s_hi = s[:, MB_BLOCK:] + bias_hi
            m_d = jnp.maximum(jnp.max(s_lo, axis=1, keepdims=True), jnp.max(s_hi, axis=1, keepdims=True))
            p = jnp.concatenate([jnp.exp(s_lo - m_d), jnp.exp(s_hi - m_d)], axis=1).astype(BF16)
            acc = jnp.dot(p, values(v_ref[win, :]), preferred_element_type=F32)

            for a in range(jj * MB_BLOCK, (jj + 1) * MB_BLOCK):
                for r in range(MB_TOPK):
                    stage_ref[r, a:a + 1, :] = buf_ref[pl.ds(pos_ref[r * STEP_ROWS + a], 1), :]

            parts = [stage_ref[r, rs, :] for r in range(MB_TOPK)]
            ms = [jnp.sum(jnp.where(is_m_lane, part, 0.0), axis=1, keepdims=True) for part in parts]
            m_all = m_d
            for m_r in ms:
                m_all = jnp.maximum(m_all, m_r)
            total = jnp.exp(m_d - m_all) * acc
            for m_r, part in zip(ms, parts):
                total = total + jnp.exp(m_r - m_all) * part
            den = jnp.sum(jnp.where(is_l_lane, total, 0.0), axis=1, keepdims=True)
            out = (total / den).astype(BF16)
            hold_rows = pl.ds(pl.multiple_of(j * MB_BLOCK, MB_BLOCK), MB_BLOCK)
            take_parked = jnp.logical_and(lane < MB_HEAD_DIM, hh == 1)
            merged = jnp.where(take_parked, hold_ref[hold_rows, :], out)
            hold_ref[hold_rows, :] = merged
            o_ref[rs, :] = merged

    steps_per_head = 2 * NSTEPS
    use = head * steps_per_head + jnp.where(t < NSTEPS, t, t - 1)
    pos_bufs = (pos_even_ref, pos_odd_ref)

    def pos_copy(u, parity):
        block = (u // steps_per_head) * NSTEPS + u % NSTEPS
        src = pos_hbm.at[pl.ds(pl.multiple_of(block * POS_BLOCK, POS_BLOCK), POS_BLOCK)]
        return pltpu.make_async_copy(src, pos_bufs[parity], pos_sem.at[parity])

    for parity in range(2):
        @pl.when(jnp.logical_and(t != NSTEPS, use % 2 == parity))
        def _positions_step(parity=parity):
            @pl.when(use == 0)
            def _first():
                pos_copy(use, parity).start()

            @pl.when(use + 1 < MB_HEADS * steps_per_head)
            def _prefetch():
                pos_copy(use + 1, 1 - parity).start()

            pos_copy(use, parity).wait()

            @pl.when(t < NSTEPS)
            def _dispatch():
                dispatch(pos_bufs[parity])

            @pl.when(t > NSTEPS)
            def _combine():
                combine(pos_bufs[parity])


def _moba_attention(chunk_meta, c31, qf, pos, k, v, prevc, bias_own, bias_prev):
    def seq_step(t):
        return jnp.where(t < NSTEPS, t, jnp.maximum(t - (NSTEPS + 1), 0) + jnp.where(t == NSTEPS, NSTEPS - 1, 0))

    def combine_step(t):
        return jnp.maximum(t - (NSTEPS + 1), 0)

    pair_resident = pl.BlockSpec((SEQ, LANES), lambda h, t, *_: (0, h // 2), pipeline_mode=pl.Buffered(1))
    bias = pl.BlockSpec((None, MB_BLOCK, MB_BLOCK), lambda h, t, *_: (h, 0, 0))
    return pl.pallas_call(
        _moba_kernel,
        grid_spec=pltpu.PrefetchScalarGridSpec(
            num_scalar_prefetch=5,
            grid=(MB_HEADS, 2 * NSTEPS + 1),
            in_specs=[
                pl.BlockSpec((STEP_ROWS, LANES), lambda h, t, *_: (seq_step(t), h // 2)),
                pl.BlockSpec(memory_space=pl.ANY),
                pair_resident, pair_resident,
                pl.BlockSpec((STEP_ROWS, LANES), lambda h, t, *_: (combine_step(t), 0)),
                bias, bias,
            ],
            out_specs=pl.BlockSpec(
                (STEP_ROWS, LANES),
                lambda h, t, *_: (jnp.where(h % 2 == 1, combine_step(t), 0), h // 2)),
            scratch_shapes=[pltpu.VMEM((BUF_ROWS, LANES), F32),
                            pltpu.VMEM((MB_TOPK, STEP_ROWS, LANES), F32),
                            pltpu.VMEM((SEQ, LANES), BF16),
                            pltpu.VMEM((CHUNKS_PER_BODY, MB_BLOCK, MB_BLOCK), F32),
                            pltpu.VMEM((CHUNKS_PER_BODY, MB_BLOCK, MB_BLOCK), BF16),
                            pltpu.VMEM((CHUNKS_PER_BODY, MB_BLOCK, LANES), F32),
                            pltpu.SMEM((POS_BLOCK,), jnp.int32), pltpu.SMEM((POS_BLOCK,), jnp.int32),
                            pltpu.SemaphoreType.DMA((2,))],
        ),
        out_shape=jax.ShapeDtypeStruct((SEQ, D_MODEL), BF16),
        compiler_params=_params("arbitrary", "arbitrary"),
        name="moba_attention",
    )(*chunk_meta, c31, qf, pos, k, v, prevc, bias_own, bias_prev)


def _proj_kernel(x_ref, a_ref, vec_ref, w_ref, o_ref):
    y = jnp.dot(a_ref[...], w_ref[...], preferred_element_type=F32)
    o_ref[...] = x_ref[...] + vec_ref[3:4, :] * y


def _out_proj(x, a, vec, w_o):
    row = pl.BlockSpec((PROJ_TM, D_MODEL), lambda i: (i, 0))
    return pl.pallas_call(
        _proj_kernel,
        grid=(SEQ // PROJ_TM,),
        in_specs=[row, row, _resident((8, D_MODEL)), _resident((D_MODEL, D_MODEL))],
        out_specs=row,
        out_shape=jax.ShapeDtypeStruct((SEQ, D_MODEL), F32),
        compiler_params=_params("parallel"),
        name="moba_out_proj",
    )(x, a, vec, w_o)


def _t5_bucket_np(rel):
    n = np.maximum(rel, 0)
    nf = np.maximum(n, T5_MAX_EXACT).astype(np.float32)
    large = T5_MAX_EXACT + (np.log(nf / np.float32(T5_MAX_EXACT))
                            / np.float32(math.log(T5_MAX_DISTANCE / T5_MAX_EXACT))
                            * np.float32(T5_NUM_BUCKETS - T5_MAX_EXACT)).astype(np.int32)
    large = np.minimum(large, T5_NUM_BUCKETS - 1)
    return np.where(n < T5_MAX_EXACT, n, large)


def _moba_bias_tables(rel_bias):
    a = np.arange(MB_BLOCK)
    rel_own = a[:, None] - a[None, :]
    table = rel_bias.astype(F32).T

    def expand(rel):
        bucket = jnp.asarray(_t5_bucket_np(rel).reshape(-1).astype(np.int32))
        onehot = (bucket[None, :] == jnp.arange(T5_NUM_BUCKETS)[:, None]).astype(F32)
        return jnp.dot(table, onehot, precision=lax.Precision.HIGHEST).reshape(
            MB_HEADS, MB_BLOCK, MB_BLOCK)

    b_own = jnp.where(jnp.asarray(rel_own >= 0)[None], expand(rel_own), NEG_INF)
    return b_own, expand(rel_own + MB_BLOCK), table[:, T5_NUM_BUCKETS - 1]


def _chunk_tables(cnt, start):
    nch = (cnt + MB_BLOCK - 1) // MB_BLOCK
    cum = jnp.cumsum(nch, axis=1)
    i = jnp.arange(MAX_CHUNKS)
    blk = jnp.sum(cum[:, None, :] <= i[None, :, None], axis=2)
    blk = jnp.minimum(blk, N_KBLOCKS - 1)
    onehot = blk[:, :, None] == jnp.arange(N_KBLOCKS)[None, None, :]

    def pick(x):
        return jnp.sum(jnp.where(onehot, x[:, None, :], 0), axis=2)

    def flat(x):
        return x.reshape(-1).astype(jnp.int32)
    within = i[None, :] - pick(cum - nch)
    off = pick(start) + within * MB_BLOCK
    valid = jnp.clip(pick(cnt) - within * MB_BLOCK, 0, MB_BLOCK)
    return flat(off), flat(blk), flat(valid), flat(cum[:, -1])


def _moba(xs, vec, w_qkv, w_o, rel_bias):
    qf, k, v, selt, prevt, cnt = _qkv(xs, vec, w_qkv)
    cnt = cnt[:, 0].astype(jnp.int32).reshape(MB_HEADS, N_KBLOCKS)
    padded = (cnt + ROUTE_ALIGN - 1) // ROUTE_ALIGN * ROUTE_ALIGN
    start = jnp.cumsum(padded, axis=1) - padded
    base = jnp.broadcast_to(start.reshape(-1, 1).astype(F32), (MB_HEADS * N_KBLOCKS, LANES))
    pos = _route(selt, base).reshape(MB_HEADS, MB_TOPK, NSTEPS, STEP_ROWS)
    pos = pos.transpose(0, 2, 1, 3).reshape(-1)
    b_own, b_prev, c31 = _moba_bias_tables(rel_bias)
    prevc = jnp.pad(prevt.T, ((0, 0), (0, LANES - MB_HEADS)))
    a = _moba_attention(_chunk_tables(cnt, start), c31, qf, pos, k, v, prevc, b_own, b_prev)
    return _out_proj(xs, a, vec, w_o)


def _vec(norm_g, mod3):
    return jnp.concatenate([norm_g[None, :], mod3, jnp.zeros((4, D_MODEL), F32)], axis=0)


def kernel(x, c, rel_bias, mod_w, mod_b, norm_g, ffn_w_in, ffn_w_out, gmlp_w_in, gmlp_v_norm,
           gmlp_w_s, gmlp_b_s, gmlp_w_out, moba_w_qkv, moba_w_o, final_norm):
    assert x.shape == (1, SEQ, D_MODEL)
    xs = x.reshape(SEQ, D_MODEL)
    mod = _modulation(c.reshape(D_MODEL, 1), mod_w, mod_b).reshape(DEPTH, N_SUBLAYERS, 3, D_MODEL)
    fin = final_norm.reshape(1, D_MODEL)
    w_in_all, w_out_all = ffn_w_in.astype(BF16), ffn_w_out.astype(BF16)

    for i in range(DEPTH):
        li = i // 2
        xs = _ffn(xs, _vec(norm_g[i, 0], mod[i, 0]), w_in_all, w_out_all, i, 0, fin, False)
        vec = _vec(norm_g[i, 1], mod[i, 1])
        if i % 2 == 0:
            bias = jnp.repeat(gmlp_b_s[li].T, GM_GROUP_DIM, axis=1)
            xs = _gmlp(xs, vec, gmlp_w_in[li].astype(BF16), gmlp_v_norm[li].reshape(1, GM_D),
                       gmlp_w_s[li].astype(BF16), bias, gmlp_w_out[li].astype(BF16))
        else:
            xs = _moba(xs, vec, moba_w_qkv[li].astype(BF16), moba_w_o[li].astype(BF16), rel_bias)
        xs = _ffn(xs, _vec(norm_g[i, 2], mod[i, 2]), w_in_all, w_out_all, i, 1, fin, i == DEPTH - 1)
    return xs.reshape(1, SEQ, D_MODEL)
```

```python
import functools
import math

import numpy as np
import jax
import jax.numpy as jnp
from jax import lax
from jax.experimental import pallas as pl
from jax.experimental.pallas import tpu as pltpu

F32 = jnp.float32
BF16 = jnp.bfloat16

D_MODEL = 1024
SEQ = 16384
DEPTH = 2
N_SUBLAYERS = 3
RMS_EPS = 1e-6
D_FF = 2816

GM_D = 3 * D_MODEL
GM_GROUPS = 16
GM_GROUP_DIM = GM_D // GM_GROUPS
GM_CHUNK = 128
GM_PAIR = 2 * GM_GROUP_DIM

MB_HEADS = 16
MB_HEAD_DIM = 64
MB_BLOCK = 256
MB_TOPK = 3
MB_SCALE = MB_HEAD_DIM ** -0.5
N_KBLOCKS = SEQ // MB_BLOCK

T5_NUM_BUCKETS = 32
T5_MAX_EXACT = 16
T5_MAX_DISTANCE = 128
NEG_INF = -1e30
BELOW_NEG_INF = -3e38

LANES = 128
VMEM_LIMIT = 56 * 1024 * 1024

MOD_TN = 1536
FFN_TM = 512
FFN_CHUNK = 1408
GM_TM = 256
GM_VCHUNK = 768
QKV_TM = 512

ROUTE_T = 512
ROUTE_ALIGN = 8
STEP_ROWS = 2048
NSTEPS = SEQ // STEP_ROWS
BUF_DATA_ROWS = MB_TOPK * SEQ + N_KBLOCKS * ROUTE_ALIGN
BUF_ROWS = BUF_DATA_ROWS + 3 * MB_BLOCK
NULL_ROW = BUF_ROWS - ROUTE_ALIGN
ROUTE_NONE = 1e9
MAX_CHUNKS = MB_TOPK * SEQ // MB_BLOCK + N_KBLOCKS
POS_BLOCK = MB_TOPK * STEP_ROWS
CHUNKS_PER_BODY = 8


def _resident(shape):
    nd = len(shape)
    return pl.BlockSpec(shape, lambda *_: (0,) * nd, pipeline_mode=pl.Buffered(1))


def _params(*sem):
    return pltpu.CompilerParams(dimension_semantics=sem, vmem_limit_bytes=VMEM_LIMIT)


def _rms(x):
    return x * lax.rsqrt(jnp.mean(x * x, axis=-1, keepdims=True) + RMS_EPS)


def _adaln(x, vec_ref):
    y = _rms(x) * vec_ref[0:1, :]
    return y * (1.0 + vec_ref[2:3, :]) + vec_ref[1:2, :]


def _gelu(x):
    return 0.5 * x * (1.0 + lax.erf(x * np.float32(math.sqrt(0.5))))


def _mod_kernel(c_ref, w_ref, b_ref, o_ref):
    c = c_ref[...]
    c_act = c * jax.nn.sigmoid(c)
    o_ref[...] = jnp.sum(c_act * w_ref[...], axis=0, keepdims=True) + b_ref[...]


def _modulation(c_col, mod_w, mod_b):
    n = mod_w.shape[-1]
    return pl.pallas_call(
        _mod_kernel,
        grid=(DEPTH, n // MOD_TN),
        in_specs=[
            pl.BlockSpec((D_MODEL, 1), lambda i, j: (0, 0)),
            pl.BlockSpec((None, D_MODEL, MOD_TN), lambda i, j: (i, 0, j)),
            pl.BlockSpec((None, 1, MOD_TN), lambda i, j: (i, 0, j)),
        ],
        out_specs=pl.BlockSpec((None, 1, MOD_TN), lambda i, j: (i, 0, j)),
        out_shape=jax.ShapeDtypeStruct((DEPTH, 1, n), F32),
        compiler_params=_params("arbitrary", "arbitrary"),
        name="modulation",
    )(c_col, mod_w, mod_b.reshape(DEPTH, 1, n))


def _ffn_kernel(*refs, final, mixed):
    if mixed:
        x_ref, a_ref, mixvec_ref, wo_ref, vec_ref, win_ref, wout_ref, fin_ref, o_ref, act_ref = refs
        x = x_ref[...] + mixvec_ref[3:4, :] * jnp.dot(a_ref[...], wo_ref[...],
                                                      preferred_element_type=F32)
    else:
        x_ref, vec_ref, win_ref, wout_ref, fin_ref, o_ref, act_ref = refs
        x = x_ref[...]
    h = _adaln(x, vec_ref).astype(BF16)
    for c in range(D_FF // FFN_CHUNK):
        lo = c * FFN_CHUNK
        g = jnp.dot(h, win_ref[:, lo:lo + FFN_CHUNK], preferred_element_type=F32)
        u = jnp.dot(h, win_ref[:, D_FF + lo:D_FF + lo + FFN_CHUNK], preferred_element_type=F32)
        act_ref[:, lo:lo + FFN_CHUNK] = (g * jax.nn.sigmoid(g) * u).astype(BF16)
    y = jnp.dot(act_ref[...], wout_ref[...], preferred_element_type=F32)
    out = x + (0.5 * vec_ref[3:4, :]) * y
    if final:
        out = _rms(out) * fin_ref[...]
    o_ref[...] = out


def _ffn(x, vec, w_in_all, w_out_all, layer, which, fin, final, mixer=None):
    def one_matrix(rows, cols):
        return pl.BlockSpec((None, None, rows, cols), lambda i: (layer, which, 0, 0),
                            pipeline_mode=pl.Buffered(1))
    row = pl.BlockSpec((FFN_TM, D_MODEL), lambda i: (i, 0))
    mixed = mixer is not None
    mix_specs = [row, _resident((8, D_MODEL)), _resident((D_MODEL, D_MODEL))] if mixed else []
    return pl.pallas_call(
        functools.partial(_ffn_kernel, final=final, mixed=mixed),
        grid=(SEQ // FFN_TM,),
        in_specs=[row, *mix_specs,
                  _resident((8, D_MODEL)),
                  one_matrix(D_MODEL, 2 * D_FF),
                  one_matrix(D_FF, D_MODEL),
                  _resident((1, D_MODEL))],
        out_specs=row,
        out_shape=jax.ShapeDtypeStruct((SEQ, D_MODEL), F32),
        scratch_shapes=[pltpu.VMEM((FFN_TM, D_FF), BF16)],
        compiler_params=_params("parallel"),
        name="ffn_final" if final else "ffn",
    )(x, *(mixer or ()), vec, w_in_all, w_out_all, fin)


def _gmlp_kernel(x_ref, vec_ref, win_ref, vnorm_ref, ws_ref, bias_ref, wout_ref,
                 o_ref, v_ref, g_ref):
    x = x_ref[...]
    h = _adaln(x, vec_ref).astype(BF16)

    ssq = jnp.zeros((GM_TM, 1), F32)
    for c in range(GM_D // GM_VCHUNK):
        lo = c * GM_VCHUNK
        v = _gelu(jnp.dot(h, win_ref[:, GM_D + lo:GM_D + lo + GM_VCHUNK],
                          preferred_element_type=F32))
        v_ref[:, lo:lo + GM_VCHUNK] = v
        ssq = ssq + jnp.sum(v * v, axis=-1, keepdims=True)
    r = lax.rsqrt(ssq * (1.0 / GM_D) + RMS_EPS)

    row = lax.broadcasted_iota(jnp.int32, (GM_CHUNK, GM_CHUNK), 0)
    col = lax.broadcasted_iota(jnp.int32, (GM_CHUNK, GM_CHUNK), 1)
    causal = row >= col
    low_half = lax.broadcasted_iota(jnp.int32, (GM_CHUNK, LANES), 1) < (GM_GROUP_DIM - LANES)

    for p in range(GM_GROUPS // 2):
        lo = p * GM_PAIR
        vn = (v_ref[:, lo:lo + GM_PAIR] * r * vnorm_ref[:, lo:lo + GM_PAIR]).astype(BF16)
        u = _gelu(jnp.dot(h, win_ref[:, lo:lo + GM_PAIR], preferred_element_type=F32))
        w0 = jnp.where(causal, ws_ref[2 * p], jnp.zeros((), BF16))
        w1 = jnp.where(causal, ws_ref[2 * p + 1], jnp.zeros((), BF16))
        for c in range(GM_TM // GM_CHUNK):
            rows = slice(c * GM_CHUNK, (c + 1) * GM_CHUNK)
            vc = vn[rows]
            a = jnp.dot(w0, vc[:, :2 * LANES], preferred_element_type=F32)
            b = jnp.dot(w1, vc[:, LANES:], preferred_element_type=F32)
            mid = jnp.where(low_half, a[:, LANES:], b[:, :LANES])
            sv = jnp.concatenate([a[:, :LANES], mid, b[:, LANES:]], axis=1)
            sv = sv + bias_ref[:, lo:lo + GM_PAIR]
            g_ref[rows, lo:lo + GM_PAIR] = (u[rows] * sv).astype(BF16)

    y = jnp.dot(g_ref[...], wout_ref[...], preferred_element_type=F32)
    o_ref[...] = x + vec_ref[3:4, :] * y


def _gmlp(x, vec, w_in, v_norm, w_s, bias, w_out):
    return pl.pallas_call(
        _gmlp_kernel,
        grid=(SEQ // GM_TM,),
        in_specs=[
            pl.BlockSpec((GM_TM, D_MODEL), lambda i: (i, 0)),
            _resident((8, D_MODEL)),
            _resident((D_MODEL, 2 * GM_D)),
            _resident((1, GM_D)),
            _resident((GM_GROUPS, GM_CHUNK, GM_CHUNK)),
            _resident((GM_CHUNK, GM_D)),
            _resident((GM_D, D_MODEL)),
        ],
        out_specs=pl.BlockSpec((GM_TM, D_MODEL), lambda i: (i, 0)),
        out_shape=jax.ShapeDtypeStruct((SEQ, D_MODEL), F32),
        scratch_shapes=[pltpu.VMEM((GM_TM, GM_D), F32), pltpu.VMEM((GM_TM, GM_D), BF16)],
        compiler_params=_params("parallel"),
        name="gmlp",
    )(x, vec, w_in, v_norm, w_s, bias, w_out)


def _qkv_kernel(x_ref, vec_ref, w_ref, q_ref, k_ref, v_ref, selt_ref, prevt_ref, cnt_ref,
                kmean_ref):
    i = pl.program_id(0)

    @pl.when(i == 0)
    def _init():
        kmean_ref[...] = jnp.zeros_like(kmean_ref)
        cnt_ref[...] = jnp.zeros_like(cnt_ref)

    h = _adaln(x_ref[...], vec_ref).astype(BF16)
    q = jnp.dot(h, w_ref[:, :D_MODEL], preferred_element_type=F32) * MB_SCALE
    q_ref[...] = q
    k = jnp.dot(h, w_ref[:, D_MODEL:2 * D_MODEL], preferred_element_type=F32)
    k_ref[...] = k.astype(BF16)
    nb = QKV_TM // MB_BLOCK
    for b in range(nb):
        kmean_ref[pl.ds(nb * i + b, 1), :] = jnp.mean(
            k[b * MB_BLOCK:(b + 1) * MB_BLOCK], axis=0, keepdims=True)
    v = jnp.dot(h, w_ref[:, 2 * D_MODEL:], preferred_element_type=F32)
    v_ref[...] = v.astype(BF16)

    lane = lax.broadcasted_iota(jnp.int32, (1, LANES), 1)
    blk = lax.broadcasted_iota(jnp.int32, (N_KBLOCKS, MB_BLOCK), 0)
    contract_last = (((1,), (1,)), ((), ()))
    for b in range(nb):
        j = nb * i + b
        cols = slice(b * MB_BLOCK, (b + 1) * MB_BLOCK)
        for pair in range(MB_HEADS // 2):
            km = kmean_ref[:, pair * LANES:(pair + 1) * LANES]
            qp = q[cols, pair * LANES:(pair + 1) * LANES]
            for hh in range(2):
                head = 2 * pair + hh
                km_h = jnp.where((lane // MB_HEAD_DIM) == hh, km, 0.0)
                g = lax.dot_general(km_h, qp, contract_last, preferred_element_type=F32)
                g = jnp.where(blk < j, g, NEG_INF)
                sel = jnp.zeros((N_KBLOCKS, MB_BLOCK), F32)
                for r in range(MB_TOPK):
                    m = jnp.max(g, axis=0, keepdims=True)
                    first = jnp.min(jnp.where(g == m, blk, N_KBLOCKS), axis=0, keepdims=True)
                    hit = blk == first
                    sel = jnp.where(jnp.logical_and(hit, r < j), 1.0, sel)
                    g = jnp.where(hit, BELOW_NEG_INF, g)
                prevt_ref[head:head + 1, cols] = jnp.max(
                    jnp.where(blk == j - 1, sel, 0.0), axis=0, keepdims=True)
                routed = jnp.where(blk < j - 1, sel, 0.0)
                selt_ref[head * N_KBLOCKS:(head + 1) * N_KBLOCKS, cols] = routed.astype(BF16)
                rows = slice(head * N_KBLOCKS, (head + 1) * N_KBLOCKS)
                cnt_ref[rows, :] = cnt_ref[rows, :] + jnp.sum(routed, axis=1, keepdims=True)


def _qkv(x, vec, w_qkv):
    row = pl.BlockSpec((QKV_TM, D_MODEL), lambda i: (i, 0))
    n_items = MB_HEADS * N_KBLOCKS
    return pl.pallas_call(
        _qkv_kernel,
        grid=(SEQ // QKV_TM,),
        in_specs=[row, _resident((8, D_MODEL)), _resident((D_MODEL, 3 * D_MODEL))],
        out_specs=[row, row, row,
                   pl.BlockSpec((n_items, QKV_TM), lambda i: (0, i)),
                   pl.BlockSpec((MB_HEADS, QKV_TM), lambda i: (0, i)),
                   pl.BlockSpec((n_items, LANES), lambda i: (0, 0))],
        out_shape=[jax.ShapeDtypeStruct((SEQ, D_MODEL), F32),
                   jax.ShapeDtypeStruct((SEQ, D_MODEL), BF16),
                   jax.ShapeDtypeStruct((SEQ, D_MODEL), BF16),
                   jax.ShapeDtypeStruct((n_items, SEQ), BF16),
                   jax.ShapeDtypeStruct((MB_HEADS, SEQ), F32),
                   jax.ShapeDtypeStruct((n_items, LANES), F32)],
        scratch_shapes=[pltpu.VMEM((N_KBLOCKS, D_MODEL), F32)],
        compiler_params=_params("arbitrary"),
        name="moba_qkv_gate",
    )(x, vec, w_qkv)


def _route_kernel(selt_ref, base_ref, pos_ref, carry_ref):
    i = pl.program_id(0)

    @pl.when(i == 0)
    def _init():
        carry_ref[...] = jnp.zeros_like(carry_ref)

    sel = selt_ref[...]
    before = (lax.broadcasted_iota(jnp.int32, (ROUTE_T, ROUTE_T), 0)
              < lax.broadcasted_iota(jnp.int32, (ROUTE_T, ROUTE_T), 1))
    rank = jnp.dot(sel, jnp.where(before, 1.0, 0.0).astype(BF16), preferred_element_type=F32)
    offset = base_ref[...] + carry_ref[...]
    rank = rank + jnp.concatenate([offset] * (ROUTE_T // LANES), axis=1)
    carry_ref[...] = carry_ref[...] + jnp.sum(sel.astype(F32), axis=1, keepdims=True)
    val = jnp.where(sel > 0, rank, ROUTE_NONE)
    for head in range(MB_HEADS):
        vh = val[head * N_KBLOCKS:(head + 1) * N_KBLOCKS]
        for r in range(MB_TOPK):
            m = jnp.min(vh, axis=0, keepdims=True)
            slot = head * MB_TOPK + r
            pos_ref[slot:slot + 1, :] = jnp.where(m < ROUTE_NONE, m, float(NULL_ROW)).astype(jnp.int32)
            vh = jnp.where(vh == m, ROUTE_NONE, vh)


def _route(selt, base):
    n_items = MB_HEADS * N_KBLOCKS
    return pl.pallas_call(
        _route_kernel,
        grid=(SEQ // ROUTE_T,),
        in_specs=[pl.BlockSpec((n_items, ROUTE_T), lambda i: (0, i)),
                  pl.BlockSpec((n_items, LANES), lambda i: (0, 0))],
        out_specs=pl.BlockSpec((MB_HEADS * MB_TOPK, ROUTE_T), lambda i: (0, i)),
        out_shape=jax.ShapeDtypeStruct((MB_HEADS * MB_TOPK, SEQ), jnp.int32),
        scratch_shapes=[pltpu.VMEM((n_items, LANES), F32)],
        compiler_params=_params("arbitrary"),
        name="moba_route",
    )(selt, base)


def _moba_kernel(choff_ref, chblk_ref, chvalid_ref, nch_ref, c31_ref,
                 qf_ref, pos_hbm, k_ref, v_ref, prev_ref, bown_ref, bprev_ref,
                 o_ref, buf_ref, stage_ref, hold_ref, s_ref, p_ref, m_ref,
                 pos_even_ref, pos_odd_ref, pos_sem):
    head = pl.program_id(0)
    t = pl.program_id(1)
    hh = head % 2
    lane = lax.broadcasted_iota(jnp.int32, (1, LANES), 1)
    in_head = (lane // MB_HEAD_DIM) == hh
    stat0 = (1 - hh) * MB_HEAD_DIM
    is_m_lane = lane == stat0
    is_l_lane = lane == stat0 + 1
    contract_last = (((1,), (1,)), ((), ()))
    ones_bf = jnp.ones((), BF16)

    def queries(rows):
        return jnp.where(in_head, rows, 0.0).astype(BF16)

    def values(rows):
        return jnp.where(in_head, rows, ones_bf)

    @pl.when(jnp.logical_and(t == 0, head == 0))
    def _init():
        def zero(c, _):
            buf_ref[pl.ds(pl.multiple_of(c * MB_BLOCK, MB_BLOCK), MB_BLOCK), :] = (
                jnp.zeros((MB_BLOCK, LANES), F32))
            return 0
        lax.fori_loop(0, BUF_ROWS // MB_BLOCK, zero, 0)

        def zero_hold(c, _):
            hold_ref[pl.ds(pl.multiple_of(c * MB_BLOCK, MB_BLOCK), MB_BLOCK), :] = (
                jnp.zeros((MB_BLOCK, LANES), BF16))
            return 0
        lax.fori_loop(0, SEQ // MB_BLOCK, zero_hold, 0)

    def dispatch(pos_ref):
        for a in range(STEP_ROWS):
            row = qf_ref[a:a + 1, :]
            for r in range(MB_TOPK):
                buf_ref[pl.ds(pos_ref[r * STEP_ROWS + a], 1), :] = row

    @pl.when(t == NSTEPS)
    def _routed():
        buf_ref[NULL_ROW:NULL_ROW + ROUTE_ALIGN, :] = jnp.where(
            is_m_lane, NEG_INF, jnp.zeros((ROUTE_ALIGN, LANES), F32))
        n = nch_ref[head]
        far_bias = c31_ref[head]
        row_id = lax.broadcasted_iota(jnp.int32, (MB_BLOCK, LANES), 0)

        def meta(i):
            idx = head * MAX_CHUNKS + jnp.clip(i, 0, n - 1)
            off = pl.multiple_of(choff_ref[idx], ROUTE_ALIGN)
            blk = pl.multiple_of(chblk_ref[idx] * MB_BLOCK, MB_BLOCK)
            return off, blk, jnp.where(jnp.logical_and(i >= 0, i < n), chvalid_ref[idx], 0)

        def chunk_queries(off):
            return queries(buf_ref[pl.ds(off, MB_BLOCK), :])

        def scores(qh, blk):
            return lax.dot_general(qh, k_ref[pl.ds(blk, MB_BLOCK), :], contract_last,
                                   preferred_element_type=F32)

        def probs(u):
            s = s_ref[u]
            m = jnp.max(s, axis=1, keepdims=True)
            p_ref[u] = jnp.exp(s - m).astype(BF16)
            m_ref[u] = jnp.broadcast_to(m, (MB_BLOCK, LANES))

        def finish(u, off, blk, valid):
            acc = jnp.dot(p_ref[u], values(v_ref[pl.ds(blk, MB_BLOCK), :]), preferred_element_type=F32)
            new = jnp.where(is_m_lane, m_ref[u] + far_bias, acc)
            pltpu.store(buf_ref.at[pl.ds(off, MB_BLOCK), :], new, mask=row_id < valid)

        @pl.when(n > 0)
        def _chunks():
            slots = range(CHUNKS_PER_BODY)
            p_ref[...] = jnp.zeros_like(p_ref)
            m_ref[...] = jnp.zeros_like(m_ref)
            for u in slots:
                off, blk, _ = meta(u)
                s_ref[u] = scores(chunk_queries(off), blk)

            def chunk_group(c, _):
                i = CHUNKS_PER_BODY * c
                nxt = [meta(i + CHUNKS_PER_BODY + u) for u in slots]
                done = [meta(i - CHUNKS_PER_BODY + u) for u in slots]
                rows = [chunk_queries(off) for off, _, _ in nxt]
                for u in slots:
                    finish(u, *done[u])
                for u in slots:
                    probs(u)
                for u in slots:
                    s_ref[u] = scores(rows[u], nxt[u][1])
                return 0
            lax.fori_loop(0, (n + CHUNKS_PER_BODY - 1) // CHUNKS_PER_BODY + 1, chunk_group, 0)

    def combine(pos_ref):
        step = t - (NSTEPS + 1)

        n_sub = STEP_ROWS // MB_BLOCK
        for jj in range(n_sub):
            j = step * n_sub + jj
            rs = slice(jj * MB_BLOCK, (jj + 1) * MB_BLOCK)
            win = pl.ds(pl.multiple_of(jnp.maximum(j - 1, 0) * MB_BLOCK, MB_BLOCK), 2 * MB_BLOCK)
            s = lax.dot_general(queries(qf_ref[rs, :]), k_ref[win, :], contract_last,
                                preferred_element_type=F32)
            picked = jnp.sum(jnp.where(lane == head, prev_ref[rs, :], 0.0),
                             axis=1, keepdims=True) > 0.0
            bias_lo, bias_hi = bprev_ref[...], bown_ref[...]
            if jj == 0:
                first = step == 0
                bias_lo = jnp.where(first, bown_ref[...], bias_lo)
                bias_hi = jnp.where(first, NEG_INF, bias_hi)
                picked = jnp.logical_or(picked, first)
            s_lo = jnp.where(picked, s[:, :MB_BLOCK] + bias_lo, NEG_INF)
            s_hi = s[:, MB_BLOCK:] + bias_hi
            m_d = jnp.maximum(jnp.max(s_lo, axis=1, keepdims=True), jnp.max(s_hi, axis=1, keepdims=True))
            p = jnp.concatenate([jnp.exp(s_lo - m_d), jnp.exp(s_hi - m_d)], axis=1).astype(BF16)
            acc = jnp.dot(p, values(v_ref[win, :]), preferred_element_type=F32)

            for a in range(jj * MB_BLOCK, (jj + 1) * MB_BLOCK):
                for r in range(MB_TOPK):
                    stage_ref[r, a:a + 1, :] = buf_ref[pl.ds(pos_ref[r * STEP_ROWS + a], 1), :]

            parts = [stage_ref[r, rs, :] for r in range(MB_TOPK)]
            ms = [jnp.sum(jnp.where(is_m_lane, part, 0.0), axis=1, keepdims=True) for part in parts]
            m_all = m_d
            for m_r in ms:
                m_all = jnp.maximum(m_all, m_r)
            total = jnp.exp(m_d - m_all) * acc
            for m_r, part in zip(ms, parts):
                total = total + jnp.exp(m_r - m_all) * part
            den = jnp.sum(jnp.where(is_l_lane, total, 0.0), axis=1, keepdims=True)
            out = (total / den).astype(BF16)
            hold_rows = pl.ds(pl.multiple_of(j * MB_BLOCK, MB_BLOCK), MB_BLOCK)
            take_parked = jnp.logical_and(lane < MB_HEAD_DIM, hh == 1)
            merged = jnp.where(take_parked, hold_ref[hold_rows, :], out)
            hold_ref[hold_rows, :] = merged
            o_ref[rs, :] = merged

    steps_per_head = 2 * NSTEPS
    use = head * steps_per_head + jnp.where(t < NSTEPS, t, t - 1)
    pos_bufs = (pos_even_ref, pos_odd_ref)

    def pos_copy(u, parity):
        block = (u // steps_per_head) * NSTEPS + u % NSTEPS
        src = pos_hbm.at[pl.ds(pl.multiple_of(block * POS_BLOCK, POS_BLOCK), POS_BLOCK)]
        return pltpu.make_async_copy(src, pos_bufs[parity], pos_sem.at[parity])

    for parity in range(2):
        @pl.when(jnp.logical_and(t != NSTEPS, use % 2 == parity))
        def _positions_step(parity=parity):
            @pl.when(use == 0)
            def _first():
                pos_copy(use, parity).start()

            @pl.when(use + 1 < MB_HEADS * steps_per_head)
            def _prefetch():
                pos_copy(use + 1, 1 - parity).start()

            pos_copy(use, parity).wait()

            @pl.when(t < NSTEPS)
            def _dispatch():
                dispatch(pos_bufs[parity])

            @pl.when(t > NSTEPS)
            def _combine():
                combine(pos_bufs[parity])


def _moba_attention(chunk_meta, c31, qf, pos, k, v, prevc, bias_own, bias_prev):
    def seq_step(t):
        return jnp.where(t < NSTEPS, t, jnp.maximum(t - (NSTEPS + 1), 0) + jnp.where(t == NSTEPS, NSTEPS - 1, 0))

    def combine_step(t):
        return jnp.maximum(t - (NSTEPS + 1), 0)

    pair_resident = pl.BlockSpec((SEQ, LANES), lambda h, t, *_: (0, h // 2), pipeline_mode=pl.Buffered(1))
    bias = pl.BlockSpec((None, MB_BLOCK, MB_BLOCK), lambda h, t, *_: (h, 0, 0))
    return pl.pallas_call(
        _moba_kernel,
        grid_spec=pltpu.PrefetchScalarGridSpec(
            num_scalar_prefetch=5,
            grid=(MB_HEADS, 2 * NSTEPS + 1),
            in_specs=[
                pl.BlockSpec((STEP_ROWS, LANES), lambda h, t, *_: (seq_step(t), h // 2)),
                pl.BlockSpec(memory_space=pl.ANY),
                pair_resident, pair_resident,
                pl.BlockSpec((STEP_ROWS, LANES), lambda h, t, *_: (combine_step(t), 0)),
                bias, bias,
            ],
            out_specs=pl.BlockSpec(
                (STEP_ROWS, LANES),
                lambda h, t, *_: (jnp.where(h % 2 == 1, combine_step(t), 0), h // 2)),
            scratch_shapes=[pltpu.VMEM((BUF_ROWS, LANES), F32),
                            pltpu.VMEM((MB_TOPK, STEP_ROWS, LANES), F32),
                            pltpu.VMEM((SEQ, LANES), BF16),
                            pltpu.VMEM((CHUNKS_PER_BODY, MB_BLOCK, MB_BLOCK), F32),
                            pltpu.VMEM((CHUNKS_PER_BODY, MB_BLOCK, MB_BLOCK), BF16),
                            pltpu.VMEM((CHUNKS_PER_BODY, MB_BLOCK, LANES), F32),
                            pltpu.SMEM((POS_BLOCK,), jnp.int32), pltpu.SMEM((POS_BLOCK,), jnp.int32),
                            pltpu.SemaphoreType.DMA((2,))],
        ),
        out_shape=jax.ShapeDtypeStruct((SEQ, D_MODEL), BF16),
        compiler_params=_params("arbitrary", "arbitrary"),
        name="moba_attention",
    )(*chunk_meta, c31, qf, pos, k, v, prevc, bias_own, bias_prev)


def _t5_bucket_np(rel):
    n = np.maximum(rel, 0)
    nf = np.maximum(n, T5_MAX_EXACT).astype(np.float32)
    large = T5_MAX_EXACT + (np.log(nf / np.float32(T5_MAX_EXACT))
                            / np.float32(math.log(T5_MAX_DISTANCE / T5_MAX_EXACT))
                            * np.float32(T5_NUM_BUCKETS - T5_MAX_EXACT)).astype(np.int32)
    large = np.minimum(large, T5_NUM_BUCKETS - 1)
    return np.where(n < T5_MAX_EXACT, n, large)


def _moba_bias_tables(rel_bias):
    a = np.arange(MB_BLOCK)
    rel_own = a[:, None] - a[None, :]
    table = rel_bias.astype(F32).T

    def expand(rel):
        bucket = jnp.asarray(_t5_bucket_np(rel).reshape(-1).astype(np.int32))
        onehot = (bucket[None, :] == jnp.arange(T5_NUM_BUCKETS)[:, None]).astype(F32)
        return jnp.dot(table, onehot, precision=lax.Precision.HIGHEST).reshape(
            MB_HEADS, MB_BLOCK, MB_BLOCK)

    b_own = jnp.where(jnp.asarray(rel_own >= 0)[None], expand(rel_own), NEG_INF)
    return b_own, expand(rel_own + MB_BLOCK), table[:, T5_NUM_BUCKETS - 1]


def _chunk_tables(cnt, start):
    nch = (cnt + MB_BLOCK - 1) // MB_BLOCK
    cum = jnp.cumsum(nch, axis=1)
    i = jnp.arange(MAX_CHUNKS)
    blk = jnp.sum(cum[:, None, :] <= i[None, :, None], axis=2)
    blk = jnp.minimum(blk, N_KBLOCKS - 1)
    onehot = blk[:, :, None] == jnp.arange(N_KBLOCKS)[None, None, :]

    def pick(x):
        return jnp.sum(jnp.where(onehot, x[:, None, :], 0), axis=2)

    def flat(x):
        return x.reshape(-1).astype(jnp.int32)
    within = i[None, :] - pick(cum - nch)
    off = pick(start) + within * MB_BLOCK
    valid = jnp.clip(pick(cnt) - within * MB_BLOCK, 0, MB_BLOCK)
    return flat(off), flat(blk), flat(valid), flat(cum[:, -1])


def _moba(xs, vec, w_qkv, rel_bias):
    qf, k, v, selt, prevt, cnt = _qkv(xs, vec, w_qkv)
    cnt = cnt[:, 0].astype(jnp.int32).reshape(MB_HEADS, N_KBLOCKS)
    padded = (cnt + ROUTE_ALIGN - 1) // ROUTE_ALIGN * ROUTE_ALIGN
    start = jnp.cumsum(padded, axis=1) - padded
    base = jnp.broadcast_to(start.reshape(-1, 1).astype(F32), (MB_HEADS * N_KBLOCKS, LANES))
    pos = _route(selt, base).reshape(MB_HEADS, MB_TOPK, NSTEPS, STEP_ROWS)
    pos = pos.transpose(0, 2, 1, 3).reshape(-1)
    b_own, b_prev, c31 = _moba_bias_tables(rel_bias)
    prevc = jnp.pad(prevt.T, ((0, 0), (0, LANES - MB_HEADS)))
    a = _moba_attention(_chunk_tables(cnt, start), c31, qf, pos, k, v, prevc, b_own, b_prev)
    return a


def _vec(norm_g, mod3):
    return jnp.concatenate([norm_g[None, :], mod3, jnp.zeros((4, D_MODEL), F32)], axis=0)


def kernel(x, c, rel_bias, mod_w, mod_b, norm_g, ffn_w_in, ffn_w_out, gmlp_w_in, gmlp_v_norm,
           gmlp_w_s, gmlp_b_s, gmlp_w_out, moba_w_qkv, moba_w_o, final_norm):
    assert x.shape == (1, SEQ, D_MODEL)
    xs = x.reshape(SEQ, D_MODEL)
    mod = _modulation(c.reshape(D_MODEL, 1), mod_w, mod_b).reshape(DEPTH, N_SUBLAYERS, 3, D_MODEL)
    fin = final_norm.reshape(1, D_MODEL)
    w_in_all, w_out_all = ffn_w_in.astype(BF16), ffn_w_out.astype(BF16)

    for i in range(DEPTH):
        li = i // 2
        xs = _ffn(xs, _vec(norm_g[i, 0], mod[i, 0]), w_in_all, w_out_all, i, 0, fin, False)
        vec = _vec(norm_g[i, 1], mod[i, 1])
        if i % 2 == 0:
            bias = jnp.repeat(gmlp_b_s[li].T, GM_GROUP_DIM, axis=1)
            xs = _gmlp(xs, vec, gmlp_w_in[li].astype(BF16), gmlp_v_norm[li].reshape(1, GM_D),
                       gmlp_w_s[li].astype(BF16), bias, gmlp_w_out[li].astype(BF16))
            mixer = None
        else:
            mixer = (_moba(xs, vec, moba_w_qkv[li].astype(BF16), rel_bias), vec,
                     moba_w_o[li].astype(BF16))
        xs = _ffn(xs, _vec(norm_g[i, 2], mod[i, 2]), w_in_all, w_out_all, i, 1, fin, i == DEPTH - 1,
                  mixer)
    return xs.reshape(1, SEQ, D_MODEL)
```

```python
import functools
import math

import numpy as np
import jax
import jax.numpy as jnp
from jax import lax
from jax.experimental import pallas as pl
from jax.experimental.pallas import tpu as pltpu

F32 = jnp.float32
BF16 = jnp.bfloat16

D_MODEL = 1024
SEQ = 16384
DEPTH = 2
N_SUBLAYERS = 3
RMS_EPS = 1e-6
D_FF = 2816

GM_D = 3 * D_MODEL
GM_GROUPS = 16
GM_GROUP_DIM = GM_D // GM_GROUPS
GM_CHUNK = 128
GM_PAIR = 2 * GM_GROUP_DIM

MB_HEADS = 16
MB_HEAD_DIM = 64
MB_BLOCK = 256
MB_TOPK = 3
MB_SCALE = MB_HEAD_DIM ** -0.5
N_KBLOCKS = SEQ // MB_BLOCK

T5_NUM_BUCKETS = 32
T5_MAX_EXACT = 16
T5_MAX_DISTANCE = 128
NEG_INF = -1e30
BELOW_NEG_INF = -3e38

LANES = 128
VMEM_LIMIT = 56 * 1024 * 1024

MOD_TN = 1536
FFN_TM = 512
FFN_CHUNK = 256
GM_TM = 512
GM_VCHUNK = 768
QKV_TM = 512

ROUTE_T = 512
ROUTE_ALIGN = 8
STEP_ROWS = 2048
NSTEPS = SEQ // STEP_ROWS
BUF_DATA_ROWS = MB_TOPK * SEQ + N_KBLOCKS * ROUTE_ALIGN
BUF_ROWS = BUF_DATA_ROWS + 3 * MB_BLOCK
NULL_ROW = BUF_ROWS - ROUTE_ALIGN
ROUTE_NONE = 1e9
MAX_CHUNKS = MB_TOPK * SEQ // MB_BLOCK + N_KBLOCKS
POS_BLOCK = MB_TOPK * STEP_ROWS
CHUNKS_PER_BODY = 8


def _resident(shape):
    nd = len(shape)
    return pl.BlockSpec(shape, lambda *_: (0,) * nd, pipeline_mode=pl.Buffered(1))


def _params(*sem):
    return pltpu.CompilerParams(dimension_semantics=sem, vmem_limit_bytes=VMEM_LIMIT)


def _rms(x):
    return x * lax.rsqrt(jnp.mean(x * x, axis=-1, keepdims=True) + RMS_EPS)


def _adaln(x, vec_ref):
    y = _rms(x) * vec_ref[0:1, :]
    return y * (1.0 + vec_ref[2:3, :]) + vec_ref[1:2, :]


def _gelu(x):
    return 0.5 * x * (1.0 + lax.erf(x * np.float32(math.sqrt(0.5))))


def _mod_kernel(c_ref, w_ref, b_ref, o_ref):
    c = c_ref[...]
    c_act = c * jax.nn.sigmoid(c)
    o_ref[...] = jnp.sum(c_act * w_ref[...], axis=0, keepdims=True) + b_ref[...]


def _modulation(c_col, mod_w, mod_b):
    n = mod_w.shape[-1]
    return pl.pallas_call(
        _mod_kernel,
        grid=(DEPTH, n // MOD_TN),
        in_specs=[
            pl.BlockSpec((D_MODEL, 1), lambda i, j: (0, 0)),
            pl.BlockSpec((None, D_MODEL, MOD_TN), lambda i, j: (i, 0, j)),
            pl.BlockSpec((None, 1, MOD_TN), lambda i, j: (i, 0, j)),
        ],
        out_specs=pl.BlockSpec((None, 1, MOD_TN), lambda i, j: (i, 0, j)),
        out_shape=jax.ShapeDtypeStruct((DEPTH, 1, n), F32),
        compiler_params=_params("arbitrary", "arbitrary"),
        name="modulation",
    )(c_col, mod_w, mod_b.reshape(DEPTH, 1, n))


def _ffn_kernel(*refs, final, mixed):
    if mixed:
        x_ref, a_ref, mixvec_ref, wo_ref, vec_ref, win_ref, wout_ref, fin_ref, o_ref, act_ref = refs
        x = x_ref[...] + mixvec_ref[3:4, :] * jnp.dot(a_ref[...], wo_ref[...],
                                                      preferred_element_type=F32)
    else:
        x_ref, vec_ref, win_ref, wout_ref, fin_ref, o_ref, act_ref = refs
        x = x_ref[...]
    h = _adaln(x, vec_ref).astype(BF16)
    for c in range(D_FF // FFN_CHUNK):
        lo = c * FFN_CHUNK
        g = jnp.dot(h, win_ref[:, lo:lo + FFN_CHUNK], preferred_element_type=F32)
        u = jnp.dot(h, win_ref[:, D_FF + lo:D_FF + lo + FFN_CHUNK], preferred_element_type=F32)
        act_ref[:, lo:lo + FFN_CHUNK] = (g * jax.nn.sigmoid(g) * u).astype(BF16)
    y = jnp.dot(act_ref[...], wout_ref[...], preferred_element_type=F32)
    out = x + (0.5 * vec_ref[3:4, :]) * y
    if final:
        out = _rms(out) * fin_ref[...]
    o_ref[...] = out


def _ffn(x, vec, w_in_all, w_out_all, layer, which, fin, final, mixer=None):
    def one_matrix(rows, cols):
        return pl.BlockSpec((None, None, rows, cols), lambda i: (layer, which, 0, 0),
                            pipeline_mode=pl.Buffered(1))
    row = pl.BlockSpec((FFN_TM, D_MODEL), lambda i: (i, 0))
    mixed = mixer is not None
    mix_specs = [row, _resident((8, D_MODEL)), _resident((D_MODEL, D_MODEL))] if mixed else []
    return pl.pallas_call(
        functools.partial(_ffn_kernel, final=final, mixed=mixed),
        grid=(SEQ // FFN_TM,),
        in_specs=[row, *mix_specs,
                  _resident((8, D_MODEL)),
                  one_matrix(D_MODEL, 2 * D_FF),
                  one_matrix(D_FF, D_MODEL),
                  _resident((1, D_MODEL))],
        out_specs=row,
        out_shape=jax.ShapeDtypeStruct((SEQ, D_MODEL), F32),
        scratch_shapes=[pltpu.VMEM((FFN_TM, D_FF), BF16)],
        compiler_params=_params("parallel"),
        name="ffn_final" if final else "ffn",
    )(x, *(mixer or ()), vec, w_in_all, w_out_all, fin)


def _gmlp_kernel(x_ref, vec_ref, win_ref, vnorm_ref, ws_ref, bias_ref, wout_ref,
                 o_ref, v_ref, g_ref):
    x = x_ref[...]
    h = _adaln(x, vec_ref).astype(BF16)

    ssq = jnp.zeros((GM_TM, 1), F32)
    for c in range(GM_D // GM_VCHUNK):
        lo = c * GM_VCHUNK
        v = _gelu(jnp.dot(h, win_ref[:, GM_D + lo:GM_D + lo + GM_VCHUNK],
                          preferred_element_type=F32))
        v_ref[:, lo:lo + GM_VCHUNK] = v
        ssq = ssq + jnp.sum(v * v, axis=-1, keepdims=True)
    r = lax.rsqrt(ssq * (1.0 / GM_D) + RMS_EPS)

    row = lax.broadcasted_iota(jnp.int32, (GM_CHUNK, GM_CHUNK), 0)
    col = lax.broadcasted_iota(jnp.int32, (GM_CHUNK, GM_CHUNK), 1)
    causal = row >= col
    low_half = lax.broadcasted_iota(jnp.int32, (GM_CHUNK, LANES), 1) < (GM_GROUP_DIM - LANES)

    for p in range(GM_GROUPS // 2):
        lo = p * GM_PAIR
        vn = (v_ref[:, lo:lo + GM_PAIR] * r * vnorm_ref[:, lo:lo + GM_PAIR]).astype(BF16)
        u = _gelu(jnp.dot(h, win_ref[:, lo:lo + GM_PAIR], preferred_element_type=F32))
        w0 = jnp.where(causal, ws_ref[2 * p], jnp.zeros((), BF16))
        w1 = jnp.where(causal, ws_ref[2 * p + 1], jnp.zeros((), BF16))
        for c in range(GM_TM // GM_CHUNK):
            rows = slice(c * GM_CHUNK, (c + 1) * GM_CHUNK)
            vc = vn[rows]
            a = jnp.dot(w0, vc[:, :2 * LANES], preferred_element_type=F32)
            b = jnp.dot(w1, vc[:, LANES:], preferred_element_type=F32)
            mid = jnp.where(low_half, a[:, LANES:], b[:, :LANES])
            sv = jnp.concatenate([a[:, :LANES], mid, b[:, LANES:]], axis=1)
            sv = sv + bias_ref[:, lo:lo + GM_PAIR]
            g_ref[rows, lo:lo + GM_PAIR] = (u[rows] * sv).astype(BF16)

    y = jnp.dot(g_ref[...], wout_ref[...], preferred_element_type=F32)
    o_ref[...] = x + vec_ref[3:4, :] * y


def _gmlp(x, vec, w_in, v_norm, w_s, bias, w_out):
    return pl.pallas_call(
        _gmlp_kernel,
        grid=(SEQ // GM_TM,),
        in_specs=[
            pl.BlockSpec((GM_TM, D_MODEL), lambda i: (i, 0)),
            _resident((8, D_MODEL)),
            _resident((D_MODEL, 2 * GM_D)),
            _resident((1, GM_D)),
            _resident((GM_GROUPS, GM_CHUNK, GM_CHUNK)),
            _resident((GM_CHUNK, GM_D)),
            _resident((GM_D, D_MODEL)),
        ],
        out_specs=pl.BlockSpec((GM_TM, D_MODEL), lambda i: (i, 0)),
        out_shape=jax.ShapeDtypeStruct((SEQ, D_MODEL), F32),
        scratch_shapes=[pltpu.VMEM((GM_TM, GM_D), F32), pltpu.VMEM((GM_TM, GM_D), BF16)],
        compiler_params=_params("parallel"),
        name="gmlp",
    )(x, vec, w_in, v_norm, w_s, bias, w_out)


def _qkv_kernel(x_ref, vec_ref, w_ref, q_ref, k_ref, v_ref, selt_ref, prevt_ref, cnt_ref,
                kmean_ref):
    i = pl.program_id(0)

    @pl.when(i == 0)
    def _init():
        kmean_ref[...] = jnp.zeros_like(kmean_ref)
        cnt_ref[...] = jnp.zeros_like(cnt_ref)

    h = _adaln(x_ref[...], vec_ref).astype(BF16)
    q = jnp.dot(h, w_ref[:, :D_MODEL], preferred_element_type=F32) * MB_SCALE
    q_ref[...] = q
    k = jnp.dot(h, w_ref[:, D_MODEL:2 * D_MODEL], preferred_element_type=F32)
    k_ref[...] = k.astype(BF16)
    nb = QKV_TM // MB_BLOCK
    for b in range(nb):
        kmean_ref[pl.ds(nb * i + b, 1), :] = jnp.mean(
            k[b * MB_BLOCK:(b + 1) * MB_BLOCK], axis=0, keepdims=True)
    v = jnp.dot(h, w_ref[:, 2 * D_MODEL:], preferred_element_type=F32)
    v_ref[...] = v.astype(BF16)

    lane = lax.broadcasted_iota(jnp.int32, (1, LANES), 1)
    blk = lax.broadcasted_iota(jnp.int32, (N_KBLOCKS, MB_BLOCK), 0)
    contract_last = (((1,), (1,)), ((), ()))
    for b in range(nb):
        j = nb * i + b
        cols = slice(b * MB_BLOCK, (b + 1) * MB_BLOCK)
        for pair in range(MB_HEADS // 2):
            km = kmean_ref[:, pair * LANES:(pair + 1) * LANES]
            qp = q[cols, pair * LANES:(pair + 1) * LANES]
            for hh in range(2):
                head = 2 * pair + hh
                km_h = jnp.where((lane // MB_HEAD_DIM) == hh, km, 0.0)
                g = lax.dot_general(km_h, qp, contract_last, preferred_element_type=F32)
                g = jnp.where(blk < j, g, NEG_INF)
                sel = jnp.zeros((N_KBLOCKS, MB_BLOCK), F32)
                for r in range(MB_TOPK):
                    m = jnp.max(g, axis=0, keepdims=True)
                    first = jnp.min(jnp.where(g == m, blk, N_KBLOCKS), axis=0, keepdims=True)
                    hit = blk == first
                    sel = jnp.where(jnp.logical_and(hit, r < j), 1.0, sel)
                    g = jnp.where(hit, BELOW_NEG_INF, g)
                prevt_ref[head:head + 1, cols] = jnp.max(
                    jnp.where(blk == j - 1, sel, 0.0), axis=0, keepdims=True)
                routed = jnp.where(blk < j - 1, sel, 0.0)
                selt_ref[head * N_KBLOCKS:(head + 1) * N_KBLOCKS, cols] = routed.astype(BF16)
                rows = slice(head * N_KBLOCKS, (head + 1) * N_KBLOCKS)
                cnt_ref[rows, :] = cnt_ref[rows, :] + jnp.sum(routed, axis=1, keepdims=True)


def _qkv(x, vec, w_qkv):
    row = pl.BlockSpec((QKV_TM, D_MODEL), lambda i: (i, 0))
    n_items = MB_HEADS * N_KBLOCKS
    return pl.pallas_call(
        _qkv_kernel,
        grid=(SEQ // QKV_TM,),
        in_specs=[row, _resident((8, D_MODEL)), _resident((D_MODEL, 3 * D_MODEL))],
        out_specs=[row, row, row,
                   pl.BlockSpec((n_items, QKV_TM), lambda i: (0, i)),
                   pl.BlockSpec((MB_HEADS, QKV_TM), lambda i: (0, i)),
                   pl.BlockSpec((n_items, LANES), lambda i: (0, 0))],
        out_shape=[jax.ShapeDtypeStruct((SEQ, D_MODEL), F32),
                   jax.ShapeDtypeStruct((SEQ, D_MODEL), BF16),
                   jax.ShapeDtypeStruct((SEQ, D_MODEL), BF16),
                   jax.ShapeDtypeStruct((n_items, SEQ), BF16),
                   jax.ShapeDtypeStruct((MB_HEADS, SEQ), F32),
                   jax.ShapeDtypeStruct((n_items, LANES), F32)],
        scratch_shapes=[pltpu.VMEM((N_KBLOCKS, D_MODEL), F32)],
        compiler_params=_params("arbitrary"),
        name="moba_qkv_gate",
    )(x, vec, w_qkv)


def _route_kernel(selt_ref, base_ref, pos_ref, carry_ref):
    i = pl.program_id(0)

    @pl.when(i == 0)
    def _init():
        carry_ref[...] = jnp.zeros_like(carry_ref)

    sel = selt_ref[...]
    before = (lax.broadcasted_iota(jnp.int32, (ROUTE_T, ROUTE_T), 0)
              < lax.broadcasted_iota(jnp.int32, (ROUTE_T, ROUTE_T), 1))
    rank = jnp.dot(sel, jnp.where(before, 1.0, 0.0).astype(BF16), preferred_element_type=F32)
    offset = base_ref[...] + carry_ref[...]
    rank = rank + jnp.concatenate([offset] * (ROUTE_T // LANES), axis=1)
    carry_ref[...] = carry_ref[...] + jnp.sum(sel.astype(F32), axis=1, keepdims=True)
    val = jnp.where(sel > 0, rank, ROUTE_NONE)
    for head in range(MB_HEADS):
        vh = val[head * N_KBLOCKS:(head + 1) * N_KBLOCKS]
        for r in range(MB_TOPK):
            m = jnp.min(vh, axis=0, keepdims=True)
            slot = head * MB_TOPK + r
            pos_ref[slot:slot + 1, :] = jnp.where(m < ROUTE_NONE, m, float(NULL_ROW)).astype(jnp.int32)
            vh = jnp.where(vh == m, ROUTE_NONE, vh)


def _route(selt, base):
    n_items = MB_HEADS * N_KBLOCKS
    return pl.pallas_call(
        _route_kernel,
        grid=(SEQ // ROUTE_T,),
        in_specs=[pl.BlockSpec((n_items, ROUTE_T), lambda i: (0, i)),
                  pl.BlockSpec((n_items, LANES), lambda i: (0, 0))],
        out_specs=pl.BlockSpec((MB_HEADS * MB_TOPK, ROUTE_T), lambda i: (0, i)),
        out_shape=jax.ShapeDtypeStruct((MB_HEADS * MB_TOPK, SEQ), jnp.int32),
        scratch_shapes=[pltpu.VMEM((n_items, LANES), F32)],
        compiler_params=_params("arbitrary"),
        name="moba_route",
    )(selt, base)


def _moba_kernel(choff_ref, chblk_ref, chvalid_ref, nch_ref, c31_ref,
                 qf_ref, pos_hbm, k_ref, v_ref, prev_ref, bown_ref, bprev_ref,
                 o_ref, buf_ref, stage_ref, hold_ref, s_ref, p_ref, m_ref,
                 pos_even_ref, pos_odd_ref, pos_sem):
    head = pl.program_id(0)
    t = pl.program_id(1)
    hh = head % 2
    lane = lax.broadcasted_iota(jnp.int32, (1, LANES), 1)
    in_head = (lane // MB_HEAD_DIM) == hh
    stat0 = (1 - hh) * MB_HEAD_DIM
    is_m_lane = lane == stat0
    is_l_lane = lane == stat0 + 1
    contract_last = (((1,), (1,)), ((), ()))
    ones_bf = jnp.ones((), BF16)

    def queries(rows):
        return jnp.where(in_head, rows, 0.0).astype(BF16)

    def values(rows):
        return jnp.where(in_head, rows, ones_bf)

    @pl.when(jnp.logical_and(t == 0, head == 0))
    def _init():
        def zero(c, _):
            buf_ref[pl.ds(pl.multiple_of(c * MB_BLOCK, MB_BLOCK), MB_BLOCK), :] = (
                jnp.zeros((MB_BLOCK, LANES), F32))
            return 0
        lax.fori_loop(0, BUF_ROWS // MB_BLOCK, zero, 0)

        def zero_hold(c, _):
            hold_ref[pl.ds(pl.multiple_of(c * MB_BLOCK, MB_BLOCK), MB_BLOCK), :] = (
                jnp.zeros((MB_BLOCK, LANES), BF16))
            return 0
        lax.fori_loop(0, SEQ // MB_BLOCK, zero_hold, 0)

    def dispatch(pos_ref):
        for a in range(STEP_ROWS):
            row = qf_ref[a:a + 1, :]
            for r in range(MB_TOPK):
                buf_ref[pl.ds(pos_ref[r * STEP_ROWS + a], 1), :] = row

    @pl.when(t == NSTEPS)
    def _routed():
        buf_ref[NULL_ROW:NULL_ROW + ROUTE_ALIGN, :] = jnp.where(
            is_m_lane, NEG_INF, jnp.zeros((ROUTE_ALIGN, LANES), F32))
        n = nch_ref[head]
        far_bias = c31_ref[head]
        row_id = lax.broadcasted_iota(jnp.int32, (MB_BLOCK, LANES), 0)

        def meta(i):
            idx = head * MAX_CHUNKS + jnp.clip(i, 0, n - 1)
            off = pl.multiple_of(choff_ref[idx], ROUTE_ALIGN)
            blk = pl.multiple_of(chblk_ref[idx] * MB_BLOCK, MB_BLOCK)
            return off, blk, jnp.where(jnp.logical_and(i >= 0, i < n), chvalid_ref[idx], 0)

        def chunk_queries(off):
            return queries(buf_ref[pl.ds(off, MB_BLOCK), :])

        def scores(qh, blk):
            return lax.dot_general(qh, k_ref[pl.ds(blk, MB_BLOCK), :], contract_last,
                                   preferred_element_type=F32)

        def probs(u):
            s = s_ref[u]
            m = jnp.max(s, axis=1, keepdims=True)
            p_ref[u] = jnp.exp(s - m).astype(BF16)
            m_ref[u] = jnp.broadcast_to(m, (MB_BLOCK, LANES))

        def finish(u, off, blk, valid):
            acc = jnp.dot(p_ref[u], values(v_ref[pl.ds(blk, MB_BLOCK), :]), preferred_element_type=F32)
            new = jnp.where(is_m_lane, m_ref[u] + far_bias, acc)
            pltpu.store(buf_ref.at[pl.ds(off, MB_BLOCK), :], new, mask=row_id < valid)

        @pl.when(n > 0)
        def _chunks():
            slots = range(CHUNKS_PER_BODY)
            p_ref[...] = jnp.zeros_like(p_ref)
            m_ref[...] = jnp.zeros_like(m_ref)
            for u in slots:
                off, blk, _ = meta(u)
                s_ref[u] = scores(chunk_queries(off), blk)

            def chunk_group(c, _):
                i = CHUNKS_PER_BODY * c
                nxt = [meta(i + CHUNKS_PER_BODY + u) for u in slots]
                done = [meta(i - CHUNKS_PER_BODY + u) for u in slots]
                rows = [chunk_queries(off) for off, _, _ in nxt]
                for u in slots:
                    finish(u, *done[u])
                for u in slots:
                    probs(u)
                for u in slots:
                    s_ref[u] = scores(rows[u], nxt[u][1])
                return 0
            lax.fori_loop(0, (n + CHUNKS_PER_BODY - 1) // CHUNKS_PER_BODY + 1, chunk_group, 0)

    def combine(pos_ref):
        step = t - (NSTEPS + 1)

        n_sub = STEP_ROWS // MB_BLOCK
        for jj in range(n_sub):
            j = step * n_sub + jj
            rs = slice(jj * MB_BLOCK, (jj + 1) * MB_BLOCK)
            win = pl.ds(pl.multiple_of(jnp.maximum(j - 1, 0) * MB_BLOCK, MB_BLOCK), 2 * MB_BLOCK)
            s = lax.dot_general(queries(qf_ref[rs, :]), k_ref[win, :], contract_last,
                                preferred_element_type=F32)
            picked = jnp.sum(jnp.where(lane == head, prev_ref[rs, :], 0.0),
                             axis=1, keepdims=True) > 0.0
            bias_lo, bias_hi = bprev_ref[...], bown_ref[...]
            if jj == 0:
                first = step == 0
                bias_lo = jnp.where(first, bown_ref[...], bias_lo)
                bias_hi = jnp.where(first, NEG_INF, bias_hi)
                picked = jnp.logical_or(picked, first)
            s_lo = jnp.where(picked, s[:, :MB_BLOCK] + bias_lo, NEG_INF)
            s_hi = s[:, MB_BLOCK:] + bias_hi
            m_d = jnp.maximum(jnp.max(s_lo, axis=1, keepdims=True), jnp.max(s_hi, axis=1, keepdims=True))
            p = jnp.concatenate([jnp.exp(s_lo - m_d), jnp.exp(s_hi - m_d)], axis=1).astype(BF16)
            acc = jnp.dot(p, values(v_ref[win, :]), preferred_element_type=F32)

            for a in range(jj * MB_BLOCK, (jj + 1) * MB_BLOCK):
                for r in range(MB_TOPK):
                    stage_ref[r, a:a + 1, :] = buf_ref[pl.ds(pos_ref[r * STEP_ROWS + a], 1), :]

            parts = [stage_ref[r, rs, :] for r in range(MB_TOPK)]
            ms = [jnp.sum(jnp.where(is_m_lane, part, 0.0), axis=1, keepdims=True) for part in parts]
            m_all = m_d
            for m_r in ms:
                m_all = jnp.maximum(m_all, m_r)
            total = jnp.exp(m_d - m_all) * acc
            for m_r, part in zip(ms, parts):
                total = total + jnp.exp(m_r - m_all) * part
            den = jnp.sum(jnp.where(is_l_lane, total, 0.0), axis=1, keepdims=True)
            out = (total / den).astype(BF16)
            hold_rows = pl.ds(pl.multiple_of(j * MB_BLOCK, MB_BLOCK), MB_BLOCK)
            take_parked = jnp.logical_and(lane < MB_HEAD_DIM, hh == 1)
            merged = jnp.where(take_parked, hold_ref[hold_rows, :], out)
            hold_ref[hold_rows, :] = merged
            o_ref[rs, :] = merged

    steps_per_head = 2 * NSTEPS
    use = head * steps_per_head + jnp.where(t < NSTEPS, t, t - 1)
    pos_bufs = (pos_even_ref, pos_odd_ref)

    def pos_copy(u, parity):
        block = (u // steps_per_head) * NSTEPS + u % NSTEPS
        src = pos_hbm.at[pl.ds(pl.multiple_of(block * POS_BLOCK, POS_BLOCK), POS_BLOCK)]
        return pltpu.make_async_copy(src, pos_bufs[parity], pos_sem.at[parity])

    for parity in range(2):
        @pl.when(jnp.logical_and(t != NSTEPS, use % 2 == parity))
        def _positions_step(parity=parity):
            @pl.when(use == 0)
            def _first():
                pos_copy(use, parity).start()

            @pl.when(use + 1 < MB_HEADS * steps_per_head)
            def _prefetch():
                pos_copy(use + 1, 1 - parity).start()

            pos_copy(use, parity).wait()

            @pl.when(t < NSTEPS)
            def _dispatch():
                dispatch(pos_bufs[parity])

            @pl.when(t > NSTEPS)
            def _combine():
                combine(pos_bufs[parity])


def _moba_attention(chunk_meta, c31, qf, pos, k, v, prevc, bias_own, bias_prev):
    def seq_step(t):
        return jnp.where(t < NSTEPS, t, jnp.maximum(t - (NSTEPS + 1), 0) + jnp.where(t == NSTEPS, NSTEPS - 1, 0))

    def combine_step(t):
        return jnp.maximum(t - (NSTEPS + 1), 0)

    pair_resident = pl.BlockSpec((SEQ, LANES), lambda h, t, *_: (0, h // 2), pipeline_mode=pl.Buffered(1))
    bias = pl.BlockSpec((None, MB_BLOCK, MB_BLOCK), lambda h, t, *_: (h, 0, 0))
    return pl.pallas_call(
        _moba_kernel,
        grid_spec=pltpu.PrefetchScalarGridSpec(
            num_scalar_prefetch=5,
            grid=(MB_HEADS, 2 * NSTEPS + 1),
            in_specs=[
                pl.BlockSpec((STEP_ROWS, LANES), lambda h, t, *_: (seq_step(t), h // 2)),
                pl.BlockSpec(memory_space=pl.ANY),
                pair_resident, pair_resident,
                pl.BlockSpec((STEP_ROWS, LANES), lambda h, t, *_: (combine_step(t), 0)),
                bias, bias,
            ],
            out_specs=pl.BlockSpec(
                (STEP_ROWS, LANES),
                lambda h, t, *_: (jnp.where(h % 2 == 1, combine_step(t), 0), h // 2)),
            scratch_shapes=[pltpu.VMEM((BUF_ROWS, LANES), F32),
                            pltpu.VMEM((MB_TOPK, STEP_ROWS, LANES), F32),
                            pltpu.VMEM((SEQ, LANES), BF16),
                            pltpu.VMEM((CHUNKS_PER_BODY, MB_BLOCK, MB_BLOCK), F32),
                            pltpu.VMEM((CHUNKS_PER_BODY, MB_BLOCK, MB_BLOCK), BF16),
                            pltpu.VMEM((CHUNKS_PER_BODY, MB_BLOCK, LANES), F32),
                            pltpu.SMEM((POS_BLOCK,), jnp.int32), pltpu.SMEM((POS_BLOCK,), jnp.int32),
                            pltpu.SemaphoreType.DMA((2,))],
        ),
        out_shape=jax.ShapeDtypeStruct((SEQ, D_MODEL), BF16),
        compiler_params=_params("arbitrary", "arbitrary"),
        name="moba_attention",
    )(*chunk_meta, c31, qf, pos, k, v, prevc, bias_own, bias_prev)


def _t5_bucket_np(rel):
    n = np.maximum(rel, 0)
    nf = np.maximum(n, T5_MAX_EXACT).astype(np.float32)
    large = T5_MAX_EXACT + (np.log(nf / np.float32(T5_MAX_EXACT))
                            / np.float32(math.log(T5_MAX_DISTANCE / T5_MAX_EXACT))
                            * np.float32(T5_NUM_BUCKETS - T5_MAX_EXACT)).astype(np.int32)
    large = np.minimum(large, T5_NUM_BUCKETS - 1)
    return np.where(n < T5_MAX_EXACT, n, large)


def _moba_bias_tables(rel_bias):
    a = np.arange(MB_BLOCK)
    rel_own = a[:, None] - a[None, :]
    table = rel_bias.astype(F32).T

    def expand(rel):
        bucket = jnp.asarray(_t5_bucket_np(rel).reshape(-1).astype(np.int32))
        onehot = (bucket[None, :] == jnp.arange(T5_NUM_BUCKETS)[:, None]).astype(F32)
        return jnp.dot(table, onehot, precision=lax.Precision.HIGHEST).reshape(
            MB_HEADS, MB_BLOCK, MB_BLOCK)

    b_own = jnp.where(jnp.asarray(rel_own >= 0)[None], expand(rel_own), NEG_INF)
    return b_own, expand(rel_own + MB_BLOCK), table[:, T5_NUM_BUCKETS - 1]


def _chunk_tables(cnt, start):
    nch = (cnt + MB_BLOCK - 1) // MB_BLOCK
    cum = jnp.cumsum(nch, axis=1)
    i = jnp.arange(MAX_CHUNKS)
    blk = jnp.sum(cum[:, None, :] <= i[None, :, None], axis=2)
    blk = jnp.minimum(blk, N_KBLOCKS - 1)
    onehot = blk[:, :, None] == jnp.arange(N_KBLOCKS)[None, None, :]

    def pick(x):
        return jnp.sum(jnp.where(onehot, x[:, None, :], 0), axis=2)

    def flat(x):
        return x.reshape(-1).astype(jnp.int32)
    within = i[None, :] - pick(cum - nch)
    off = pick(start) + within * MB_BLOCK
    valid = jnp.clip(pick(cnt) - within * MB_BLOCK, 0, MB_BLOCK)
    return flat(off), flat(blk), flat(valid), flat(cum[:, -1])


def _moba(xs, vec, w_qkv, rel_bias):
    qf, k, v, selt, prevt, cnt = _qkv(xs, vec, w_qkv)
    cnt = cnt[:, 0].astype(jnp.int32).reshape(MB_HEADS, N_KBLOCKS)
    padded = (cnt + ROUTE_ALIGN - 1) // ROUTE_ALIGN * ROUTE_ALIGN
    start = jnp.cumsum(padded, axis=1) - padded
    base = jnp.broadcast_to(start.reshape(-1, 1).astype(F32), (MB_HEADS * N_KBLOCKS, LANES))
    pos = _route(selt, base).reshape(MB_HEADS, MB_TOPK, NSTEPS, STEP_ROWS)
    pos = pos.transpose(0, 2, 1, 3).reshape(-1)
    b_own, b_prev, c31 = _moba_bias_tables(rel_bias)
    prevc = jnp.pad(prevt.T, ((0, 0), (0, LANES - MB_HEADS)))
    a = _moba_attention(_chunk_tables(cnt, start), c31, qf, pos, k, v, prevc, b_own, b_prev)
    return a


def _vec(norm_g, mod3):
    return jnp.concatenate([norm_g[None, :], mod3, jnp.zeros((4, D_MODEL), F32)], axis=0)


def kernel(x, c, rel_bias, mod_w, mod_b, norm_g, ffn_w_in, ffn_w_out, gmlp_w_in, gmlp_v_norm,
           gmlp_w_s, gmlp_b_s, gmlp_w_out, moba_w_qkv, moba_w_o, final_norm):
    assert x.shape == (1, SEQ, D_MODEL)
    xs = x.reshape(SEQ, D_MODEL)
    mod = _modulation(c.reshape(D_MODEL, 1), mod_w, mod_b).reshape(DEPTH, N_SUBLAYERS, 3, D_MODEL)
    fin = final_norm.reshape(1, D_MODEL)
    w_in_all, w_out_all = ffn_w_in.astype(BF16), ffn_w_out.astype(BF16)

    for i in range(DEPTH):
        li = i // 2
        xs = _ffn(xs, _vec(norm_g[i, 0], mod[i, 0]), w_in_all, w_out_all, i, 0, fin, False)
        vec = _vec(norm_g[i, 1], mod[i, 1])
        if i % 2 == 0:
            bias = jnp.repeat(gmlp_b_s[li].T, GM_GROUP_DIM, axis=1)
            xs = _gmlp(xs, vec, gmlp_w_in[li].astype(BF16), gmlp_v_norm[li].reshape(1, GM_D),
                       gmlp_w_s[li].astype(BF16), bias, gmlp_w_out[li].astype(BF16))
            mixer = None
        else:
            mixer = (_moba(xs, vec, moba_w_qkv[li].astype(BF16), rel_bias), vec,
                     moba_w_o[li].astype(BF16))
        xs = _ffn(xs, _vec(norm_g[i, 2], mod[i, 2]), w_in_all, w_out_all, i, 1, fin, i == DEPTH - 1,
                  mixer)
    return xs.reshape(1, SEQ, D_MODEL)
```

```python
import functools
import math

import numpy as np
import jax
import jax.numpy as jnp
from jax import lax
from jax.experimental import pallas as pl
from jax.experimental.pallas import tpu as pltpu

F32 = jnp.float32
BF16 = jnp.bfloat16

D_MODEL = 1024
SEQ = 16384
DEPTH = 2
N_SUBLAYERS = 3
RMS_EPS = 1e-6
D_FF = 2816

GM_D = 3 * D_MODEL
GM_GROUPS = 16
GM_GROUP_DIM = GM_D // GM_GROUPS
GM_CHUNK = 128
GM_PAIR = 2 * GM_GROUP_DIM

MB_HEADS = 16
MB_HEAD_DIM = 64
MB_BLOCK = 256
MB_TOPK = 3
MB_SCALE = MB_HEAD_DIM ** -0.5
N_KBLOCKS = SEQ // MB_BLOCK

T5_NUM_BUCKETS = 32
T5_MAX_EXACT = 16
T5_MAX_DISTANCE = 128
NEG_INF = -1e30
BELOW_NEG_INF = -3e38

LANES = 128
VMEM_LIMIT = 56 * 1024 * 1024

MOD_TN = 1536
FFN_TM = 1024
FFN_MIXED_TM = 512
FFN_CHUNK = 256
GM_TM = 512
GM_VCHUNK = 768
QKV_TM = 512

ROUTE_T = 512
ROUTE_ALIGN = 8
STEP_ROWS = 2048
NSTEPS = SEQ // STEP_ROWS
BUF_DATA_ROWS = MB_TOPK * SEQ + N_KBLOCKS * ROUTE_ALIGN
BUF_ROWS = BUF_DATA_ROWS + 3 * MB_BLOCK
NULL_ROW = BUF_ROWS - ROUTE_ALIGN
ROUTE_NONE = 1e9
MAX_CHUNKS = MB_TOPK * SEQ // MB_BLOCK + N_KBLOCKS
POS_BLOCK = MB_TOPK * STEP_ROWS
CHUNKS_PER_BODY = 8


def _resident(shape):
    nd = len(shape)
    return pl.BlockSpec(shape, lambda *_: (0,) * nd, pipeline_mode=pl.Buffered(1))


def _params(*sem):
    return pltpu.CompilerParams(dimension_semantics=sem, vmem_limit_bytes=VMEM_LIMIT)


def _rms(x):
    return x * lax.rsqrt(jnp.mean(x * x, axis=-1, keepdims=True) + RMS_EPS)


def _adaln(x, vec_ref):
    y = _rms(x) * vec_ref[0:1, :]
    return y * (1.0 + vec_ref[2:3, :]) + vec_ref[1:2, :]


def _gelu(x):
    return 0.5 * x * (1.0 + lax.erf(x * np.float32(math.sqrt(0.5))))


def _mod_kernel(c_ref, w_ref, b_ref, o_ref):
    c = c_ref[...]
    c_act = c * jax.nn.sigmoid(c)
    o_ref[...] = jnp.sum(c_act * w_ref[...], axis=0, keepdims=True) + b_ref[...]


def _modulation(c_col, mod_w, mod_b):
    n = mod_w.shape[-1]
    return pl.pallas_call(
        _mod_kernel,
        grid=(DEPTH, n // MOD_TN),
        in_specs=[
            pl.BlockSpec((D_MODEL, 1), lambda i, j: (0, 0)),
            pl.BlockSpec((None, D_MODEL, MOD_TN), lambda i, j: (i, 0, j)),
            pl.BlockSpec((None, 1, MOD_TN), lambda i, j: (i, 0, j)),
        ],
        out_specs=pl.BlockSpec((None, 1, MOD_TN), lambda i, j: (i, 0, j)),
        out_shape=jax.ShapeDtypeStruct((DEPTH, 1, n), F32),
        compiler_params=_params("arbitrary", "arbitrary"),
        name="modulation",
    )(c_col, mod_w, mod_b.reshape(DEPTH, 1, n))


def _ffn_kernel(*refs, final, mixed):
    if mixed:
        x_ref, a_ref, mixvec_ref, wo_ref, vec_ref, win_ref, wout_ref, fin_ref, o_ref, act_ref = refs
        x = x_ref[...] + mixvec_ref[3:4, :] * jnp.dot(a_ref[...], wo_ref[...],
                                                      preferred_element_type=F32)
    else:
        x_ref, vec_ref, win_ref, wout_ref, fin_ref, o_ref, act_ref = refs
        x = x_ref[...]
    h = _adaln(x, vec_ref).astype(BF16)
    for c in range(D_FF // FFN_CHUNK):
        lo = c * FFN_CHUNK
        g = jnp.dot(h, win_ref[:, lo:lo + FFN_CHUNK], preferred_element_type=F32)
        u = jnp.dot(h, win_ref[:, D_FF + lo:D_FF + lo + FFN_CHUNK], preferred_element_type=F32)
        act_ref[:, lo:lo + FFN_CHUNK] = (g * jax.nn.sigmoid(g) * u).astype(BF16)
    y = jnp.dot(act_ref[...], wout_ref[...], preferred_element_type=F32)
    out = x + (0.5 * vec_ref[3:4, :]) * y
    if final:
        out = _rms(out) * fin_ref[...]
    o_ref[...] = out


def _ffn(x, vec, w_in_all, w_out_all, layer, which, fin, final, mixer=None):
    def one_matrix(rows, cols):
        return pl.BlockSpec((None, None, rows, cols), lambda i: (layer, which, 0, 0),
                            pipeline_mode=pl.Buffered(1))
    mixed = mixer is not None
    tm = FFN_MIXED_TM if mixed else FFN_TM
    row = pl.BlockSpec((tm, D_MODEL), lambda i: (i, 0))
    mix_specs = [row, _resident((8, D_MODEL)), _resident((D_MODEL, D_MODEL))] if mixed else []
    return pl.pallas_call(
        functools.partial(_ffn_kernel, final=final, mixed=mixed),
        grid=(SEQ // tm,),
        in_specs=[row, *mix_specs,
                  _resident((8, D_MODEL)),
                  one_matrix(D_MODEL, 2 * D_FF),
                  one_matrix(D_FF, D_MODEL),
                  _resident((1, D_MODEL))],
        out_specs=row,
        out_shape=jax.ShapeDtypeStruct((SEQ, D_MODEL), F32),
        scratch_shapes=[pltpu.VMEM((tm, D_FF), BF16)],
        compiler_params=_params("parallel"),
        name="ffn_final" if final else "ffn",
    )(x, *(mixer or ()), vec, w_in_all, w_out_all, fin)


def _gmlp_kernel(x_ref, vec_ref, win_ref, vnorm_ref, ws_ref, bias_ref, wout_ref,
                 o_ref, v_ref, g_ref):
    x = x_ref[...]
    h = _adaln(x, vec_ref).astype(BF16)

    ssq = jnp.zeros((GM_TM, 1), F32)
    for c in range(GM_D // GM_VCHUNK):
        lo = c * GM_VCHUNK
        v = _gelu(jnp.dot(h, win_ref[:, GM_D + lo:GM_D + lo + GM_VCHUNK],
                          preferred_element_type=F32))
        v_ref[:, lo:lo + GM_VCHUNK] = v
        ssq = ssq + jnp.sum(v * v, axis=-1, keepdims=True)
    r = lax.rsqrt(ssq * (1.0 / GM_D) + RMS_EPS)

    row = lax.broadcasted_iota(jnp.int32, (GM_CHUNK, GM_CHUNK), 0)
    col = lax.broadcasted_iota(jnp.int32, (GM_CHUNK, GM_CHUNK), 1)
    causal = row >= col
    low_half = lax.broadcasted_iota(jnp.int32, (GM_CHUNK, LANES), 1) < (GM_GROUP_DIM - LANES)

    for p in range(GM_GROUPS // 2):
        lo = p * GM_PAIR
        vn = (v_ref[:, lo:lo + GM_PAIR] * r * vnorm_ref[:, lo:lo + GM_PAIR]).astype(BF16)
        u = _gelu(jnp.dot(h, win_ref[:, lo:lo + GM_PAIR], preferred_element_type=F32))
        w0 = jnp.where(causal, ws_ref[2 * p], jnp.zeros((), BF16))
        w1 = jnp.where(causal, ws_ref[2 * p + 1], jnp.zeros((), BF16))
        for c in range(GM_TM // GM_CHUNK):
            rows = slice(c * GM_CHUNK, (c + 1) * GM_CHUNK)
            vc = vn[rows]
            a = jnp.dot(w0, vc[:, :2 * LANES], preferred_element_type=F32)
            b = jnp.dot(w1, vc[:, LANES:], preferred_element_type=F32)
            mid = jnp.where(low_half, a[:, LANES:], b[:, :LANES])
            sv = jnp.concatenate([a[:, :LANES], mid, b[:, LANES:]], axis=1)
            sv = sv + bias_ref[:, lo:lo + GM_PAIR]
            g_ref[rows, lo:lo + GM_PAIR] = (u[rows] * sv).astype(BF16)

    y = jnp.dot(g_ref[...], wout_ref[...], preferred_element_type=F32)
    o_ref[...] = x + vec_ref[3:4, :] * y


def _gmlp(x, vec, w_in, v_norm, w_s, bias, w_out):
    return pl.pallas_call(
        _gmlp_kernel,
        grid=(SEQ // GM_TM,),
        in_specs=[
            pl.BlockSpec((GM_TM, D_MODEL), lambda i: (i, 0)),
            _resident((8, D_MODEL)),
            _resident((D_MODEL, 2 * GM_D)),
            _resident((1, GM_D)),
            _resident((GM_GROUPS, GM_CHUNK, GM_CHUNK)),
            _resident((GM_CHUNK, GM_D)),
            _resident((GM_D, D_MODEL)),
        ],
        out_specs=pl.BlockSpec((GM_TM, D_MODEL), lambda i: (i, 0)),
        out_shape=jax.ShapeDtypeStruct((SEQ, D_MODEL), F32),
        scratch_shapes=[pltpu.VMEM((GM_TM, GM_D), F32), pltpu.VMEM((GM_TM, GM_D), BF16)],
        compiler_params=_params("parallel"),
        name="gmlp",
    )(x, vec, w_in, v_norm, w_s, bias, w_out)


def _qkv_kernel(x_ref, vec_ref, w_ref, q_ref, k_ref, v_ref, selt_ref, prevt_ref, cnt_ref,
                kmean_ref):
    i = pl.program_id(0)

    @pl.when(i == 0)
    def _init():
        kmean_ref[...] = jnp.zeros_like(kmean_ref)
        cnt_ref[...] = jnp.zeros_like(cnt_ref)

    h = _adaln(x_ref[...], vec_ref).astype(BF16)
    q = jnp.dot(h, w_ref[:, :D_MODEL], preferred_element_type=F32) * MB_SCALE
    q_ref[...] = q
    k = jnp.dot(h, w_ref[:, D_MODEL:2 * D_MODEL], preferred_element_type=F32)
    k_ref[...] = k.astype(BF16)
    nb = QKV_TM // MB_BLOCK
    for b in range(nb):
        kmean_ref[pl.ds(nb * i + b, 1), :] = jnp.mean(
            k[b * MB_BLOCK:(b + 1) * MB_BLOCK], axis=0, keepdims=True)
    v = jnp.dot(h, w_ref[:, 2 * D_MODEL:], preferred_element_type=F32)
    v_ref[...] = v.astype(BF16)

    lane = lax.broadcasted_iota(jnp.int32, (1, LANES), 1)
    blk = lax.broadcasted_iota(jnp.int32, (N_KBLOCKS, MB_BLOCK), 0)
    contract_last = (((1,), (1,)), ((), ()))
    for b in range(nb):
        j = nb * i + b
        cols = slice(b * MB_BLOCK, (b + 1) * MB_BLOCK)
        for pair in range(MB_HEADS // 2):
            km = kmean_ref[:, pair * LANES:(pair + 1) * LANES]
            qp = q[cols, pair * LANES:(pair + 1) * LANES]
            for hh in range(2):
                head = 2 * pair + hh
                km_h = jnp.where((lane // MB_HEAD_DIM) == hh, km, 0.0)
                g = lax.dot_general(km_h, qp, contract_last, preferred_element_type=F32)
                g = jnp.where(blk < j, g, NEG_INF)
                sel = jnp.zeros((N_KBLOCKS, MB_BLOCK), F32)
                for r in range(MB_TOPK):
                    m = jnp.max(g, axis=0, keepdims=True)
                    first = jnp.min(jnp.where(g == m, blk, N_KBLOCKS), axis=0, keepdims=True)
                    hit = blk == first
                    sel = jnp.where(jnp.logical_and(hit, r < j), 1.0, sel)
                    g = jnp.where(hit, BELOW_NEG_INF, g)
                prevt_ref[head:head + 1, cols] = jnp.max(
                    jnp.where(blk == j - 1, sel, 0.0), axis=0, keepdims=True)
                routed = jnp.where(blk < j - 1, sel, 0.0)
                selt_ref[head * N_KBLOCKS:(head + 1) * N_KBLOCKS, cols] = routed.astype(BF16)
                rows = slice(head * N_KBLOCKS, (head + 1) * N_KBLOCKS)
                cnt_ref[rows, :] = cnt_ref[rows, :] + jnp.sum(routed, axis=1, keepdims=True)


def _qkv(x, vec, w_qkv):
    row = pl.BlockSpec((QKV_TM, D_MODEL), lambda i: (i, 0))
    n_items = MB_HEADS * N_KBLOCKS
    return pl.pallas_call(
        _qkv_kernel,
        grid=(SEQ // QKV_TM,),
        in_specs=[row, _resident((8, D_MODEL)), _resident((D_MODEL, 3 * D_MODEL))],
        out_specs=[row, row, row,
                   pl.BlockSpec((n_items, QKV_TM), lambda i: (0, i)),
                   pl.BlockSpec((MB_HEADS, QKV_TM), lambda i: (0, i)),
                   pl.BlockSpec((n_items, LANES), lambda i: (0, 0))],
        out_shape=[jax.ShapeDtypeStruct((SEQ, D_MODEL), F32),
                   jax.ShapeDtypeStruct((SEQ, D_MODEL), BF16),
                   jax.ShapeDtypeStruct((SEQ, D_MODEL), BF16),
                   jax.ShapeDtypeStruct((n_items, SEQ), BF16),
                   jax.ShapeDtypeStruct((MB_HEADS, SEQ), F32),
                   jax.ShapeDtypeStruct((n_items, LANES), F32)],
        scratch_shapes=[pltpu.VMEM((N_KBLOCKS, D_MODEL), F32)],
        compiler_params=_params("arbitrary"),
        name="moba_qkv_gate",
    )(x, vec, w_qkv)


def _route_kernel(selt_ref, base_ref, pos_ref, carry_ref):
    i = pl.program_id(0)

    @pl.when(i == 0)
    def _init():
        carry_ref[...] = jnp.zeros_like(carry_ref)

    sel = selt_ref[...]
    before = (lax.broadcasted_iota(jnp.int32, (ROUTE_T, ROUTE_T), 0)
              < lax.broadcasted_iota(jnp.int32, (ROUTE_T, ROUTE_T), 1))
    rank = jnp.dot(sel, jnp.where(before, 1.0, 0.0).astype(BF16), preferred_element_type=F32)
    offset = base_ref[...] + carry_ref[...]
    rank = rank + jnp.concatenate([offset] * (ROUTE_T // LANES), axis=1)
    carry_ref[...] = carry_ref[...] + jnp.sum(sel.astype(F32), axis=1, keepdims=True)
    val = jnp.where(sel > 0, rank, ROUTE_NONE)
    for head in range(MB_HEADS):
        vh = val[head * N_KBLOCKS:(head + 1) * N_KBLOCKS]
        for r in range(MB_TOPK):
            m = jnp.min(vh, axis=0, keepdims=True)
            slot = head * MB_TOPK + r
            pos_ref[slot:slot + 1, :] = jnp.where(m < ROUTE_NONE, m, float(NULL_ROW)).astype(jnp.int32)
            vh = jnp.where(vh == m, ROUTE_NONE, vh)


def _route(selt, base):
    n_items = MB_HEADS * N_KBLOCKS
    return pl.pallas_call(
        _route_kernel,
        grid=(SEQ // ROUTE_T,),
        in_specs=[pl.BlockSpec((n_items, ROUTE_T), lambda i: (0, i)),
                  pl.BlockSpec((n_items, LANES), lambda i: (0, 0))],
        out_specs=pl.BlockSpec((MB_HEADS * MB_TOPK, ROUTE_T), lambda i: (0, i)),
        out_shape=jax.ShapeDtypeStruct((MB_HEADS * MB_TOPK, SEQ), jnp.int32),
        scratch_shapes=[pltpu.VMEM((n_items, LANES), F32)],
        compiler_params=_params("arbitrary"),
        name="moba_route",
    )(selt, base)


def _moba_kernel(choff_ref, chblk_ref, chvalid_ref, nch_ref, c31_ref,
                 qf_ref, pos_hbm, k_ref, v_ref, prev_ref, bown_ref, bprev_ref,
                 o_ref, buf_ref, stage_ref, hold_ref, s_ref, p_ref, m_ref,
                 pos_even_ref, pos_odd_ref, pos_sem):
    head = pl.program_id(0)
    t = pl.program_id(1)
    hh = head % 2
    lane = lax.broadcasted_iota(jnp.int32, (1, LANES), 1)
    in_head = (lane // MB_HEAD_DIM) == hh
    stat0 = (1 - hh) * MB_HEAD_DIM
    is_m_lane = lane == stat0
    is_l_lane = lane == stat0 + 1
    contract_last = (((1,), (1,)), ((), ()))
    ones_bf = jnp.ones((), BF16)

    def queries(rows):
        return jnp.where(in_head, rows, 0.0).astype(BF16)

    def values(rows):
        return jnp.where(in_head, rows, ones_bf)

    @pl.when(jnp.logical_and(t == 0, head == 0))
    def _init():
        def zero(c, _):
            buf_ref[pl.ds(pl.multiple_of(c * MB_BLOCK, MB_BLOCK), MB_BLOCK), :] = (
                jnp.zeros((MB_BLOCK, LANES), F32))
            return 0
        lax.fori_loop(0, BUF_ROWS // MB_BLOCK, zero, 0)

        def zero_hold(c, _):
            hold_ref[pl.ds(pl.multiple_of(c * MB_BLOCK, MB_BLOCK), MB_BLOCK), :] = (
                jnp.zeros((MB_BLOCK, LANES), BF16))
            return 0
        lax.fori_loop(0, SEQ // MB_BLOCK, zero_hold, 0)

    def dispatch(pos_ref):
        for a in range(STEP_ROWS):
            row = qf_ref[a:a + 1, :]
            for r in range(MB_TOPK):
                buf_ref[pl.ds(pos_ref[r * STEP_ROWS + a], 1), :] = row

    @pl.when(t == NSTEPS)
    def _routed():
        buf_ref[NULL_ROW:NULL_ROW + ROUTE_ALIGN, :] = jnp.where(
            is_m_lane, NEG_INF, jnp.zeros((ROUTE_ALIGN, LANES), F32))
        n = nch_ref[head]
        far_bias = c31_ref[head]
        row_id = lax.broadcasted_iota(jnp.int32, (MB_BLOCK, LANES), 0)

        def meta(i):
            idx = head * MAX_CHUNKS + jnp.clip(i, 0, n - 1)
            off = pl.multiple_of(choff_ref[idx], ROUTE_ALIGN)
            blk = pl.multiple_of(chblk_ref[idx] * MB_BLOCK, MB_BLOCK)
            return off, blk, jnp.where(jnp.logical_and(i >= 0, i < n), chvalid_ref[idx], 0)

        def chunk_queries(off):
            return queries(buf_ref[pl.ds(off, MB_BLOCK), :])

        def scores(qh, blk):
            return lax.dot_general(qh, k_ref[pl.ds(blk, MB_BLOCK), :], contract_last,
                                   preferred_element_type=F32)

        def probs(u):
            s = s_ref[u]
            m = jnp.max(s, axis=1, keepdims=True)
            p_ref[u] = jnp.exp(s - m).astype(BF16)
            m_ref[u] = jnp.broadcast_to(m, (MB_BLOCK, LANES))

        def finish(u, off, blk, valid):
            acc = jnp.dot(p_ref[u], values(v_ref[pl.ds(blk, MB_BLOCK), :]), preferred_element_type=F32)
            new = jnp.where(is_m_lane, m_ref[u] + far_bias, acc)
            pltpu.store(buf_ref.at[pl.ds(off, MB_BLOCK), :], new, mask=row_id < valid)

        @pl.when(n > 0)
        def _chunks():
            slots = range(CHUNKS_PER_BODY)
            p_ref[...] = jnp.zeros_like(p_ref)
            m_ref[...] = jnp.zeros_like(m_ref)
            for u in slots:
                off, blk, _ = meta(u)
                s_ref[u] = scores(chunk_queries(off), blk)

            def chunk_group(c, _):
                i = CHUNKS_PER_BODY * c
                nxt = [meta(i + CHUNKS_PER_BODY + u) for u in slots]
                done = [meta(i - CHUNKS_PER_BODY + u) for u in slots]
                rows = [chunk_queries(off) for off, _, _ in nxt]
                for u in slots:
                    finish(u, *done[u])
                for u in slots:
                    probs(u)
                for u in slots:
                    s_ref[u] = scores(rows[u], nxt[u][1])
                return 0
            lax.fori_loop(0, (n + CHUNKS_PER_BODY - 1) // CHUNKS_PER_BODY + 1, chunk_group, 0)

    def combine(pos_ref):
        step = t - (NSTEPS + 1)

        n_sub = STEP_ROWS // MB_BLOCK
        for jj in range(n_sub):
            j = step * n_sub + jj
            rs = slice(jj * MB_BLOCK, (jj + 1) * MB_BLOCK)
            win = pl.ds(pl.multiple_of(jnp.maximum(j - 1, 0) * MB_BLOCK, MB_BLOCK), 2 * MB_BLOCK)
            s = lax.dot_general(queries(qf_ref[rs, :]), k_ref[win, :], contract_last,
                                preferred_element_type=F32)
            picked = jnp.sum(jnp.where(lane == head, prev_ref[rs, :], 0.0),
                             axis=1, keepdims=True) > 0.0
            bias_lo, bias_hi = bprev_ref[...], bown_ref[...]
            if jj == 0:
                first = step == 0
                bias_lo = jnp.where(first, bown_ref[...], bias_lo)
                bias_hi = jnp.where(first, NEG_INF, bias_hi)
                picked = jnp.logical_or(picked, first)
            s_lo = jnp.where(picked, s[:, :MB_BLOCK] + bias_lo, NEG_INF)
            s_hi = s[:, MB_BLOCK:] + bias_hi
            m_d = jnp.maximum(jnp.max(s_lo, axis=1, keepdims=True), jnp.max(s_hi, axis=1, keepdims=True))
            p = jnp.concatenate([jnp.exp(s_lo - m_d), jnp.exp(s_hi - m_d)], axis=1).astype(BF16)
            acc = jnp.dot(p, values(v_ref[win, :]), preferred_element_type=F32)

            for a in range(jj * MB_BLOCK, (jj + 1) * MB_BLOCK):
                for r in range(MB_TOPK):
                    stage_ref[r, a:a + 1, :] = buf_ref[pl.ds(pos_ref[r * STEP_ROWS + a], 1), :]

            parts = [stage_ref[r, rs, :] for r in range(MB_TOPK)]
            ms = [jnp.sum(jnp.where(is_m_lane, part, 0.0), axis=1, keepdims=True) for part in parts]
            m_all = m_d
            for m_r in ms:
                m_all = jnp.maximum(m_all, m_r)
            total = jnp.exp(m_d - m_all) * acc
            for m_r, part in zip(ms, parts):
                total = total + jnp.exp(m_r - m_all) * part
            den = jnp.sum(jnp.where(is_l_lane, total, 0.0), axis=1, keepdims=True)
            out = (total / den).astype(BF16)
            hold_rows = pl.ds(pl.multiple_of(j * MB_BLOCK, MB_BLOCK), MB_BLOCK)
            take_parked = jnp.logical_and(lane < MB_HEAD_DIM, hh == 1)
            merged = jnp.where(take_parked, hold_ref[hold_rows, :], out)
            hold_ref[hold_rows, :] = merged
            o_ref[rs, :] = merged

    steps_per_head = 2 * NSTEPS
    use = head * steps_per_head + jnp.where(t < NSTEPS, t, t - 1)
    pos_bufs = (pos_even_ref, pos_odd_ref)

    def pos_copy(u, parity):
        block = (u // steps_per_head) * NSTEPS + u % NSTEPS
        src = pos_hbm.at[pl.ds(pl.multiple_of(block * POS_BLOCK, POS_BLOCK), POS_BLOCK)]
        return pltpu.make_async_copy(src, pos_bufs[parity], pos_sem.at[parity])

    for parity in range(2):
        @pl.when(jnp.logical_and(t != NSTEPS, use % 2 == parity))
        def _positions_step(parity=parity):
            @pl.when(use == 0)
            def _first():
                pos_copy(use, parity).start()

            @pl.when(use + 1 < MB_HEADS * steps_per_head)
            def _prefetch():
                pos_copy(use + 1, 1 - parity).start()

            pos_copy(use, parity).wait()

            @pl.when(t < NSTEPS)
            def _dispatch():
                dispatch(pos_bufs[parity])

            @pl.when(t > NSTEPS)
            def _combine():
                combine(pos_bufs[parity])


def _moba_attention(chunk_meta, c31, qf, pos, k, v, prevc, bias_own, bias_prev):
    def seq_step(t):
        return jnp.where(t < NSTEPS, t, jnp.maximum(t - (NSTEPS + 1), 0) + jnp.where(t == NSTEPS, NSTEPS - 1, 0))

    def combine_step(t):
        return jnp.maximum(t - (NSTEPS + 1), 0)

    pair_resident = pl.BlockSpec((SEQ, LANES), lambda h, t, *_: (0, h // 2), pipeline_mode=pl.Buffered(1))
    bias = pl.BlockSpec((None, MB_BLOCK, MB_BLOCK), lambda h, t, *_: (h, 0, 0))
    return pl.pallas_call(
        _moba_kernel,
        grid_spec=pltpu.PrefetchScalarGridSpec(
            num_scalar_prefetch=5,
            grid=(MB_HEADS, 2 * NSTEPS + 1),
            in_specs=[
                pl.BlockSpec((STEP_ROWS, LANES), lambda h, t, *_: (seq_step(t), h // 2)),
                pl.BlockSpec(memory_space=pl.ANY),
                pair_resident, pair_resident,
                pl.BlockSpec((STEP_ROWS, LANES), lambda h, t, *_: (combine_step(t), 0)),
                bias, bias,
            ],
            out_specs=pl.BlockSpec(
                (STEP_ROWS, LANES),
                lambda h, t, *_: (jnp.where(h % 2 == 1, combine_step(t), 0), h // 2)),
            scratch_shapes=[pltpu.VMEM((BUF_ROWS, LANES), F32),
                            pltpu.VMEM((MB_TOPK, STEP_ROWS, LANES), F32),
                            pltpu.VMEM((SEQ, LANES), BF16),
                            pltpu.VMEM((CHUNKS_PER_BODY, MB_BLOCK, MB_BLOCK), F32),
                            pltpu.VMEM((CHUNKS_PER_BODY, MB_BLOCK, MB_BLOCK), BF16),
                            pltpu.VMEM((CHUNKS_PER_BODY, MB_BLOCK, LANES), F32),
                            pltpu.SMEM((POS_BLOCK,), jnp.int32), pltpu.SMEM((POS_BLOCK,), jnp.int32),
                            pltpu.SemaphoreType.DMA((2,))],
        ),
        out_shape=jax.ShapeDtypeStruct((SEQ, D_MODEL), BF16),
        compiler_params=_params("arbitrary", "arbitrary"),
        name="moba_attention",
    )(*chunk_meta, c31, qf, pos, k, v, prevc, bias_own, bias_prev)


def _t5_bucket_np(rel):
    n = np.maximum(rel, 0)
    nf = np.maximum(n, T5_MAX_EXACT).astype(np.float32)
    large = T5_MAX_EXACT + (np.log(nf / np.float32(T5_MAX_EXACT))
                            / np.float32(math.log(T5_MAX_DISTANCE / T5_MAX_EXACT))
                            * np.float32(T5_NUM_BUCKETS - T5_MAX_EXACT)).astype(np.int32)
    large = np.minimum(large, T5_NUM_BUCKETS - 1)
    return np.where(n < T5_MAX_EXACT, n, large)


def _moba_bias_tables(rel_bias):
    a = np.arange(MB_BLOCK)
    rel_own = a[:, None] - a[None, :]
    table = rel_bias.astype(F32).T

    def expand(rel):
        bucket = jnp.asarray(_t5_bucket_np(rel).reshape(-1).astype(np.int32))
        onehot = (bucket[None, :] == jnp.arange(T5_NUM_BUCKETS)[:, None]).astype(F32)
        return jnp.dot(table, onehot, precision=lax.Precision.HIGHEST).reshape(
            MB_HEADS, MB_BLOCK, MB_BLOCK)

    b_own = jnp.where(jnp.asarray(rel_own >= 0)[None], expand(rel_own), NEG_INF)
    return b_own, expand(rel_own + MB_BLOCK), table[:, T5_NUM_BUCKETS - 1]


def _chunk_tables(cnt, start):
    nch = (cnt + MB_BLOCK - 1) // MB_BLOCK
    cum = jnp.cumsum(nch, axis=1)
    i = jnp.arange(MAX_CHUNKS)
    blk = jnp.sum(cum[:, None, :] <= i[None, :, None], axis=2)
    blk = jnp.minimum(blk, N_KBLOCKS - 1)
    onehot = blk[:, :, None] == jnp.arange(N_KBLOCKS)[None, None, :]

    def pick(x):
        return jnp.sum(jnp.where(onehot, x[:, None, :], 0), axis=2)

    def flat(x):
        return x.reshape(-1).astype(jnp.int32)
    within = i[None, :] - pick(cum - nch)
    off = pick(start) + within * MB_BLOCK
    valid = jnp.clip(pick(cnt) - within * MB_BLOCK, 0, MB_BLOCK)
    return flat(off), flat(blk), flat(valid), flat(cum[:, -1])


def _moba(xs, vec, w_qkv, rel_bias):
    qf, k, v, selt, prevt, cnt = _qkv(xs, vec, w_qkv)
    cnt = cnt[:, 0].astype(jnp.int32).reshape(MB_HEADS, N_KBLOCKS)
    padded = (cnt + ROUTE_ALIGN - 1) // ROUTE_ALIGN * ROUTE_ALIGN
    start = jnp.cumsum(padded, axis=1) - padded
    base = jnp.broadcast_to(start.reshape(-1, 1).astype(F32), (MB_HEADS * N_KBLOCKS, LANES))
    pos = _route(selt, base).reshape(MB_HEADS, MB_TOPK, NSTEPS, STEP_ROWS)
    pos = pos.transpose(0, 2, 1, 3).reshape(-1)
    b_own, b_prev, c31 = _moba_bias_tables(rel_bias)
    prevc = jnp.pad(prevt.T, ((0, 0), (0, LANES - MB_HEADS)))
    a = _moba_attention(_chunk_tables(cnt, start), c31, qf, pos, k, v, prevc, b_own, b_prev)
    return a


def _vec(norm_g, mod3):
    return jnp.concatenate([norm_g[None, :], mod3, jnp.zeros((4, D_MODEL), F32)], axis=0)


def kernel(x, c, rel_bias, mod_w, mod_b, norm_g, ffn_w_in, ffn_w_out, gmlp_w_in, gmlp_v_norm,
           gmlp_w_s, gmlp_b_s, gmlp_w_out, moba_w_qkv, moba_w_o, final_norm):
    assert x.shape == (1, SEQ, D_MODEL)
    xs = x.reshape(SEQ, D_MODEL)
    mod = _modulation(c.reshape(D_MODEL, 1), mod_w, mod_b).reshape(DEPTH, N_SUBLAYERS, 3, D_MODEL)
    fin = final_norm.reshape(1, D_MODEL)
    w_in_all, w_out_all = ffn_w_in.astype(BF16), ffn_w_out.astype(BF16)

    for i in range(DEPTH):
        li = i // 2
        xs = _ffn(xs, _vec(norm_g[i, 0], mod[i, 0]), w_in_all, w_out_all, i, 0, fin, False)
        vec = _vec(norm_g[i, 1], mod[i, 1])
        if i % 2 == 0:
            bias = jnp.repeat(gmlp_b_s[li].T, GM_GROUP_DIM, axis=1)
            xs = _gmlp(xs, vec, gmlp_w_in[li].astype(BF16), gmlp_v_norm[li].reshape(1, GM_D),
                       gmlp_w_s[li].astype(BF16), bias, gmlp_w_out[li].astype(BF16))
            mixer = None
        else:
            mixer = (_moba(xs, vec, moba_w_qkv[li].astype(BF16), rel_bias), vec,
                     moba_w_o[li].astype(BF16))
        xs = _ffn(xs, _vec(norm_g[i, 2], mod[i, 2]), w_in_all, w_out_all, i, 1, fin, i == DEPTH - 1,
                  mixer)
    return xs.reshape(1, SEQ, D_MODEL)
```

```python
import functools
import math

import numpy as np
import jax
import jax.numpy as jnp
from jax import lax
from jax.experimental import pallas as pl
from jax.experimental.pallas import tpu as pltpu

F32 = jnp.float32
BF16 = jnp.bfloat16

D_MODEL = 1024
SEQ = 16384
DEPTH = 2
N_SUBLAYERS = 3
RMS_EPS = 1e-6
D_FF = 2816

GM_D = 3 * D_MODEL
GM_GROUPS = 16
GM_GROUP_DIM = GM_D // GM_GROUPS
GM_CHUNK = 128
GM_PAIR = 2 * GM_GROUP_DIM

MB_HEADS = 16
MB_HEAD_DIM = 64
MB_BLOCK = 256
MB_TOPK = 3
MB_SCALE = MB_HEAD_DIM ** -0.5
N_KBLOCKS = SEQ // MB_BLOCK

T5_NUM_BUCKETS = 32
T5_MAX_EXACT = 16
T5_MAX_DISTANCE = 128
NEG_INF = -1e30
BELOW_NEG_INF = -3e38

LANES = 128
VMEM_LIMIT = 56 * 1024 * 1024

MOD_TN = 1536
FFN_TM = 1024
FFN_MIXED_TM = 512
FFN_CHUNK = 256
GM_TM = 512
GM_VCHUNK = 768
QKV_TM = 1024

ROUTE_T = 512
ROUTE_ALIGN = 8
STEP_ROWS = 2048
NSTEPS = SEQ // STEP_ROWS
BUF_DATA_ROWS = MB_TOPK * SEQ + N_KBLOCKS * ROUTE_ALIGN
BUF_ROWS = BUF_DATA_ROWS + 3 * MB_BLOCK
NULL_ROW = BUF_ROWS - ROUTE_ALIGN
ROUTE_NONE = 1e9
MAX_CHUNKS = MB_TOPK * SEQ // MB_BLOCK + N_KBLOCKS
POS_BLOCK = MB_TOPK * STEP_ROWS
CHUNKS_PER_BODY = 8


def _resident(shape):
    nd = len(shape)
    return pl.BlockSpec(shape, lambda *_: (0,) * nd, pipeline_mode=pl.Buffered(1))


def _params(*sem):
    return pltpu.CompilerParams(dimension_semantics=sem, vmem_limit_bytes=VMEM_LIMIT)


def _rms(x):
    return x * lax.rsqrt(jnp.mean(x * x, axis=-1, keepdims=True) + RMS_EPS)


def _adaln(x, vec_ref):
    y = _rms(x) * vec_ref[0:1, :]
    return y * (1.0 + vec_ref[2:3, :]) + vec_ref[1:2, :]


def _gelu(x):
    return 0.5 * x * (1.0 + lax.erf(x * np.float32(math.sqrt(0.5))))


def _mod_kernel(c_ref, w_ref, b_ref, o_ref):
    c = c_ref[...]
    c_act = c * jax.nn.sigmoid(c)
    o_ref[...] = jnp.sum(c_act * w_ref[...], axis=0, keepdims=True) + b_ref[...]


def _modulation(c_col, mod_w, mod_b):
    n = mod_w.shape[-1]
    return pl.pallas_call(
        _mod_kernel,
        grid=(DEPTH, n // MOD_TN),
        in_specs=[
            pl.BlockSpec((D_MODEL, 1), lambda i, j: (0, 0)),
            pl.BlockSpec((None, D_MODEL, MOD_TN), lambda i, j: (i, 0, j)),
            pl.BlockSpec((None, 1, MOD_TN), lambda i, j: (i, 0, j)),
        ],
        out_specs=pl.BlockSpec((None, 1, MOD_TN), lambda i, j: (i, 0, j)),
        out_shape=jax.ShapeDtypeStruct((DEPTH, 1, n), F32),
        compiler_params=_params("arbitrary", "arbitrary"),
        name="modulation",
    )(c_col, mod_w, mod_b.reshape(DEPTH, 1, n))


def _ffn_kernel(*refs, final, mixed):
    if mixed:
        x_ref, a_ref, mixvec_ref, wo_ref, vec_ref, win_ref, wout_ref, fin_ref, o_ref, act_ref = refs
        x = x_ref[...] + mixvec_ref[3:4, :] * jnp.dot(a_ref[...], wo_ref[...],
                                                      preferred_element_type=F32)
    else:
        x_ref, vec_ref, win_ref, wout_ref, fin_ref, o_ref, act_ref = refs
        x = x_ref[...]
    h = _adaln(x, vec_ref).astype(BF16)
    for c in range(D_FF // FFN_CHUNK):
        lo = c * FFN_CHUNK
        g = jnp.dot(h, win_ref[:, lo:lo + FFN_CHUNK], preferred_element_type=F32)
        u = jnp.dot(h, win_ref[:, D_FF + lo:D_FF + lo + FFN_CHUNK], preferred_element_type=F32)
        act_ref[:, lo:lo + FFN_CHUNK] = (g * jax.nn.sigmoid(g) * u).astype(BF16)
    y = jnp.dot(act_ref[...], wout_ref[...], preferred_element_type=F32)
    out = x + (0.5 * vec_ref[3:4, :]) * y
    if final:
        out = _rms(out) * fin_ref[...]
    o_ref[...] = out


def _ffn(x, vec, w_in_all, w_out_all, layer, which, fin, final, mixer=None):
    def one_matrix(rows, cols):
        return pl.BlockSpec((None, None, rows, cols), lambda i: (layer, which, 0, 0),
                            pipeline_mode=pl.Buffered(1))
    mixed = mixer is not None
    tm = FFN_MIXED_TM if mixed else FFN_TM
    row = pl.BlockSpec((tm, D_MODEL), lambda i: (i, 0))
    mix_specs = [row, _resident((8, D_MODEL)), _resident((D_MODEL, D_MODEL))] if mixed else []
    return pl.pallas_call(
        functools.partial(_ffn_kernel, final=final, mixed=mixed),
        grid=(SEQ // tm,),
        in_specs=[row, *mix_specs,
                  _resident((8, D_MODEL)),
                  one_matrix(D_MODEL, 2 * D_FF),
                  one_matrix(D_FF, D_MODEL),
                  _resident((1, D_MODEL))],
        out_specs=row,
        out_shape=jax.ShapeDtypeStruct((SEQ, D_MODEL), F32),
        scratch_shapes=[pltpu.VMEM((tm, D_FF), BF16)],
        compiler_params=_params("parallel"),
        name="ffn_final" if final else "ffn",
    )(x, *(mixer or ()), vec, w_in_all, w_out_all, fin)


def _gmlp_kernel(x_ref, vec_ref, win_ref, vnorm_ref, ws_ref, bias_ref, wout_ref,
                 o_ref, v_ref, g_ref):
    x = x_ref[...]
    h = _adaln(x, vec_ref).astype(BF16)

    ssq = jnp.zeros((GM_TM, 1), F32)
    for c in range(GM_D // GM_VCHUNK):
        lo = c * GM_VCHUNK
        v = _gelu(jnp.dot(h, win_ref[:, GM_D + lo:GM_D + lo + GM_VCHUNK],
                          preferred_element_type=F32))
        v_ref[:, lo:lo + GM_VCHUNK] = v
        ssq = ssq + jnp.sum(v * v, axis=-1, keepdims=True)
    r = lax.rsqrt(ssq * (1.0 / GM_D) + RMS_EPS)

    row = lax.broadcasted_iota(jnp.int32, (GM_CHUNK, GM_CHUNK), 0)
    col = lax.broadcasted_iota(jnp.int32, (GM_CHUNK, GM_CHUNK), 1)
    causal = row >= col
    low_half = lax.broadcasted_iota(jnp.int32, (GM_CHUNK, LANES), 1) < (GM_GROUP_DIM - LANES)

    for p in range(GM_GROUPS // 2):
        lo = p * GM_PAIR
        vn = (v_ref[:, lo:lo + GM_PAIR] * r * vnorm_ref[:, lo:lo + GM_PAIR]).astype(BF16)
        u = _gelu(jnp.dot(h, win_ref[:, lo:lo + GM_PAIR], preferred_element_type=F32))
        w0 = jnp.where(causal, ws_ref[2 * p], jnp.zeros((), BF16))
        w1 = jnp.where(causal, ws_ref[2 * p + 1], jnp.zeros((), BF16))
        for c in range(GM_TM // GM_CHUNK):
            rows = slice(c * GM_CHUNK, (c + 1) * GM_CHUNK)
            vc = vn[rows]
            a = jnp.dot(w0, vc[:, :2 * LANES], preferred_element_type=F32)
            b = jnp.dot(w1, vc[:, LANES:], preferred_element_type=F32)
            mid = jnp.where(low_half, a[:, LANES:], b[:, :LANES])
            sv = jnp.concatenate([a[:, :LANES], mid, b[:, LANES:]], axis=1)
            sv = sv + bias_ref[:, lo:lo + GM_PAIR]
            g_ref[rows, lo:lo + GM_PAIR] = (u[rows] * sv).astype(BF16)

    y = jnp.dot(g_ref[...], wout_ref[...], preferred_element_type=F32)
    o_ref[...] = x + vec_ref[3:4, :] * y


def _gmlp(x, vec, w_in, v_norm, w_s, bias, w_out):
    return pl.pallas_call(
        _gmlp_kernel,
        grid=(SEQ // GM_TM,),
        in_specs=[
            pl.BlockSpec((GM_TM, D_MODEL), lambda i: (i, 0)),
            _resident((8, D_MODEL)),
            _resident((D_MODEL, 2 * GM_D)),
            _resident((1, GM_D)),
            _resident((GM_GROUPS, GM_CHUNK, GM_CHUNK)),
            _resident((GM_CHUNK, GM_D)),
            _resident((GM_D, D_MODEL)),
        ],
        out_specs=pl.BlockSpec((GM_TM, D_MODEL), lambda i: (i, 0)),
        out_shape=jax.ShapeDtypeStruct((SEQ, D_MODEL), F32),
        scratch_shapes=[pltpu.VMEM((GM_TM, GM_D), F32), pltpu.VMEM((GM_TM, GM_D), BF16)],
        compiler_params=_params("parallel"),
        name="gmlp",
    )(x, vec, w_in, v_norm, w_s, bias, w_out)


def _qkv_kernel(x_ref, vec_ref, w_ref, q_ref, k_ref, v_ref, selt_ref, prevt_ref, cnt_ref,
                kmean_ref):
    i = pl.program_id(0)

    @pl.when(i == 0)
    def _init():
        kmean_ref[...] = jnp.zeros_like(kmean_ref)
        cnt_ref[...] = jnp.zeros_like(cnt_ref)

    h = _adaln(x_ref[...], vec_ref).astype(BF16)
    q = jnp.dot(h, w_ref[:, :D_MODEL], preferred_element_type=F32) * MB_SCALE
    q_ref[...] = q
    k = jnp.dot(h, w_ref[:, D_MODEL:2 * D_MODEL], preferred_element_type=F32)
    k_ref[...] = k.astype(BF16)
    nb = QKV_TM // MB_BLOCK
    for b in range(nb):
        kmean_ref[pl.ds(nb * i + b, 1), :] = jnp.mean(
            k[b * MB_BLOCK:(b + 1) * MB_BLOCK], axis=0, keepdims=True)
    v = jnp.dot(h, w_ref[:, 2 * D_MODEL:], preferred_element_type=F32)
    v_ref[...] = v.astype(BF16)

    lane = lax.broadcasted_iota(jnp.int32, (1, LANES), 1)
    blk = lax.broadcasted_iota(jnp.int32, (N_KBLOCKS, MB_BLOCK), 0)
    contract_last = (((1,), (1,)), ((), ()))
    for b in range(nb):
        j = nb * i + b
        cols = slice(b * MB_BLOCK, (b + 1) * MB_BLOCK)
        for pair in range(MB_HEADS // 2):
            km = kmean_ref[:, pair * LANES:(pair + 1) * LANES]
            qp = q[cols, pair * LANES:(pair + 1) * LANES]
            for hh in range(2):
                head = 2 * pair + hh
                km_h = jnp.where((lane // MB_HEAD_DIM) == hh, km, 0.0)
                g = lax.dot_general(km_h, qp, contract_last, preferred_element_type=F32)
                g = jnp.where(blk < j, g, NEG_INF)
                sel = jnp.zeros((N_KBLOCKS, MB_BLOCK), F32)
                for r in range(MB_TOPK):
                    m = jnp.max(g, axis=0, keepdims=True)
                    first = jnp.min(jnp.where(g == m, blk, N_KBLOCKS), axis=0, keepdims=True)
                    hit = blk == first
                    sel = jnp.where(jnp.logical_and(hit, r < j), 1.0, sel)
                    g = jnp.where(hit, BELOW_NEG_INF, g)
                prevt_ref[head:head + 1, cols] = jnp.max(
                    jnp.where(blk == j - 1, sel, 0.0), axis=0, keepdims=True)
                routed = jnp.where(blk < j - 1, sel, 0.0)
                selt_ref[head * N_KBLOCKS:(head + 1) * N_KBLOCKS, cols] = routed.astype(BF16)
                rows = slice(head * N_KBLOCKS, (head + 1) * N_KBLOCKS)
                cnt_ref[rows, :] = cnt_ref[rows, :] + jnp.sum(routed, axis=1, keepdims=True)


def _qkv(x, vec, w_qkv):
    row = pl.BlockSpec((QKV_TM, D_MODEL), lambda i: (i, 0))
    n_items = MB_HEADS * N_KBLOCKS
    return pl.pallas_call(
        _qkv_kernel,
        grid=(SEQ // QKV_TM,),
        in_specs=[row, _resident((8, D_MODEL)), _resident((D_MODEL, 3 * D_MODEL))],
        out_specs=[row, row, row,
                   pl.BlockSpec((n_items, QKV_TM), lambda i: (0, i)),
                   pl.BlockSpec((MB_HEADS, QKV_TM), lambda i: (0, i)),
                   pl.BlockSpec((n_items, LANES), lambda i: (0, 0))],
        out_shape=[jax.ShapeDtypeStruct((SEQ, D_MODEL), F32),
                   jax.ShapeDtypeStruct((SEQ, D_MODEL), BF16),
                   jax.ShapeDtypeStruct((SEQ, D_MODEL), BF16),
                   jax.ShapeDtypeStruct((n_items, SEQ), BF16),
                   jax.ShapeDtypeStruct((MB_HEADS, SEQ), F32),
                   jax.ShapeDtypeStruct((n_items, LANES), F32)],
        scratch_shapes=[pltpu.VMEM((N_KBLOCKS, D_MODEL), F32)],
        compiler_params=_params("arbitrary"),
        name="moba_qkv_gate",
    )(x, vec, w_qkv)


def _route_kernel(selt_ref, base_ref, pos_ref, carry_ref):
    i = pl.program_id(0)

    @pl.when(i == 0)
    def _init():
        carry_ref[...] = jnp.zeros_like(carry_ref)

    sel = selt_ref[...]
    before = (lax.broadcasted_iota(jnp.int32, (ROUTE_T, ROUTE_T), 0)
              < lax.broadcasted_iota(jnp.int32, (ROUTE_T, ROUTE_T), 1))
    rank = jnp.dot(sel, jnp.where(before, 1.0, 0.0).astype(BF16), preferred_element_type=F32)
    offset = base_ref[...] + carry_ref[...]
    rank = rank + jnp.concatenate([offset] * (ROUTE_T // LANES), axis=1)
    carry_ref[...] = carry_ref[...] + jnp.sum(sel.astype(F32), axis=1, keepdims=True)
    val = jnp.where(sel > 0, rank, ROUTE_NONE)
    for head in range(MB_HEADS):
        vh = val[head * N_KBLOCKS:(head + 1) * N_KBLOCKS]
        for r in range(MB_TOPK):
            m = jnp.min(vh, axis=0, keepdims=True)
            slot = head * MB_TOPK + r
            pos_ref[slot:slot + 1, :] = jnp.where(m < ROUTE_NONE, m, float(NULL_ROW)).astype(jnp.int32)
            vh = jnp.where(vh == m, ROUTE_NONE, vh)


def _route(selt, base):
    n_items = MB_HEADS * N_KBLOCKS
    return pl.pallas_call(
        _route_kernel,
        grid=(SEQ // ROUTE_T,),
        in_specs=[pl.BlockSpec((n_items, ROUTE_T), lambda i: (0, i)),
                  pl.BlockSpec((n_items, LANES), lambda i: (0, 0))],
        out_specs=pl.BlockSpec((MB_HEADS * MB_TOPK, ROUTE_T), lambda i: (0, i)),
        out_shape=jax.ShapeDtypeStruct((MB_HEADS * MB_TOPK, SEQ), jnp.int32),
        scratch_shapes=[pltpu.VMEM((n_items, LANES), F32)],
        compiler_params=_params("arbitrary"),
        name="moba_route",
    )(selt, base)


def _moba_kernel(choff_ref, chblk_ref, chvalid_ref, nch_ref, c31_ref,
                 qf_ref, pos_hbm, k_ref, v_ref, prev_ref, bown_ref, bprev_ref,
                 o_ref, buf_ref, stage_ref, hold_ref, s_ref, p_ref, m_ref,
                 pos_even_ref, pos_odd_ref, pos_sem):
    head = pl.program_id(0)
    t = pl.program_id(1)
    hh = head % 2
    lane = lax.broadcasted_iota(jnp.int32, (1, LANES), 1)
    in_head = (lane // MB_HEAD_DIM) == hh
    stat0 = (1 - hh) * MB_HEAD_DIM
    is_m_lane = lane == stat0
    is_l_lane = lane == stat0 + 1
    contract_last = (((1,), (1,)), ((), ()))
    ones_bf = jnp.ones((), BF16)

    def queries(rows):
        return jnp.where(in_head, rows, 0.0).astype(BF16)

    def values(rows):
        return jnp.where(in_head, rows, ones_bf)

    @pl.when(jnp.logical_and(t == 0, head == 0))
    def _init():
        def zero(c, _):
            buf_ref[pl.ds(pl.multiple_of(c * MB_BLOCK, MB_BLOCK), MB_BLOCK), :] = (
                jnp.zeros((MB_BLOCK, LANES), F32))
            return 0
        lax.fori_loop(0, BUF_ROWS // MB_BLOCK, zero, 0)

        def zero_hold(c, _):
            hold_ref[pl.ds(pl.multiple_of(c * MB_BLOCK, MB_BLOCK), MB_BLOCK), :] = (
                jnp.zeros((MB_BLOCK, LANES), BF16))
            return 0
        lax.fori_loop(0, SEQ // MB_BLOCK, zero_hold, 0)

    def dispatch(pos_ref):
        for a in range(STEP_ROWS):
            row = qf_ref[a:a + 1, :]
            for r in range(MB_TOPK):
                buf_ref[pl.ds(pos_ref[r * STEP_ROWS + a], 1), :] = row

    @pl.when(t == NSTEPS)
    def _routed():
        buf_ref[NULL_ROW:NULL_ROW + ROUTE_ALIGN, :] = jnp.where(
            is_m_lane, NEG_INF, jnp.zeros((ROUTE_ALIGN, LANES), F32))
        n = nch_ref[head]
        far_bias = c31_ref[head]
        row_id = lax.broadcasted_iota(jnp.int32, (MB_BLOCK, LANES), 0)

        def meta(i):
            idx = head * MAX_CHUNKS + jnp.clip(i, 0, n - 1)
            off = pl.multiple_of(choff_ref[idx], ROUTE_ALIGN)
            blk = pl.multiple_of(chblk_ref[idx] * MB_BLOCK, MB_BLOCK)
            return off, blk, jnp.where(jnp.logical_and(i >= 0, i < n), chvalid_ref[idx], 0)

        def chunk_queries(off):
            return queries(buf_ref[pl.ds(off, MB_BLOCK), :])

        def scores(qh, blk):
            return lax.dot_general(qh, k_ref[pl.ds(blk, MB_BLOCK), :], contract_last,
                                   preferred_element_type=F32)

        def probs(u):
            s = s_ref[u]
            m = jnp.max(s, axis=1, keepdims=True)
            p_ref[u] = jnp.exp(s - m).astype(BF16)
            m_ref[u] = jnp.broadcast_to(m, (MB_BLOCK, LANES))

        def finish(u, off, blk, valid):
            acc = jnp.dot(p_ref[u], values(v_ref[pl.ds(blk, MB_BLOCK), :]), preferred_element_type=F32)
            new = jnp.where(is_m_lane, m_ref[u] + far_bias, acc)
            pltpu.store(buf_ref.at[pl.ds(off, MB_BLOCK), :], new, mask=row_id < valid)

        @pl.when(n > 0)
        def _chunks():
            slots = range(CHUNKS_PER_BODY)
            p_ref[...] = jnp.zeros_like(p_ref)
            m_ref[...] = jnp.zeros_like(m_ref)
            for u in slots:
                off, blk, _ = meta(u)
                s_ref[u] = scores(chunk_queries(off), blk)

            def chunk_group(c, _):
                i = CHUNKS_PER_BODY * c
                nxt = [meta(i + CHUNKS_PER_BODY + u) for u in slots]
                done = [meta(i - CHUNKS_PER_BODY + u) for u in slots]
                rows = [chunk_queries(off) for off, _, _ in nxt]
                for u in slots:
                    finish(u, *done[u])
                for u in slots:
                    probs(u)
                for u in slots:
                    s_ref[u] = scores(rows[u], nxt[u][1])
                return 0
            lax.fori_loop(0, (n + CHUNKS_PER_BODY - 1) // CHUNKS_PER_BODY + 1, chunk_group, 0)

    def combine(pos_ref):
        step = t - (NSTEPS + 1)

        n_sub = STEP_ROWS // MB_BLOCK
        for jj in range(n_sub):
            j = step * n_sub + jj
            rs = slice(jj * MB_BLOCK, (jj + 1) * MB_BLOCK)
            win = pl.ds(pl.multiple_of(jnp.maximum(j - 1, 0) * MB_BLOCK, MB_BLOCK), 2 * MB_BLOCK)
            s = lax.dot_general(queries(qf_ref[rs, :]), k_ref[win, :], contract_last,
                                preferred_element_type=F32)
            picked = jnp.sum(jnp.where(lane == head, prev_ref[rs, :], 0.0),
                             axis=1, keepdims=True) > 0.0
            bias_lo, bias_hi = bprev_ref[...], bown_ref[...]
            if jj == 0:
                first = step == 0
                bias_lo = jnp.where(first, bown_ref[...], bias_lo)
                bias_hi = jnp.where(first, NEG_INF, bias_hi)
                picked = jnp.logical_or(picked, first)
            s_lo = jnp.where(picked, s[:, :MB_BLOCK] + bias_lo, NEG_INF)
            s_hi = s[:, MB_BLOCK:] + bias_hi
            m_d = jnp.maximum(jnp.max(s_lo, axis=1, keepdims=True), jnp.max(s_hi, axis=1, keepdims=True))
            p = jnp.concatenate([jnp.exp(s_lo - m_d), jnp.exp(s_hi - m_d)], axis=1).astype(BF16)
            acc = jnp.dot(p, values(v_ref[win, :]), preferred_element_type=F32)

            for a in range(jj * MB_BLOCK, (jj + 1) * MB_BLOCK):
                for r in range(MB_TOPK):
                    stage_ref[r, a:a + 1, :] = buf_ref[pl.ds(pos_ref[r * STEP_ROWS + a], 1), :]

            parts = [stage_ref[r, rs, :] for r in range(MB_TOPK)]
            ms = [jnp.sum(jnp.where(is_m_lane, part, 0.0), axis=1, keepdims=True) for part in parts]
            m_all = m_d
            for m_r in ms:
                m_all = jnp.maximum(m_all, m_r)
            total = jnp.exp(m_d - m_all) * acc
            for m_r, part in zip(ms, parts):
                total = total + jnp.exp(m_r - m_all) * part
            den = jnp.sum(jnp.where(is_l_lane, total, 0.0), axis=1, keepdims=True)
            out = (total / den).astype(BF16)
            hold_rows = pl.ds(pl.multiple_of(j * MB_BLOCK, MB_BLOCK), MB_BLOCK)
            take_parked = jnp.logical_and(lane < MB_HEAD_DIM, hh == 1)
            merged = jnp.where(take_parked, hold_ref[hold_rows, :], out)
            hold_ref[hold_rows, :] = merged
            o_ref[rs, :] = merged

    steps_per_head = 2 * NSTEPS
    use = head * steps_per_head + jnp.where(t < NSTEPS, t, t - 1)
    pos_bufs = (pos_even_ref, pos_odd_ref)

    def pos_copy(u, parity):
        block = (u // steps_per_head) * NSTEPS + u % NSTEPS
        src = pos_hbm.at[pl.ds(pl.multiple_of(block * POS_BLOCK, POS_BLOCK), POS_BLOCK)]
        return pltpu.make_async_copy(src, pos_bufs[parity], pos_sem.at[parity])

    for parity in range(2):
        @pl.when(jnp.logical_and(t != NSTEPS, use % 2 == parity))
        def _positions_step(parity=parity):
            @pl.when(use == 0)
            def _first():
                pos_copy(use, parity).start()

            @pl.when(use + 1 < MB_HEADS * steps_per_head)
            def _prefetch():
                pos_copy(use + 1, 1 - parity).start()

            pos_copy(use, parity).wait()

            @pl.when(t < NSTEPS)
            def _dispatch():
                dispatch(pos_bufs[parity])

            @pl.when(t > NSTEPS)
            def _combine():
                combine(pos_bufs[parity])


def _moba_attention(chunk_meta, c31, qf, pos, k, v, prevc, bias_own, bias_prev):
    def seq_step(t):
        return jnp.where(t < NSTEPS, t, jnp.maximum(t - (NSTEPS + 1), 0) + jnp.where(t == NSTEPS, NSTEPS - 1, 0))

    def combine_step(t):
        return jnp.maximum(t - (NSTEPS + 1), 0)

    pair_resident = pl.BlockSpec((SEQ, LANES), lambda h, t, *_: (0, h // 2), pipeline_mode=pl.Buffered(1))
    bias = pl.BlockSpec((None, MB_BLOCK, MB_BLOCK), lambda h, t, *_: (h, 0, 0))
    return pl.pallas_call(
        _moba_kernel,
        grid_spec=pltpu.PrefetchScalarGridSpec(
            num_scalar_prefetch=5,
            grid=(MB_HEADS, 2 * NSTEPS + 1),
            in_specs=[
                pl.BlockSpec((STEP_ROWS, LANES), lambda h, t, *_: (seq_step(t), h // 2)),
                pl.BlockSpec(memory_space=pl.ANY),
                pair_resident, pair_resident,
                pl.BlockSpec((STEP_ROWS, LANES), lambda h, t, *_: (combine_step(t), 0)),
                bias, bias,
            ],
            out_specs=pl.BlockSpec(
                (STEP_ROWS, LANES),
                lambda h, t, *_: (jnp.where(h % 2 == 1, combine_step(t), 0), h // 2)),
            scratch_shapes=[pltpu.VMEM((BUF_ROWS, LANES), F32),
                            pltpu.VMEM((MB_TOPK, STEP_ROWS, LANES), F32),
                            pltpu.VMEM((SEQ, LANES), BF16),
                            pltpu.VMEM((CHUNKS_PER_BODY, MB_BLOCK, MB_BLOCK), F32),
                            pltpu.VMEM((CHUNKS_PER_BODY, MB_BLOCK, MB_BLOCK), BF16),
                            pltpu.VMEM((CHUNKS_PER_BODY, MB_BLOCK, LANES), F32),
                            pltpu.SMEM((POS_BLOCK,), jnp.int32), pltpu.SMEM((POS_BLOCK,), jnp.int32),
                            pltpu.SemaphoreType.DMA((2,))],
        ),
        out_shape=jax.ShapeDtypeStruct((SEQ, D_MODEL), BF16),
        compiler_params=_params("arbitrary", "arbitrary"),
        name="moba_attention",
    )(*chunk_meta, c31, qf, pos, k, v, prevc, bias_own, bias_prev)


def _t5_bucket_np(rel):
    n = np.maximum(rel, 0)
    nf = np.maximum(n, T5_MAX_EXACT).astype(np.float32)
    large = T5_MAX_EXACT + (np.log(nf / np.float32(T5_MAX_EXACT))
                            / np.float32(math.log(T5_MAX_DISTANCE / T5_MAX_EXACT))
                            * np.float32(T5_NUM_BUCKETS - T5_MAX_EXACT)).astype(np.int32)
    large = np.minimum(large, T5_NUM_BUCKETS - 1)
    return np.where(n < T5_MAX_EXACT, n, large)


def _moba_bias_tables(rel_bias):
    a = np.arange(MB_BLOCK)
    rel_own = a[:, None] - a[None, :]
    table = rel_bias.astype(F32).T

    def expand(rel):
        bucket = jnp.asarray(_t5_bucket_np(rel).reshape(-1).astype(np.int32))
        onehot = (bucket[None, :] == jnp.arange(T5_NUM_BUCKETS)[:, None]).astype(F32)
        return jnp.dot(table, onehot, precision=lax.Precision.HIGHEST).reshape(
            MB_HEADS, MB_BLOCK, MB_BLOCK)

    b_own = jnp.where(jnp.asarray(rel_own >= 0)[None], expand(rel_own), NEG_INF)
    return b_own, expand(rel_own + MB_BLOCK), table[:, T5_NUM_BUCKETS - 1]


def _chunk_tables(cnt, start):
    nch = (cnt + MB_BLOCK - 1) // MB_BLOCK
    cum = jnp.cumsum(nch, axis=1)
    i = jnp.arange(MAX_CHUNKS)
    blk = jnp.sum(cum[:, None, :] <= i[None, :, None], axis=2)
    blk = jnp.minimum(blk, N_KBLOCKS - 1)
    onehot = blk[:, :, None] == jnp.arange(N_KBLOCKS)[None, None, :]

    def pick(x):
        return jnp.sum(jnp.where(onehot, x[:, None, :], 0), axis=2)

    def flat(x):
        return x.reshape(-1).astype(jnp.int32)
    within = i[None, :] - pick(cum - nch)
    off = pick(start) + within * MB_BLOCK
    valid = jnp.clip(pick(cnt) - within * MB_BLOCK, 0, MB_BLOCK)
    return flat(off), flat(blk), flat(valid), flat(cum[:, -1])


def _moba(xs, vec, w_qkv, rel_bias):
    qf, k, v, selt, prevt, cnt = _qkv(xs, vec, w_qkv)
    cnt = cnt[:, 0].astype(jnp.int32).reshape(MB_HEADS, N_KBLOCKS)
    padded = (cnt + ROUTE_ALIGN - 1) // ROUTE_ALIGN * ROUTE_ALIGN
    start = jnp.cumsum(padded, axis=1) - padded
    base = jnp.broadcast_to(start.reshape(-1, 1).astype(F32), (MB_HEADS * N_KBLOCKS, LANES))
    pos = _route(selt, base).reshape(MB_HEADS, MB_TOPK, NSTEPS, STEP_ROWS)
    pos = pos.transpose(0, 2, 1, 3).reshape(-1)
    b_own, b_prev, c31 = _moba_bias_tables(rel_bias)
    prevc = jnp.pad(prevt.T, ((0, 0), (0, LANES - MB_HEADS)))
    a = _moba_attention(_chunk_tables(cnt, start), c31, qf, pos, k, v, prevc, b_own, b_prev)
    return a


def _vec(norm_g, mod3):
    return jnp.concatenate([norm_g[None, :], mod3, jnp.zeros((4, D_MODEL), F32)], axis=0)


def kernel(x, c, rel_bias, mod_w, mod_b, norm_g, ffn_w_in, ffn_w_out, gmlp_w_in, gmlp_v_norm,
           gmlp_w_s, gmlp_b_s, gmlp_w_out, moba_w_qkv, moba_w_o, final_norm):
    assert x.shape == (1, SEQ, D_MODEL)
    xs = x.reshape(SEQ, D_MODEL)
    mod = _modulation(c.reshape(D_MODEL, 1), mod_w, mod_b).reshape(DEPTH, N_SUBLAYERS, 3, D_MODEL)
    fin = final_norm.reshape(1, D_MODEL)
    w_in_all, w_out_all = ffn_w_in.astype(BF16), ffn_w_out.astype(BF16)

    for i in range(DEPTH):
        li = i // 2
        xs = _ffn(xs, _vec(norm_g[i, 0], mod[i, 0]), w_in_all, w_out_all, i, 0, fin, False)
        vec = _vec(norm_g[i, 1], mod[i, 1])
        if i % 2 == 0:
            bias = jnp.repeat(gmlp_b_s[li].T, GM_GROUP_DIM, axis=1)
            xs = _gmlp(xs, vec, gmlp_w_in[li].astype(BF16), gmlp_v_norm[li].reshape(1, GM_D),
                       gmlp_w_s[li].astype(BF16), bias, gmlp_w_out[li].astype(BF16))
            mixer = None
        else:
            mixer = (_moba(xs, vec, moba_w_qkv[li].astype(BF16), rel_bias), vec,
                     moba_w_o[li].astype(BF16))
        xs = _ffn(xs, _vec(norm_g[i, 2], mod[i, 2]), w_in_all, w_out_all, i, 1, fin, i == DEPTH - 1,
                  mixer)
    return xs.reshape(1, SEQ, D_MODEL)
```
